```python
import math
import jax
import jax.numpy as jnp
from jax import lax
import numpy as np

D_MODEL = 1024
BATCH = 8
SEQ = 8192
DEPTH = 2
DEC_BATCH = 8
DEC_SEQ = 16
PAST_LEN = 4096

CHUNK = 64
EPS = 1e-6
GDN_HEADS = 6
GDN_DK = 64
GDN_DV = 64
GDN_WIDTH = GDN_HEADS * GDN_DV
GDN_QKV = GDN_HEADS * (2 * GDN_DK + GDN_DV)
CONV_W = 4
MLA_HEADS = 6
MLA_NOPE = 64
MLA_ROPE = 32
MLA_QK = MLA_NOPE + MLA_ROPE
MLA_V = 64
MLA_WIDTH = MLA_HEADS * MLA_V
MLA_Q_LORA = 384
MLA_KV_LORA = 256
ROPE_THETA = 10000.0
Q_BLOCK = 128
MLSTM_HEADS = 4
MLSTM_DK = 64
MLSTM_DV = 64
MLSTM_WIDTH = MLSTM_HEADS * MLSTM_DV
MLSTM_QKV = MLSTM_HEADS * (2 * MLSTM_DK + MLSTM_DV)
D_MIX = GDN_WIDTH + MLA_WIDTH + MLSTM_WIDTH
IN_SIZES = (GDN_QKV, GDN_WIDTH, GDN_HEADS, GDN_HEADS,
            MLA_Q_LORA, MLA_KV_LORA, MLA_ROPE,
            MLSTM_QKV, MLSTM_WIDTH, MLSTM_HEADS, MLSTM_HEADS)
D_IN = (GDN_QKV + GDN_WIDTH + 2 * GDN_HEADS + MLA_Q_LORA + MLA_KV_LORA + MLA_ROPE
        + MLSTM_QKV + MLSTM_WIDTH + 2 * MLSTM_HEADS)
PEER_HEADS = 8
N_KEYS = 128
N_EXPERTS = N_KEYS * N_KEYS
PEER_DKEY = 128
PEER_TOPK = 16
PEER_BLOCK = 128

kernel_name = 'hybrid_gdn_mla_mlstm_peer_stream_step'


def rmsnorm(x, g):
    xf = x.astype(jnp.float32)
    y = xf * lax.rsqrt(jnp.mean(xf * xf, axis=-1, keepdims=True) + EPS)
    return (y * g.astype(jnp.float32)).astype(x.dtype)


def l2norm(x):
    xf = x.astype(jnp.float32)
    return (xf * lax.rsqrt(jnp.sum(xf * xf, axis=-1, keepdims=True) + EPS)).astype(x.dtype)


def split_cols(x, sizes):
    out, start = [], 0
    for s in sizes:
        out.append(x[..., start:start + s])
        start += s
    return out


def causal_conv(x, buf, w):
    L = x.shape[1]
    xp = jnp.concatenate([buf.astype(x.dtype), x], axis=1)
    y = w[0] * xp[:, 0:L]
    for j in range(1, CONV_W):
        y = y + w[j] * xp[:, j:j + L]
    return y, xp[:, -(CONV_W - 1):]


def rope(x, pos):
    half = MLA_ROPE // 2
    inv = ROPE_THETA ** (-jnp.arange(half, dtype=jnp.float32) / half)
    ang = pos.astype(jnp.float32)[:, None] * inv[None, :]
    cos = jnp.cos(ang)[None, :, None, :]
    sin = jnp.sin(ang)[None, :, None, :]
    xf = x.astype(jnp.float32)
    x1, x2 = xf[..., :half], xf[..., half:]
    return jnp.concatenate([x1 * cos - x2 * sin, x2 * cos + x1 * sin], axis=-1).astype(x.dtype)


def to_chunks(t, chunk):
    B, L = t.shape[:2]
    t = t.reshape(B, L // chunk, chunk, *t.shape[2:])
    return jnp.moveaxis(jnp.moveaxis(t, 1, 0), 2, 3)


def from_chunks(o):
    o = jnp.moveaxis(jnp.moveaxis(o, 3, 2), 0, 1)
    B, n, C = o.shape[:3]
    return o.reshape(B, n * C, *o.shape[3:])


def gdn_chunked(q, k, v, g, beta, s0, chunk):
    f32 = jnp.float32
    dk = q.shape[-1]
    q = to_chunks(q.astype(f32) * dk ** -0.5, chunk)
    k = to_chunks(k.astype(f32), chunk)
    v = to_chunks(v.astype(f32), chunk)
    g = to_chunks(g.astype(f32), chunk)
    beta = to_chunks(beta.astype(f32), chunk)
    causal = jnp.tril(jnp.ones((chunk, chunk), bool))
    strict = jnp.tril(jnp.ones((chunk, chunk), bool), -1)
    G = jnp.cumsum(g, axis=-1)
    diff = G[..., :, None] - G[..., None, :]
    decay = jnp.where(causal, jnp.exp(jnp.where(causal, diff, 0.0)), 0.0)
    a = jnp.where(strict, beta[..., :, None] * jnp.einsum('nbhtd,nbhsd->nbhts', k, k) * decay, 0.0)
    m = a + jnp.eye(chunk, dtype=f32)
    u = lax.linalg.triangular_solve(m, beta[..., None] * v, left_side=True, lower=True, unit_diagonal=True)
    wk = lax.linalg.triangular_solve(m, (beta * jnp.exp(G))[..., None] * k,
                                     left_side=True, lower=True, unit_diagonal=True)
    qk = jnp.einsum('nbhtd,nbhsd->nbhts', q, k) * decay
    qg = q * jnp.exp(G)[..., None]
    kend = k * jnp.exp(G[..., -1:] - G)[..., None]
    gend = jnp.exp(G[..., -1])

    def step(s, xs):
        u_c, wk_c, qk_c, qg_c, kend_c, gend_c = xs
        w = u_c - jnp.einsum('bhsk,bhkv->bhsv', wk_c, s)
        o = jnp.einsum('bhtk,bhkv->bhtv', qg_c, s) + jnp.einsum('bhts,bhsv->bhtv', qk_c, w)
        s_new = gend_c[..., None, None] * s + jnp.einsum('bhsk,bhsv->bhkv', kend_c, w)
        return s_new, o

    s_fin, o = lax.scan(step, s0.astype(f32), (u, wk, qk, qg, kend, gend))
    return from_chunks(o), s_fin


def mlstm_chunked(q, k, v, ig, lf, c0, n0, m0, chunk):
    f32 = jnp.float32
    dk = k.shape[-1]
    q = to_chunks(q.astype(f32), chunk)
    k = to_chunks(k.astype(f32) * dk ** -0.5, chunk)
    v = to_chunks(v.astype(f32), chunk)
    ig = to_chunks(ig.astype(f32), chunk)
    lf = to_chunks(lf.astype(f32), chunk)
    causal = jnp.tril(jnp.ones((chunk, chunk), bool))
    F = jnp.cumsum(lf, axis=-1)
    dm = jnp.where(causal, F[..., :, None] - F[..., None, :] + ig[..., None, :], -jnp.inf)
    dmax = jnp.max(dm, axis=-1)
    qk = jnp.einsum('nbhtd,nbhsd->nbhts', q, k)
    src_end = F[..., -1:] - F + ig
    f_end = F[..., -1]

    def step(carry, xs):
        cs, ns, ms = carry
        f_c, dm_c, dmax_c, qk_c, q_c, k_c, v_c, se_c, fe_c = xs
        mt = jnp.maximum(f_c + ms[..., None], dmax_c)
        inter = jnp.exp(f_c + ms[..., None] - mt)
        w = jnp.exp(dm_c - mt[..., None]) * qk_c
        num = inter[..., None] * jnp.einsum('bhtk,bhkv->bhtv', q_c, cs) + jnp.einsum('bhts,bhsv->bhtv', w, v_c)
        den = inter * jnp.einsum('bhtk,bhk->bht', q_c, ns) + jnp.sum(w, axis=-1)
        h = num / jnp.maximum(jnp.abs(den), jnp.exp(-mt))[..., None]
        m_new = jnp.maximum(fe_c + ms, jnp.max(se_c, axis=-1))
        sc = jnp.exp(fe_c + ms - m_new)
        ws = jnp.exp(se_c - m_new[..., None])
        c_new = sc[..., None, None] * cs + jnp.einsum('bhs,bhsk,bhsv->bhkv', ws, k_c, v_c)
        n_new = sc[..., None] * ns + jnp.einsum('bhs,bhsk->bhk', ws, k_c)
        return (c_new, n_new, m_new), h

    (c_f, n_f, m_f), h = lax.scan(step, (c0.astype(f32), n0.astype(f32), m0.astype(f32)),
                                  (F, dm, dmax, qk, q, k, v, src_end, f_end))
    return from_chunks(h), c_f, n_f, m_f


def chunk_causal_attention(q, k, v):
    B, L, H, dq = q.shape
    scale = dq ** -0.5
    outs = []
    for i in range(L // Q_BLOCK):
        end = (i + 1) * Q_BLOCK
        s = jnp.einsum('bqhd,bkhd->bhqk', q[:, i * Q_BLOCK:end], k[:, :end]).astype(jnp.float32) * scale
        qc = (i * Q_BLOCK + jnp.arange(Q_BLOCK)) // CHUNK
        kc = jnp.arange(end) // CHUNK
        s = jnp.where(kc[None, :] <= qc[:, None], s, -jnp.inf)
        p = jax.nn.softmax(s, axis=-1).astype(v.dtype)
        outs.append(jnp.einsum('bhqk,bkhd->bqhd', p, v[:, :end]))
    return jnp.concatenate(outs, axis=1)


def open_attention(q, k, v):
    scale = q.shape[-1] ** -0.5
    s = jnp.einsum('bqhd,bkhd->bhqk', q, k).astype(jnp.float32) * scale
    p = jax.nn.softmax(s, axis=-1).astype(v.dtype)
    return jnp.einsum('bhqk,bkhd->bqhd', p, v)


def peer(h, w_q, sub_keys, u_tab, v_tab):
    B, L, D = h.shape
    T = B * L
    nblk = -(-T // PEER_BLOCK)
    xt = jnp.pad(h.reshape(T, D), ((0, nblk * PEER_BLOCK - T), (0, 0))).reshape(nblk, PEER_BLOCK, D)

    def one(xb):
        q = (xb @ w_q).reshape(PEER_BLOCK, PEER_HEADS, 2, PEER_DKEY)
        s = jnp.einsum('thpd,hpkd->thpk', q, sub_keys).astype(jnp.float32)
        s1, i1 = lax.top_k(s[:, :, 0], PEER_TOPK)
        s2, i2 = lax.top_k(s[:, :, 1], PEER_TOPK)
        cand = (s1[..., :, None] + s2[..., None, :]).reshape(PEER_BLOCK, PEER_HEADS, PEER_TOPK * PEER_TOPK)
        sc, ci = lax.top_k(cand, PEER_TOPK)
        e = (jnp.take_along_axis(i1, ci // PEER_TOPK, axis=-1) * N_KEYS
             + jnp.take_along_axis(i2, ci % PEER_TOPK, axis=-1))
        gate = jax.nn.softmax(sc, axis=-1)
        act = jax.nn.gelu(jnp.einsum('td,thkd->thk', xb, u_tab[e]).astype(jnp.float32), approximate=False)
        return jnp.einsum('thk,thkd->td', (gate * act).astype(xb.dtype), v_tab[e])

    y = lax.map(one, xt).reshape(nblk * PEER_BLOCK, D)[:T]
    return y.reshape(B, L, D)


def token_mixers(h, lp, st, pos, prompt):
    B, L, _ = h.shape
    f32 = jnp.float32
    (a_qkv, a_z, a_b, a_a, b_ql, b_ckv, b_kr, c_qkv, c_o, c_i, c_f) = split_cols(h @ lp['w_in'], IN_SIZES)
    chunk = CHUNK if prompt else L
    a_qkv, conv_new = causal_conv(a_qkv, st['gdn_conv'], lp['gdn_conv_w'])
    a_qkv = jax.nn.silu(a_qkv)
    aq, ak, av = split_cols(a_qkv, (GDN_HEADS * GDN_DK, GDN_HEADS * GDN_DK, GDN_WIDTH))
    aq = l2norm(aq.reshape(B, L, GDN_HEADS, GDN_DK))
    ak = l2norm(ak.reshape(B, L, GDN_HEADS, GDN_DK))
    av = av.reshape(B, L, GDN_HEADS, GDN_DV)
    beta = jax.nn.sigmoid(a_b.astype(f32))
    g_log = -jnp.exp(lp['gdn_a_log'].astype(f32)) * jax.nn.softplus(a_a.astype(f32) + lp['gdn_dt_bias'].astype(f32))
    o_a, gdn_new = gdn_chunked(aq, ak, av, g_log, beta, st['gdn'], chunk)
    o_a = rmsnorm(o_a.astype(h.dtype), lp['gdn_out_g']) * jax.nn.silu(a_z.reshape(B, L, GDN_HEADS, GDN_DV))
    o_a = o_a.reshape(B, L, GDN_WIDTH)
    qg, kg = lp['mla_q_gain'], lp['mla_k_gain']
    q = (rmsnorm(b_ql, lp['mla_q_a_g']) @ lp['mla_w_uq']).reshape(B, L, MLA_HEADS, MLA_QK)
    q = jnp.concatenate([rmsnorm(q[..., :MLA_NOPE], qg[:MLA_NOPE]),
                         rope(rmsnorm(q[..., MLA_NOPE:], qg[MLA_NOPE:]), pos)], axis=-1)
    ckv = rmsnorm(b_ckv, lp['mla_kv_a_g'])
    kr = rope(rmsnorm(b_kr, kg[MLA_NOPE:])[:, :, None, :], pos)[:, :, 0]
    if prompt:
        ckv_all, kr_all = ckv, kr
    else:
        ckv_all = jnp.concatenate([st['mla_latent'].astype(h.dtype), ckv], axis=1)
        kr_all = jnp.concatenate([st['mla_krope'].astype(h.dtype), kr], axis=1)
    Lk = ckv_all.shape[1]
    kv = (ckv_all @ lp['mla_w_ukv']).reshape(B, Lk, MLA_HEADS, MLA_NOPE + MLA_V)
    k = jnp.concatenate([rmsnorm(kv[..., :MLA_NOPE], kg[:MLA_NOPE]),
                         jnp.broadcast_to(kr_all[:, :, None, :], (B, Lk, MLA_HEADS, MLA_ROPE))], axis=-1)
    v = kv[..., MLA_NOPE:]
    o_b = chunk_causal_attention(q, k, v) if prompt else open_attention(q, k, v)
    o_b = o_b.reshape(B, L, MLA_WIDTH)
    cq, ck, cv = split_cols(c_qkv, (MLSTM_HEADS * MLSTM_DK, MLSTM_HEADS * MLSTM_DK, MLSTM_WIDTH))
    cq = cq.reshape(B, L, MLSTM_HEADS, MLSTM_DK)
    ck = ck.reshape(B, L, MLSTM_HEADS, MLSTM_DK)
    cv = cv.reshape(B, L, MLSTM_HEADS, MLSTM_DV)
    ig = c_i.astype(f32) + lp['mlstm_i_bias'].astype(f32)
    lf = jax.nn.log_sigmoid(c_f.astype(f32) + lp['mlstm_f_bias'].astype(f32))
    o_c, mc, mn, mm = mlstm_chunked(cq, ck, cv, ig, lf, st['mlstm_c'], st['mlstm_n'], st['mlstm_m'], chunk)
    o_c = jax.nn.sigmoid(c_o).reshape(B, L, MLSTM_HEADS, MLSTM_DV) * o_c.astype(h.dtype)
    o_c = rmsnorm(o_c, lp['mlstm_out_g'].reshape(MLSTM_HEADS, MLSTM_DV)).reshape(B, L, MLSTM_WIDTH)
    mix = jnp.concatenate([o_a, o_b, o_c], axis=-1)
    dt = h.dtype
    new_state = (ckv.astype(dt), kr.astype(dt), gdn_new.astype(dt), conv_new.astype(dt),
                 mc.astype(dt), mn.astype(dt), mm.astype(dt))
    return mix, new_state


def trunk_layer(x, c, lp, st, pos, prompt):
    mod = jax.nn.silu(c) @ lp['ada_w'] + lp['ada_b']
    sh1, sc1, g1, sh2, sc2, g2 = [m[:, None, :] for m in jnp.split(mod, 6, axis=-1)]
    h = rmsnorm(x, lp['norm_attn_g']) * (1 + sc1) + sh1
    mix, new_state = token_mixers(h, lp, st, pos, prompt)
    x = x + g1 * (mix @ lp['w_out'])
    h = rmsnorm(x, lp['norm_ffn_g']) * (1 + sc2) + sh2
    x = x + g2 * peer(h, lp['peer_w_q'], lp['peer_sub_keys'], lp['peer_u'], lp['peer_v'])
    return x, new_state


def setup_inputs(seed: int = 0) -> dict:
    key = jax.random.key(seed)
    ks = jax.random.split(key, 40)
    f32 = jnp.float32

    def nrm(k, shape, scale):
        return scale * jax.random.normal(k, shape, f32)

    def gain(k, shape):
        return 1.0 + 0.02 * jax.random.normal(k, shape, f32)

    dt = jnp.exp(jax.random.uniform(ks[18], (DEPTH, GDN_HEADS), f32, math.log(1e-3), math.log(1e-1)))
    return {
        'x_prompt': nrm(ks[0], (BATCH, SEQ, D_MODEL), 1.0),
        'x_sample': nrm(ks[1], (DEC_BATCH, DEC_SEQ, D_MODEL), 1.0),
        'c_prompt': nrm(ks[2], (BATCH, D_MODEL), 1.0),
        'c_sample': nrm(ks[3], (DEC_BATCH, D_MODEL), 1.0),
        'cache_mla_latent': nrm(ks[4], (DEPTH, DEC_BATCH, PAST_LEN, MLA_KV_LORA), 1.0),
        'cache_mla_krope': nrm(ks[5], (DEPTH, DEC_BATCH, PAST_LEN, MLA_ROPE), 1.0),
        'state_gdn': nrm(ks[6], (DEPTH, DEC_BATCH, GDN_HEADS, GDN_DK, GDN_DV), 0.1),
        'state_gdn_conv': nrm(ks[7], (DEPTH, DEC_BATCH, CONV_W - 1, GDN_QKV), 1.0),
        'state_mlstm_c': nrm(ks[8], (DEPTH, DEC_BATCH, MLSTM_HEADS, MLSTM_DK, MLSTM_DV), 0.5),
        'state_mlstm_n': nrm(ks[9], (DEPTH, DEC_BATCH, MLSTM_HEADS, MLSTM_DK), 0.5),
        'state_mlstm_m': nrm(ks[10], (DEPTH, DEC_BATCH, MLSTM_HEADS), 1.0),
        'ada_w': nrm(ks[11], (DEPTH, D_MODEL, 6 * D_MODEL), 0.5 * D_MODEL ** -0.5),
        'ada_b': nrm(ks[12], (DEPTH, 6 * D_MODEL), 0.02),
        'norm_attn_g': gain(ks[13], (DEPTH, D_MODEL)),
        'norm_ffn_g': gain(ks[14], (DEPTH, D_MODEL)),
        'w_in': nrm(ks[15], (DEPTH, D_MODEL, D_IN), D_MODEL ** -0.5),
        'gdn_conv_w': nrm(ks[16], (DEPTH, CONV_W, GDN_QKV), CONV_W ** -0.5),
        'gdn_a_log': jnp.log(jax.random.uniform(ks[17], (DEPTH, GDN_HEADS), f32, 1.0, 16.0)),
        'gdn_dt_bias': dt + jnp.log(-jnp.expm1(-dt)),
        'gdn_out_g': gain(ks[19], (DEPTH, GDN_DV)),
        'mla_q_a_g': gain(ks[20], (DEPTH, MLA_Q_LORA)),
        'mla_w_uq': nrm(ks[21], (DEPTH, MLA_Q_LORA, MLA_HEADS * MLA_QK), MLA_Q_LORA ** -0.5),
        'mla_kv_a_g': gain(ks[22], (DEPTH, MLA_KV_LORA)),
        'mla_w_ukv': nrm(ks[23], (DEPTH, MLA_KV_LORA, MLA_HEADS * (MLA_NOPE + MLA_V)), MLA_KV_LORA ** -0.5),
        'mla_q_gain': gain(ks[24], (DEPTH, MLA_QK)),
        'mla_k_gain': gain(ks[25], (DEPTH, MLA_QK)),
        'mlstm_i_bias': nrm(ks[26], (DEPTH, MLSTM_HEADS), 0.1),
        'mlstm_f_bias': 3.0 + nrm(ks[27], (DEPTH, MLSTM_HEADS), 0.5),
        'mlstm_out_g': gain(ks[28], (DEPTH, MLSTM_WIDTH)),
        'w_out': nrm(ks[29], (DEPTH, D_MIX, D_MODEL), D_MIX ** -0.5),
        'peer_w_q': nrm(ks[30], (DEPTH, D_MODEL, PEER_HEADS * 2 * PEER_DKEY), D_MODEL ** -0.5),
        'peer_sub_keys': nrm(ks[31], (DEPTH, PEER_HEADS, 2, N_KEYS, PEER_DKEY), PEER_DKEY ** -0.5),
        'peer_u': nrm(ks[32], (DEPTH, N_EXPERTS, D_MODEL), D_MODEL ** -0.5),
        'peer_v': nrm(ks[33], (DEPTH, N_EXPERTS, D_MODEL), 0.5),
    }


def reference(x_prompt, x_sample, c_prompt, c_sample, cache_mla_latent, cache_mla_krope,
              state_gdn, state_gdn_conv, state_mlstm_c, state_mlstm_n, state_mlstm_m,
              ada_w, ada_b, norm_attn_g, norm_ffn_g, w_in, gdn_conv_w, gdn_a_log, gdn_dt_bias,
              gdn_out_g, mla_q_a_g, mla_w_uq, mla_kv_a_g, mla_w_ukv, mla_q_gain, mla_k_gain,
              mlstm_i_bias, mlstm_f_bias, mlstm_out_g, w_out, peer_w_q, peer_sub_keys, peer_u, peer_v):
    f32 = jnp.float32
    B, Lp, _ = x_prompt.shape
    Ls = x_sample.shape[1]
    past = cache_mla_latent.shape[2]
    pos_p = jnp.arange(Lp, dtype=jnp.int32)
    pos_s = past + jnp.arange(Ls, dtype=jnp.int32)
    xp, xs = x_prompt, x_sample
    new_p, new_s = [], []
    for l in range(DEPTH):
        lp = {
            'ada_w': ada_w[l], 'ada_b': ada_b[l], 'norm_attn_g': norm_attn_g[l], 'norm_ffn_g': norm_ffn_g[l],
            'w_in': w_in[l], 'gdn_conv_w': gdn_conv_w[l], 'gdn_a_log': gdn_a_log[l],
            'gdn_dt_bias': gdn_dt_bias[l], 'gdn_out_g': gdn_out_g[l],
            'mla_q_a_g': mla_q_a_g[l], 'mla_w_uq': mla_w_uq[l], 'mla_kv_a_g': mla_kv_a_g[l],
            'mla_w_ukv': mla_w_ukv[l], 'mla_q_gain': mla_q_gain[l], 'mla_k_gain': mla_k_gain[l],
            'mlstm_i_bias': mlstm_i_bias[l], 'mlstm_f_bias': mlstm_f_bias[l], 'mlstm_out_g': mlstm_out_g[l],
            'w_out': w_out[l], 'peer_w_q': peer_w_q[l], 'peer_sub_keys': peer_sub_keys[l],
            'peer_u': peer_u[l], 'peer_v': peer_v[l],
        }
        st_p = {
            'mla_latent': None, 'mla_krope': None,
            'gdn': jnp.zeros((B, GDN_HEADS, GDN_DK, GDN_DV), f32),
            'gdn_conv': jnp.zeros((B, CONV_W - 1, GDN_QKV), xp.dtype),
            'mlstm_c': jnp.zeros((B, MLSTM_HEADS, MLSTM_DK, MLSTM_DV), f32),
            'mlstm_n': jnp.zeros((B, MLSTM_HEADS, MLSTM_DK), f32),
            'mlstm_m': jnp.zeros((B, MLSTM_HEADS), f32),
        }
        st_s = {
            'mla_latent': cache_mla_latent[l], 'mla_krope': cache_mla_krope[l],
            'gdn': state_gdn[l], 'gdn_conv': state_gdn_conv[l],
            'mlstm_c': state_mlstm_c[l], 'mlstm_n': state_mlstm_n[l], 'mlstm_m': state_mlstm_m[l],
        }
        xp, sp = trunk_layer(xp, c_prompt, lp, st_p, pos_p, True)
        xs, ss = trunk_layer(xs, c_sample, lp, st_s, pos_s, False)
        new_p.append(sp)
        new_s.append(ss)
    lat_p, kr_p, gdn_p, conv_p, mc_p, mn_p, mm_p = [jnp.stack([s[i] for s in new_p]) for i in range(7)]
    lat_s, kr_s, gdn_s, conv_s, mc_s, mn_s, mm_s = [jnp.stack([s[i] for s in new_s]) for i in range(7)]
    return (xp, xs, lat_p, kr_p, gdn_p, conv_p, mc_p, mn_p, mm_p,
            lat_s, kr_s, gdn_s, conv_s, mc_s, mn_s, mm_s)
```

```python
import functools
import math

import jax
import jax.numpy as jnp
from jax import lax
from jax.experimental import pallas as pl
from jax.experimental.pallas import tpu as pltpu

F32 = jnp.float32
BF16 = jnp.bfloat16

D_MODEL = 1024
DEPTH = 2
CHUNK = 64
EPS = 1e-6
GDN_HEADS = 6
GDN_DK = 64
GDN_WIDTH = 384
GDN_QKV = 1152
CONV_W = 4
MLA_HEADS = 6
MLA_NOPE = 64
MLA_ROPE = 32
MLA_QK = 96
MLA_Q_LORA = 384
MLA_KV_LORA = 256
ROPE_THETA = 10000.0
MLSTM_HEADS = 4
MLSTM_WIDTH = 256
MLSTM_QKV = 768
PEER_HEADS = 8
N_KEYS = 128
N_EXPERTS = N_KEYS * N_KEYS
PEER_TOPK = 16
PEER_SLOTS = PEER_HEADS * PEER_TOPK

HEAD_W = 64
LANES = 128
NEG = -1e30
VMEM_LIMIT = 56 * 1024 * 1024


def _bf(x):
    return x.astype(BF16)


def _mm(a, b):
    return jnp.dot(_bf(a), _bf(b), preferred_element_type=F32)


def _mm_nt(a, b):
    return lax.dot_general(_bf(a), _bf(b), (((1,), (1,)), ((), ())), preferred_element_type=F32)


def _split3(x):
    hi = _bf(x)
    r = x - hi.astype(F32)
    mid = _bf(r)
    lo = _bf(r - mid.astype(F32))
    return hi, mid, lo


def _mm_sel_exact(sel_bf, x):
    hi, mid, lo = _split3(x)
    d = lambda t: jnp.dot(sel_bf, t, preferred_element_type=F32)
    return d(hi) + d(mid) + d(lo)


def _sigmoid(x):
    return 1.0 / (1.0 + jnp.exp(-x))


def _silu(x):
    return x * _sigmoid(x)


def _softplus(x):
    return jnp.maximum(x, 0.0) + jnp.log1p(jnp.exp(-jnp.abs(x)))


def _iota(shape, axis):
    return lax.broadcasted_iota(jnp.int32, shape, axis)


def _pair_consts(C):
    row = _iota((C, LANES), 0)
    lane = _iota((C, LANES), 1)
    s = lane & (HEAD_W - 1)
    lo_half = lane < HEAD_W
    r2 = _iota((LANES, LANES), 0)
    l2 = _iota((LANES, LANES), 1)
    same_half = (r2 >> 6) == (l2 >> 6)
    return row, lane, s, lo_half, same_half


def _bd(y, lo_half):
    z = jnp.zeros_like(y)
    return jnp.concatenate([jnp.where(lo_half, y, z), jnp.where(lo_half, z, y)], axis=0)


def _diag_row(x, row, s):
    return jnp.sum(jnp.where(row == s, x, 0.0), axis=0, keepdims=True)


def _cparams(sem, vmem=None):
    kw = dict(dimension_semantics=sem)
    if vmem is not None:
        kw["vmem_limit_bytes"] = vmem
    return pltpu.CompilerParams(**kw)


def _mod_kernel(c_ref, w_ref, b_ref, o_ref):
    c = c_ref[...]
    o_ref[0] = _mm(_silu(c), w_ref[0]) + b_ref[0]


def _mod_call(c_all, ada_w, ada_b):
    nb = c_all.shape[0]
    nj = 6
    return pl.pallas_call(
        _mod_kernel,
        out_shape=jax.ShapeDtypeStruct((DEPTH, nb, 6 * D_MODEL), F32),
        grid=(DEPTH, nj),
        in_specs=[
            pl.BlockSpec((nb, D_MODEL), lambda l, j: (0, 0)),
            pl.BlockSpec((1, D_MODEL, D_MODEL), lambda l, j: (l, 0, j)),
            pl.BlockSpec((1, 1, D_MODEL), lambda l, j: (l, 0, j)),
        ],
        out_specs=pl.BlockSpec((1, nb, D_MODEL), lambda l, j: (l, 0, j)),
        compiler_params=_cparams(("arbitrary", "arbitrary")),
        name="adaln_mod",
    )(c_all, ada_w, ada_b.reshape(DEPTH, 1, 6 * D_MODEL))


def _inproj_kernel(x_ref, sh_ref, sc_ref, g_ref, wg_ref, wm_ref, wl_ref, og_ref, om_ref, ol_ref):
    x = x_ref[0]
    y = x * lax.rsqrt(jnp.mean(x * x, axis=-1, keepdims=True) + EPS) * g_ref[...]
    h = _bf(y * (1.0 + sc_ref[0]) + sh_ref[0])
    og_ref[0] = jnp.dot(h, wg_ref[...], preferred_element_type=F32)
    om_ref[0] = jnp.dot(h, wm_ref[...], preferred_element_type=F32)
    ol_ref[0] = jnp.dot(h, wl_ref[...], preferred_element_type=F32)


def _inproj_call(x, sh, sc, g, wg, wm, wl):
    B, L, _ = x.shape
    tb = min(L, 512)
    ng, nm, nl = wg.shape[1], wm.shape[1], wl.shape[1]
    full = lambda shape: pl.BlockSpec(shape, lambda b, i: (0,) * len(shape))
    per_b = pl.BlockSpec((1, 1, D_MODEL), lambda b, i: (b, 0, 0))
    return pl.pallas_call(
        _inproj_kernel,
        out_shape=(jax.ShapeDtypeStruct((B, L, ng), F32),
                   jax.ShapeDtypeStruct((B, L, nm), F32),
                   jax.ShapeDtypeStruct((B, L, nl), F32)),
        grid=(B, L // tb),
        in_specs=[pl.BlockSpec((1, tb, D_MODEL), lambda b, i: (b, i, 0)), per_b, per_b,
                  full((1, D_MODEL)), full(wg.shape), full(wm.shape), full(wl.shape)],
        out_specs=(pl.BlockSpec((1, tb, ng), lambda b, i: (b, i, 0)),
                   pl.BlockSpec((1, tb, nm), lambda b, i: (b, i, 0)),
                   pl.BlockSpec((1, tb, nl), lambda b, i: (b, i, 0))),
        compiler_params=_cparams(("arbitrary", "arbitrary"), VMEM_LIMIT),
        name="in_proj",
    )(x, sh, sc, g, wg, wm, wl)


_GDN_Z0 = GDN_QKV
_GDN_B0 = GDN_QKV + GDN_WIDTH
_GDN_A0 = _GDN_B0 + GDN_WIDTH
_GDN_COLS = _GDN_A0 + GDN_WIDTH
_CONV_PAD = 8


def _gdn_kernel(pg_ref, conv0_ref, s0_ref, convw_ref, alog_ref, dtb_ref, og_ref,
                o_ref, sfin_ref, convfin_ref, xp_ref, s_ref, *, C, n_valid, n_chunks):
    n = pl.program_id(1)
    hist = CONV_W - 1

    @pl.when(n == 0)
    def _():
        xp_ref[0:_CONV_PAD, :] = jnp.zeros((_CONV_PAD, GDN_QKV), F32)
        xp_ref[_CONV_PAD - hist:_CONV_PAD, :] = conv0_ref[0]
        s_ref[...] = s0_ref[0]

    a_pre = pg_ref[0, :, 0:GDN_QKV]
    xp_ref[_CONV_PAD:_CONV_PAD + C, :] = a_pre
    w = convw_ref[...]
    y = w[CONV_W - 1:CONV_W] * a_pre
    for j in range(CONV_W - 1):
        y = y + w[j:j + 1] * xp_ref[_CONV_PAD - hist + j:_CONV_PAD - hist + j + C, :]
    y = _silu(y)

    row, lane, s, lo_half, same_half = _pair_consts(C)
    ones_blk = jnp.where(same_half, 1.0, 0.0).astype(BF16)
    tri = jnp.where(_iota((C, C), 1) <= _iota((C, C), 0), 1.0, 0.0).astype(BF16)
    causal = s <= row
    strict = s < row
    eye = s == row

    def hsum(x):
        return jnp.dot(_bf(x), ones_blk, preferred_element_type=F32)

    beta_all = _sigmoid(pg_ref[0, :, _GDN_B0:_GDN_A0])
    gl_all = -jnp.exp(alog_ref[...]) * _softplus(pg_ref[0, :, _GDN_A0:_GDN_COLS] + dtb_ref[...])
    if n_valid < C * n_chunks:
        valid = (_iota((C, GDN_WIDTH), 0) + n * C) < n_valid
        beta_all = jnp.where(valid, beta_all, 0.0)
        gl_all = jnp.where(valid, gl_all, 0.0)
    g_all = _mm_sel_exact(tri, gl_all)

    for j in range(GDN_HEADS // 2):
        sl = slice(LANES * j, LANES * (j + 1))
        q2 = y[:, LANES * j:LANES * (j + 1)]
        k2 = y[:, GDN_WIDTH + LANES * j:GDN_WIDTH + LANES * (j + 1)]
        v2 = y[:, 2 * GDN_WIDTH + LANES * j:2 * GDN_WIDTH + LANES * (j + 1)]
        q2 = q2 * lax.rsqrt(hsum(q2 * q2) + EPS) * (GDN_DK ** -0.5)
        k2 = k2 * lax.rsqrt(hsum(k2 * k2) + EPS)
        beta = beta_all[:, sl]
        G = g_all[:, sl]
        eG = jnp.exp(G)
        g_row = _diag_row(G, row, s)
        diff = G - g_row
        decay = jnp.where(causal, jnp.exp(jnp.where(causal, diff, 0.0)), 0.0)
        kbd = _bd(k2, lo_half)
        A = jnp.where(strict, beta * _mm_nt(k2, kbd) * decay, 0.0)
        T = jnp.where(eye, 1.0, 0.0) - jnp.where(((row & 1) == 1) & (s == row - 1), A, 0.0)
        for lg in range(1, 6):
            sub = (((row >> lg) & 1) == 1) & ((s >> lg) == (row >> lg) - 1)
            t1 = _mm(T, _bd(jnp.where(sub, A, 0.0), lo_half))
            T = T - _mm(t1, _bd(T, lo_half))
        rhs = jnp.concatenate([_bd(beta * v2, lo_half), _bd(beta * eG * k2, lo_half)], axis=1)
        uw = _mm(T, rhs)
        u = uw[:, 0:LANES]
        wk = uw[:, LANES:2 * LANES]
        qk = _mm_nt(q2, kbd) * decay
        g_last = G[C - 1:C, :]
        kend = k2 * jnp.exp(g_last - G)
        S = s_ref[j]
        ws = _mm(jnp.concatenate([wk, q2 * eG], axis=0), S)
        wn = u - ws[0:C]
        o = ws[C:2 * C] + _mm(qk, _bd(wn, lo_half))
        s_ref[j] = jnp.exp(g_last) * S + jnp.where(same_half, _mm(kend.T, wn), 0.0)
        o = o * lax.rsqrt(hsum(o * o) * (1.0 / HEAD_W) + EPS) * og_ref[...]
        z = pg_ref[0, :, _GDN_Z0 + LANES * j:_GDN_Z0 + LANES * (j + 1)]
        o_ref[0, :, sl] = o * _silu(z)

    @pl.when(n < n_chunks - 1)
    def _():
        xp_ref[_CONV_PAD - hist:_CONV_PAD, :] = xp_ref[_CONV_PAD + C - hist:_CONV_PAD + C, :]

    @pl.when(n == n_chunks - 1)
    def _():
        lv = n_valid - (n_chunks - 1) * C
        convfin_ref[0] = xp_ref[_CONV_PAD + lv - hist:_CONV_PAD + lv, :]
        sfin_ref[0] = s_ref[...]


def _gdn_call(pg, conv0, s0_bd, conv_w, alog_rep, dtb_rep, og_rep, n_valid):
    B, Lp, _ = pg.shape
    C = CHUNK
    n_chunks = Lp // C
    npair = GDN_HEADS // 2
    kern = functools.partial(_gdn_kernel, C=C, n_valid=n_valid, n_chunks=n_chunks)
    full = lambda shape: pl.BlockSpec(shape, lambda b, n: (0,) * len(shape))
    return pl.pallas_call(
        kern,
        out_shape=(jax.ShapeDtypeStruct((B, Lp, GDN_WIDTH), F32),
                   jax.ShapeDtypeStruct((B, npair, LANES, LANES), F32),
                   jax.ShapeDtypeStruct((B, CONV_W - 1, GDN_QKV), F32)),
        grid=(B, n_chunks),
        in_specs=[pl.BlockSpec((1, C, _GDN_COLS), lambda b, n: (b, n, 0)),
                  pl.BlockSpec((1, CONV_W - 1, GDN_QKV), lambda b, n: (b, 0, 0)),
                  pl.BlockSpec((1, npair, LANES, LANES), lambda b, n: (b, 0, 0, 0)),
                  full((CONV_W, GDN_QKV)), full((1, GDN_WIDTH)), full((1, GDN_WIDTH)), full((1, LANES))],
        out_specs=(pl.BlockSpec((1, C, GDN_WIDTH), lambda b, n: (b, n, 0)),
                   pl.BlockSpec((1, npair, LANES, LANES), lambda b, n: (b, 0, 0, 0)),
                   pl.BlockSpec((1, CONV_W - 1, GDN_QKV), lambda b, n: (b, 0, 0))),
        scratch_shapes=[pltpu.VMEM((_CONV_PAD + C, GDN_QKV), F32),
                        pltpu.VMEM((npair, LANES, LANES), F32)],
        compiler_params=_cparams(("arbitrary", "arbitrary")),
        name="gdn_chunks",
    )(pg, conv0, s0_bd, conv_w, alog_rep, dtb_rep, og_rep)


_ML_O0 = MLSTM_QKV
_ML_I0 = _ML_O0 + MLSTM_WIDTH
_ML_F0 = _ML_I0 + MLSTM_WIDTH
_ML_COLS = _ML_F0 + MLSTM_WIDTH


def _mlstm_kernel(pm_ref, c0_ref, n0_ref, m0_ref, ib_ref, fb_ref, og_ref,
                  o_ref, cfin_ref, nfin_ref, mfin_ref, c_ref, n_ref, m_ref, *, C, n_valid, n_chunks):
    n = pl.program_id(1)

    @pl.when(n == 0)
    def _():
        c_ref[...] = c0_ref[0]
        n_ref[...] = n0_ref[0]
        m_ref[...] = m0_ref[0]

    row, lane, s, lo_half, same_half = _pair_consts(C)
    ones_blk = jnp.where(same_half, 1.0, 0.0).astype(BF16)
    tri = jnp.where(_iota((C, C), 1) <= _iota((C, C), 0), 1.0, 0.0).astype(BF16)
    causal = s <= row

    def hsum(x):
        return jnp.dot(_bf(x), ones_blk, preferred_element_type=F32)

    ig_all = pm_ref[0, :, _ML_I0:_ML_F0] + ib_ref[...]
    lf_all = -_softplus(-(pm_ref[0, :, _ML_F0:_ML_COLS] + fb_ref[...]))
    if n_valid < C * n_chunks:
        valid = (_iota((C, MLSTM_WIDTH), 0) + n * C) < n_valid
        ig_all = jnp.where(valid, ig_all, NEG)
        lf_all = jnp.where(valid, lf_all, 0.0)
    f_all = _mm_sel_exact(tri, lf_all)

    for j in range(MLSTM_HEADS // 2):
        sl = slice(LANES * j, LANES * (j + 1))
        q2 = pm_ref[0, :, LANES * j:LANES * (j + 1)]
        k2 = pm_ref[0, :, MLSTM_WIDTH + LANES * j:MLSTM_WIDTH + LANES * (j + 1)] * (HEAD_W ** -0.5)
        v2 = pm_ref[0, :, 2 * MLSTM_WIDTH + LANES * j:2 * MLSTM_WIDTH + LANES * (j + 1)]
        ig = ig_all[:, sl]
        F = f_all[:, sl]
        f_row = _diag_row(F, row, s)
        i_row = _diag_row(ig, row, s)
        dm = jnp.where(causal, F - f_row + i_row, NEG)
        dmax_e = jnp.max(jnp.where(lo_half, dm, NEG), axis=1, keepdims=True)
        dmax_o = jnp.max(jnp.where(lo_half, NEG, dm), axis=1, keepdims=True)
        dmax = jnp.where(lo_half, dmax_e, dmax_o)
        ms = m_ref[j]
        cs = c_ref[j]
        ns = n_ref[j]
        mt = jnp.maximum(F + ms, dmax)
        inter = jnp.exp(F + ms - mt)
        wgt = jnp.exp(dm - mt) * _mm_nt(q2, _bd(k2, lo_half))
        num = inter * _mm(q2, cs) + _mm(wgt, _bd(v2, lo_half))
        den = inter * hsum(q2 * ns) + hsum(wgt)
        h = num / jnp.maximum(jnp.abs(den), jnp.exp(-mt))
        fe = F[C - 1:C, :]
        se = fe - F + ig
        m_new = jnp.maximum(fe + ms, jnp.max(se, axis=0, keepdims=True))
        sc = jnp.exp(fe + ms - m_new)
        kw = k2 * jnp.exp(se - m_new)
        c_ref[j] = sc * cs + jnp.where(same_half, _mm(kw.T, v2), 0.0)
        n_ref[j] = sc * ns + jnp.sum(kw, axis=0, keepdims=True)
        m_ref[j] = m_new
        o = _sigmoid(pm_ref[0, :, _ML_O0 + LANES * j:_ML_O0 + LANES * (j + 1)]) * h
        o_ref[0, :, sl] = o * lax.rsqrt(hsum(o * o) * (1.0 / HEAD_W) + EPS) * og_ref[:, sl]

    @pl.when(n == n_chunks - 1)
    def _():
        cfin_ref[0] = c_ref[...]
        nfin_ref[0] = n_ref[...]
        mfin_ref[0] = m_ref[...]


def _mlstm_call(pm, c0_bd, n0, m0_rep, ib_rep, fb_rep, og, n_valid):
    B, Lp, _ = pm.shape
    C = CHUNK
    n_chunks = Lp // C
    npair = MLSTM_HEADS // 2
    kern = functools.partial(_mlstm_kernel, C=C, n_valid=n_valid, n_chunks=n_chunks)
    full = lambda shape: pl.BlockSpec(shape, lambda b, n: (0,) * len(shape))
    st_c = pl.BlockSpec((1, npair, LANES, LANES), lambda b, n: (b, 0, 0, 0))
    st_v = pl.BlockSpec((1, npair, 1, LANES), lambda b, n: (b, 0, 0, 0))
    return pl.pallas_call(
        kern,
        out_shape=(jax.ShapeDtypeStruct((B, Lp, MLSTM_WIDTH), F32),
                   jax.ShapeDtypeStruct((B, npair, LANES, LANES), F32),
                   jax.ShapeDtypeStruct((B, npair, 1, LANES), F32),
                   jax.ShapeDtypeStruct((B, npair, 1, LANES), F32)),
        grid=(B, n_chunks),
        in_specs=[pl.BlockSpec((1, C, _ML_COLS), lambda b, n: (b, n, 0)), st_c, st_v, st_v,
                  full((1, MLSTM_WIDTH)), full((1, MLSTM_WIDTH)), full((1, MLSTM_WIDTH))],
        out_specs=(pl.BlockSpec((1, C, MLSTM_WIDTH), lambda b, n: (b, n, 0)), st_c, st_v, st_v),
        scratch_shapes=[pltpu.VMEM((npair, LANES, LANES), F32),
                        pltpu.VMEM((npair, 1, LANES), F32),
                        pltpu.VMEM((npair, 1, LANES), F32)],
        compiler_params=_cparams(("arbitrary", "arbitrary")),
        name="mlstm_chunks",
    )(pm, c0_bd, n0, m0_rep, ib_rep, fb_rep, og)


_MLA_C0 = MLA_Q_LORA
_MLA_R0 = MLA_Q_LORA + MLA_KV_LORA
_MLA_COLS = _MLA_R0 + LANES
_MLA_HW = LANES


def _swap_halves(x, lane, base, half):
    up = pltpu.roll(x, LANES - half, 1)
    dn = pltpu.roll(x, half, 1)
    first = (lane >= base) & (lane < base + half)
    second = (lane >= base + half) & (lane < base + 2 * half)
    return jnp.where(first, up, jnp.where(second, dn, 0.0))


def _mla_q_kernel(pm_ref, qag_ref, wuq_ref, qgain_ref, kvag_ref, krgain_ref,
                  cosq_ref, sinq_ref, cosk_ref, sink_ref, q_ref, lat_ref, kr_ref, *, tb):
    lane = _iota((tb, LANES), 1)
    ql = pm_ref[0, :, 0:_MLA_C0]
    ql = ql * lax.rsqrt(jnp.mean(ql * ql, axis=-1, keepdims=True) + EPS) * qag_ref[...]
    q = jnp.dot(_bf(ql), wuq_ref[...], preferred_element_type=F32)
    nope = lane < MLA_NOPE
    rope = (lane >= MLA_NOPE) & (lane < MLA_QK)
    cosq = cosq_ref[...]
    sinq = sinq_ref[...]
    for h in range(MLA_HEADS):
        qh = q[:, _MLA_HW * h:_MLA_HW * (h + 1)]
        sq = qh * qh
        ss_n = jnp.sum(jnp.where(nope, sq, 0.0), axis=-1, keepdims=True) * (1.0 / MLA_NOPE)
        ss_r = jnp.sum(jnp.where(rope, sq, 0.0), axis=-1, keepdims=True) * (1.0 / MLA_ROPE)
        rinv = jnp.where(nope, lax.rsqrt(ss_n + EPS), lax.rsqrt(ss_r + EPS))
        qn = qh * rinv * qgain_ref[...]
        sw = _swap_halves(qn, lane, MLA_NOPE, MLA_ROPE // 2)
        qr = jnp.where(nope, qn, qn * cosq + sw * sinq)
        q_ref[0, :, _MLA_HW * h:_MLA_HW * (h + 1)] = _bf(qr * (MLA_QK ** -0.5))
    ckv = pm_ref[0, :, _MLA_C0:_MLA_R0]
    lat_ref[0] = ckv * lax.rsqrt(jnp.mean(ckv * ckv, axis=-1, keepdims=True) + EPS) * kvag_ref[...]
    kx = pm_ref[0, :, _MLA_R0:_MLA_COLS]
    ssk = jnp.sum(kx * kx, axis=-1, keepdims=True) * (1.0 / MLA_ROPE)
    kn = kx * lax.rsqrt(ssk + EPS) * krgain_ref[...]
    swk = _swap_halves(kn, lane, 0, MLA_ROPE // 2)
    kr = kn * cosk_ref[...] + swk * sink_ref[...]
    kr_ref[0] = kr[:, 0:MLA_ROPE]


def _mla_q_call(pm, qag, wuq, qgain, kvag, krgain, cosq, sinq, cosk, sink):
    B, L, _ = pm.shape
    tb = min(L, 512)
    full = lambda shape: pl.BlockSpec(shape, lambda b, i: (0,) * len(shape))
    tab = pl.BlockSpec((tb, LANES), lambda b, i: (i, 0))
    nq = MLA_HEADS * _MLA_HW
    return pl.pallas_call(
        functools.partial(_mla_q_kernel, tb=tb),
        out_shape=(jax.ShapeDtypeStruct((B, L, nq), BF16),
                   jax.ShapeDtypeStruct((B, L, MLA_KV_LORA), F32),
                   jax.ShapeDtypeStruct((B, L, MLA_ROPE), F32)),
        grid=(B, L // tb),
        in_specs=[pl.BlockSpec((1, tb, _MLA_COLS), lambda b, i: (b, i, 0)),
                  full((1, MLA_Q_LORA)), full((MLA_Q_LORA, nq)), full((1, LANES)),
                  full((1, MLA_KV_LORA)), full((1, LANES)), tab, tab, tab, tab],
        out_specs=(pl.BlockSpec((1, tb, nq), lambda b, i: (b, i, 0)),
                   pl.BlockSpec((1, tb, MLA_KV_LORA), lambda b, i: (b, i, 0)),
                   pl.BlockSpec((1, tb, MLA_ROPE), lambda b, i: (b, i, 0))),
        compiler_params=_cparams(("arbitrary", "arbitrary")),
        name="mla_q_proj",
    )(pm, qag, wuq, qgain, kvag, krgain, cosq, sinq, cosk, sink)


def _mla_kv_kernel(ckv_ref, krp_ref, wukv_ref, kgain_ref, k_ref, kv_ref, *, tb):
    lane = _iota((tb, LANES), 1)
    nope = lane < MLA_NOPE
    kv = jnp.dot(_bf(ckv_ref[0]), wukv_ref[...], preferred_element_type=F32)
    krp = krp_ref[0]
    for h in range(MLA_HEADS):
        g = kv[:, _MLA_HW * h:_MLA_HW * (h + 1)]
        ss = jnp.sum(jnp.where(nope, g * g, 0.0), axis=-1, keepdims=True) * (1.0 / MLA_NOPE)
        k_ref[0, :, _MLA_HW * h:_MLA_HW * (h + 1)] = _bf(g * lax.rsqrt(ss + EPS) * kgain_ref[...] + krp)
    kv_ref[0] = _bf(kv)


def _mla_kv_call(ckv_all, krp_all, wukv, kgain):
    B, Lk, _ = ckv_all.shape
    tb = 512 if Lk % 512 == 0 else Lk
    nk = MLA_HEADS * _MLA_HW
    full = lambda shape: pl.BlockSpec(shape, lambda b, i: (0,) * len(shape))
    return pl.pallas_call(
        functools.partial(_mla_kv_kernel, tb=tb),
        out_shape=(jax.ShapeDtypeStruct((B, Lk, nk), BF16), jax.ShapeDtypeStruct((B, Lk, nk), BF16)),
        grid=(B, Lk // tb),
        in_specs=[pl.BlockSpec((1, tb, MLA_KV_LORA), lambda b, i: (b, i, 0)),
                  pl.BlockSpec((1, tb, LANES), lambda b, i: (b, i, 0)),
                  full((MLA_KV_LORA, nk)), full((1, LANES))],
        out_specs=(pl.BlockSpec((1, tb, nk), lambda b, i: (b, i, 0)),
                   pl.BlockSpec((1, tb, nk), lambda b, i: (b, i, 0))),
        compiler_params=_cparams(("arbitrary", "arbitrary"), VMEM_LIMIT),
        name="mla_kv_proj",
    )(ckv_all, krp_all, wukv, kgain)


def _attn_kernel(q_ref, k_ref, kv_ref, o_ref, m_ref, l_ref, acc_ref, *, tq, tk, nk, causal):
    qi = pl.program_id(2)
    ki = pl.program_id(3)
    last = qi if causal else nk - 1

    @pl.when(ki == 0)
    def _():
        m_ref[...] = jnp.full(m_ref.shape, NEG, F32)
        l_ref[...] = jnp.zeros(l_ref.shape, F32)
        acc_ref[...] = jnp.zeros(acc_ref.shape, F32)

    @pl.when(ki <= last)
    def _():
        if causal:
            qc = (qi * tq + _iota((tq, tk), 0)) >> 6
            kc = (ki * tk + _iota((tq, tk), 1)) >> 6
            keep = kc <= qc
        for hh in range(2):
            hs = slice(_MLA_HW * hh, _MLA_HW * (hh + 1))
            sc = lax.dot_general(q_ref[0, :, hs], k_ref[0, :, hs], (((1,), (1,)), ((), ())),
                                 preferred_element_type=F32)
            if causal:
                sc = jnp.where(keep, sc, NEG)
            m_prev = m_ref[hh]
            m_new = jnp.maximum(m_prev, jnp.max(sc, axis=-1, keepdims=True))
            p = jnp.exp(sc - m_new)
            alpha = jnp.exp(m_prev - m_new)
            l_ref[hh] = alpha * l_ref[hh] + jnp.sum(p, axis=-1, keepdims=True)
            acc_ref[hh] = alpha * acc_ref[hh] + jnp.dot(_bf(p), kv_ref[0, :, hs],
                                                        preferred_element_type=F32)
            m_ref[hh] = m_new

    @pl.when(ki == last)
    def _():
        lane = _iota((tq, LANES), 1)
        o_e = pltpu.roll(acc_ref[0] / l_ref[0], MLA_NOPE, 1)
        o_o = acc_ref[1] / l_ref[1]
        o_ref[0] = jnp.where(lane < MLA_NOPE, o_e, o_o)


def _attn_call(q, k, kv, causal):
    B, Lq, _ = q.shape
    Lk = k.shape[1]
    if causal:
        tq = tk = min(Lq, 512)
    else:
        tq, tk = Lq, Lk
    nq, nk = Lq // tq, Lk // tk
    w2 = 2 * _MLA_HW
    kmap = (lambda b, hp, qi, ki: (b, jnp.minimum(ki, qi), hp)) if causal else (lambda b, hp, qi, ki: (b, ki, hp))
    return pl.pallas_call(
        functools.partial(_attn_kernel, tq=tq, tk=tk, nk=nk, causal=causal),
        out_shape=jax.ShapeDtypeStruct((B, Lq, MLA_HEADS * MLA_NOPE), F32),
        grid=(B, MLA_HEADS // 2, nq, nk),
        in_specs=[pl.BlockSpec((1, tq, w2), lambda b, hp, qi, ki: (b, qi, hp)),
                  pl.BlockSpec((1, tk, w2), kmap),
                  pl.BlockSpec((1, tk, w2), kmap)],
        out_specs=pl.BlockSpec((1, tq, LANES), lambda b, hp, qi, ki: (b, qi, hp)),
        scratch_shapes=[pltpu.VMEM((2, tq, 1), F32), pltpu.VMEM((2, tq, 1), F32),
                        pltpu.VMEM((2, tq, LANES), F32)],
        compiler_params=_cparams(("arbitrary",) * 4, VMEM_LIMIT),
        name="mla_attention",
    )(q, k, kv)


def _outproj_kernel(a_ref, b_ref, c_ref, x_ref, g1_ref, sh_ref, sc_ref, ng_ref, wa_ref, wb_ref, wc_ref,
                    x1_ref, h2_ref):
    y = (jnp.dot(_bf(a_ref[0]), wa_ref[...], preferred_element_type=F32)
         + jnp.dot(_bf(b_ref[0]), wb_ref[...], preferred_element_type=F32)
         + jnp.dot(_bf(c_ref[0]), wc_ref[...], preferred_element_type=F32))
    x1 = x_ref[0] + g1_ref[0] * y
    x1_ref[0] = x1
    n = x1 * lax.rsqrt(jnp.mean(x1 * x1, axis=-1, keepdims=True) + EPS) * ng_ref[...]
    h2_ref[0] = n * (1.0 + sc_ref[0]) + sh_ref[0]


def _outproj_call(oa, ob, oc, x, g1, sh2, sc2, ng, wa, wb, wc):
    B, L, _ = x.shape
    tb = min(L, 512)
    full = lambda shape: pl.BlockSpec(shape, lambda b, i: (0,) * len(shape))
    per_b = pl.BlockSpec((1, 1, D_MODEL), lambda b, i: (b, 0, 0))
    blk = lambda wdt: pl.BlockSpec((1, tb, wdt), lambda b, i: (b, i, 0))
    return pl.pallas_call(
        _outproj_kernel,
        out_shape=(jax.ShapeDtypeStruct((B, L, D_MODEL), F32), jax.ShapeDtypeStruct((B, L, D_MODEL), F32)),
        grid=(B, L // tb),
        in_specs=[blk(GDN_WIDTH), blk(MLA_HEADS * MLA_NOPE), blk(MLSTM_WIDTH), blk(D_MODEL),
                  per_b, per_b, per_b, full((1, D_MODEL)), full(wa.shape), full(wb.shape), full(wc.shape)],
        out_specs=(blk(D_MODEL), blk(D_MODEL)),
        compiler_params=_cparams(("arbitrary", "arbitrary"), VMEM_LIMIT),
        name="out_proj",
    )(oa, ob, oc, x, g1, sh2, sc2, ng, wa, wb, wc)


def _topk_rows(sv, io, k):
    vals, idxs = [], []
    big = jnp.float32(1e9)
    for _ in range(k):
        m = jnp.max(sv, axis=0, keepdims=True)
        ix = jnp.min(jnp.where(sv == m, io, big), axis=0, keepdims=True)
        sv = jnp.where(io == ix, -jnp.inf, sv)
        vals.append(m)
        idxs.append(ix)
    return jnp.concatenate(vals, axis=0), jnp.concatenate(idxs, axis=0)


def _route_kernel(h_ref, wq_ref, keys_ref, e_ref, g_ref, qs_ref, es_ref, gs_ref, *, tb):
    q = jnp.dot(_bf(h_ref[...]), wq_ref[...], preferred_element_type=F32)
    for g in range(2 * PEER_HEADS):
        qs_ref[g] = _bf(q[:, LANES * g:LANES * (g + 1)])
    io_k = _iota((N_KEYS, tb), 0).astype(F32)
    io_c = _iota((PEER_TOPK * PEER_TOPK, tb), 0).astype(F32)

    def head(h, carry):
        tops = []
        for p in range(2):
            sc = lax.dot_general(keys_ref[2 * h + p], qs_ref[2 * h + p], (((1,), (1,)), ((), ())),
                                 preferred_element_type=F32)
            tops.append(_topk_rows(sc, io_k, PEER_TOPK))
        (v1, i1), (v2, i2) = tops
        cand = jnp.concatenate([v1[a:a + 1] + v2 for a in range(PEER_TOPK)], axis=0)
        sc, ci = _topk_rows(cand, io_c, PEER_TOPK)
        ai = jnp.floor(ci * (1.0 / PEER_TOPK))
        bi = ci - ai * PEER_TOPK
        e1 = jnp.zeros_like(ci)
        e2 = jnp.zeros_like(ci)
        for a in range(PEER_TOPK):
            e1 = e1 + jnp.where(ai == a, i1[a:a + 1], 0.0)
            e2 = e2 + jnp.where(bi == a, i2[a:a + 1], 0.0)
        ex = jnp.exp(sc - jnp.max(sc, axis=0, keepdims=True))
        es_ref[h] = e1 * N_KEYS + e2
        gs_ref[h] = ex / jnp.sum(ex, axis=0, keepdims=True)
        return carry

    lax.fori_loop(0, PEER_HEADS, head, 0)
    e_ref[...] = es_ref[...].reshape(PEER_SLOTS, tb).T.astype(jnp.int32)
    g_ref[...] = gs_ref[...].reshape(PEER_SLOTS, tb).T


def _route_call(h2, wq, keys):
    T = h2.shape[0]
    tb = min(T, 256)
    full = lambda shape: pl.BlockSpec(shape, lambda i: (0,) * len(shape))
    return pl.pallas_call(
        functools.partial(_route_kernel, tb=tb),
        out_shape=(jax.ShapeDtypeStruct((T, PEER_SLOTS), jnp.int32), jax.ShapeDtypeStruct((T, PEER_SLOTS), F32)),
        grid=(T // tb,),
        in_specs=[pl.BlockSpec((tb, D_MODEL), lambda i: (i, 0)), full(wq.shape), full(keys.shape)],
        out_specs=(pl.BlockSpec((tb, PEER_SLOTS), lambda i: (i, 0)), pl.BlockSpec((tb, PEER_SLOTS), lambda i: (i, 0))),
        scratch_shapes=[pltpu.VMEM((2 * PEER_HEADS, tb, LANES), BF16),
                        pltpu.VMEM((PEER_HEADS, PEER_TOPK, tb), F32),
                        pltpu.VMEM((PEER_HEADS, PEER_TOPK, tb), F32)],
        compiler_params=_cparams(("arbitrary",), VMEM_LIMIT),
        name="peer_route",
    )(h2, wq, keys)


_ROW_SUB = 4
_HI_MASK = 0xFFFF0000


def _unpack_row(w):
    lo = lax.bitcast_convert_type(w << 16, F32)
    hi = lax.bitcast_convert_type(w & jnp.uint32(_HI_MASK), F32)
    return lo, hi


def _gelu(x):
    return 0.5 * x * (1.0 + lax.erf(x * (2.0 ** -0.5)))


def _peer_u_kernel(idx_ref, x_ref, g_ref, tab_ref, c_ref, part_ref, *, tb):
    ones8 = jnp.ones((8, LANES), BF16)

    def tok(t, carry):
        xv = x_ref[t]
        xlo = xv[0:_ROW_SUB]
        xhi = xv[_ROW_SUB:2 * _ROW_SUB]
        for j in range(PEER_SLOTS):
            lo, hi = _unpack_row(tab_ref[idx_ref[t, j]])
            part_ref[pl.ds(j, 1), :] = jnp.sum(lo * xlo + hi * xhi, axis=0, keepdims=True)
        part = part_ref[...]
        p_hi = _bf(part)
        p_lo = _bf(part - p_hi.astype(F32))
        nt = lambda a, b: lax.dot_general(a, b, (((1,), (1,)), ((), ())), preferred_element_type=F32)
        act = (nt(ones8, p_hi) + nt(ones8, p_lo))[0:1]
        c_ref[pl.ds(t, 1), :] = g_ref[pl.ds(t, 1), :] * _gelu(act)
        return carry

    lax.fori_loop(0, tb, tok, 0)


def _peer_v_kernel(idx_ref, c_ref, x_ref, g2_ref, tab_ref, o_ref, *, tb):
    def tok(t, carry):
        zero = jnp.zeros((_ROW_SUB, LANES), F32)
        acc = [zero, zero, zero, zero]
        for j in range(PEER_SLOTS):
            lo, hi = _unpack_row(tab_ref[idx_ref[t, j]])
            cj = c_ref[t, j]
            acc[2 * (j % 2)] = acc[2 * (j % 2)] + cj * lo
            acc[2 * (j % 2) + 1] = acc[2 * (j % 2) + 1] + cj * hi
        y = jnp.concatenate([acc[0] + acc[2], acc[1] + acc[3]], axis=0)
        o_ref[t] = x_ref[t] + g2_ref[0] * y
        return carry

    lax.fori_loop(0, tb, tok, 0)


def _table_spec():
    return pl.BlockSpec((N_EXPERTS, _ROW_SUB, LANES), lambda i: (0, 0, 0), pipeline_mode=pl.Buffered(1))


def _peer_u_call(idx, x3, gate, tab, tb):
    T = idx.shape[0]
    smem = pl.BlockSpec((tb, PEER_SLOTS), lambda i: (i, 0), memory_space=pltpu.SMEM)
    return pl.pallas_call(
        functools.partial(_peer_u_kernel, tb=tb),
        out_shape=jax.ShapeDtypeStruct((T, PEER_SLOTS), F32),
        grid=(T // tb,),
        in_specs=[smem, pl.BlockSpec((tb, 8, LANES), lambda i: (i, 0, 0)),
                  pl.BlockSpec((tb, PEER_SLOTS), lambda i: (i, 0)), _table_spec()],
        out_specs=pl.BlockSpec((tb, PEER_SLOTS), lambda i: (i, 0)),
        scratch_shapes=[pltpu.VMEM((PEER_SLOTS, LANES), F32)],
        compiler_params=_cparams(("arbitrary",), VMEM_LIMIT),
        name="peer_u",
    )(idx, x3, gate, tab)


def _peer_v_call(idx, coef, x3, g2, tab, tb, L):
    T = idx.shape[0]
    smem = pl.BlockSpec((tb, PEER_SLOTS), lambda i: (i, 0), memory_space=pltpu.SMEM)
    per_step = L // tb
    return pl.pallas_call(
        functools.partial(_peer_v_kernel, tb=tb),
        out_shape=jax.ShapeDtypeStruct((T, 8, LANES), F32),
        grid=(T // tb,),
        in_specs=[smem, smem, pl.BlockSpec((tb, 8, LANES), lambda i: (i, 0, 0)),
                  pl.BlockSpec((1, 8, LANES), lambda i: (i // per_step, 0, 0)), _table_spec()],
        out_specs=pl.BlockSpec((tb, 8, LANES), lambda i: (i, 0, 0)),
        compiler_params=_cparams(("arbitrary",), VMEM_LIMIT),
        name="peer_v",
    )(idx, coef, x3, g2, tab)


def _pack_table(tab):
    bits = lax.bitcast_convert_type(tab.astype(BF16), jnp.uint16).astype(jnp.uint32)
    half = D_MODEL // 2
    return (bits[:, :half] | (bits[:, half:] << 16)).reshape(tab.shape[0], _ROW_SUB, LANES)


def _rep_heads(w, width=HEAD_W):
    return jnp.repeat(w, width, axis=-1)


def _to_bd(s):
    B, H = s.shape[:2]
    s = s.reshape(B, H // 2, 2, HEAD_W, HEAD_W)
    z = jnp.zeros_like(s[:, :, 0])
    top = jnp.concatenate([s[:, :, 0], z], axis=-1)
    bot = jnp.concatenate([z, s[:, :, 1]], axis=-1)
    return jnp.concatenate([top, bot], axis=-2)


def _from_bd(s):
    B, P = s.shape[:2]
    return jnp.stack([s[:, :, :HEAD_W, :HEAD_W], s[:, :, HEAD_W:, HEAD_W:]], axis=2).reshape(B, 2 * P, HEAD_W, HEAD_W)


def _prep_layer(l, w):
    o = [0]
    for sz in (GDN_QKV, GDN_WIDTH, GDN_HEADS, GDN_HEADS, MLA_Q_LORA, MLA_KV_LORA, MLA_ROPE,
               MLSTM_QKV, MLSTM_WIDTH, MLSTM_HEADS, MLSTM_HEADS):
        o.append(o[-1] + sz)
    wi = w['w_in'][l]
    col = lambda i: wi[:, o[i]:o[i + 1]]
    p = {}
    p['w_gdn'] = _bf(jnp.concatenate([col(0), col(1), _rep_heads(col(2)), _rep_heads(col(3))], axis=1))
    p['w_mla'] = _bf(jnp.concatenate([col(4), col(5), col(6), jnp.zeros((D_MODEL, LANES - MLA_ROPE), F32)], axis=1))
    p['w_mls'] = _bf(jnp.concatenate([col(7), col(8), _rep_heads(col(9)), _rep_heads(col(10))], axis=1))
    p['alog'] = _rep_heads(w['gdn_a_log'][l]).reshape(1, GDN_WIDTH)
    p['dtb'] = _rep_heads(w['gdn_dt_bias'][l]).reshape(1, GDN_WIDTH)
    p['gdn_og'] = jnp.tile(w['gdn_out_g'][l], 2).reshape(1, LANES)
    p['ib'] = _rep_heads(w['mlstm_i_bias'][l]).reshape(1, MLSTM_WIDTH)
    p['fb'] = _rep_heads(w['mlstm_f_bias'][l]).reshape(1, MLSTM_WIDTH)
    p['mls_og'] = w['mlstm_out_g'][l].reshape(1, MLSTM_WIDTH)
    wuq = w['mla_w_uq'][l].reshape(MLA_Q_LORA, MLA_HEADS, MLA_QK)
    p['wuq'] = _bf(jnp.pad(wuq, ((0, 0), (0, 0), (0, _MLA_HW - MLA_QK))).reshape(MLA_Q_LORA, MLA_HEADS * _MLA_HW))
    p['wukv'] = _bf(w['mla_w_ukv'][l])
    p['qag'] = w['mla_q_a_g'][l].reshape(1, MLA_Q_LORA)
    p['kvag'] = w['mla_kv_a_g'][l].reshape(1, MLA_KV_LORA)
    p['qgain'] = jnp.pad(w['mla_q_gain'][l], (0, LANES - MLA_QK)).reshape(1, LANES)
    kg = w['mla_k_gain'][l]
    p['kgain'] = jnp.pad(kg[:MLA_NOPE], (0, LANES - MLA_NOPE)).reshape(1, LANES)
    p['krgain'] = jnp.pad(kg[MLA_NOPE:], (0, LANES - MLA_ROPE)).reshape(1, LANES)
    wo = w['w_out'][l]
    p['wo_a'] = _bf(wo[0:GDN_WIDTH])
    p['wo_b'] = _bf(wo[GDN_WIDTH:2 * GDN_WIDTH])
    p['wo_c'] = _bf(wo[2 * GDN_WIDTH:])
    p['wq'] = _bf(w['peer_w_q'][l])
    p['keys'] = _bf(w['peer_sub_keys'][l].reshape(2 * PEER_HEADS, N_KEYS, LANES))
    p['u_tab'] = _pack_table(w['peer_u'][l])
    p['v_tab'] = _pack_table(w['peer_v'][l])
    p['conv_w'] = w['gdn_conv_w'][l]
    p['norm_attn_g'] = w['norm_attn_g'][l].reshape(1, D_MODEL)
    p['norm_ffn_g'] = w['norm_ffn_g'][l].reshape(1, D_MODEL)
    return p


def _rope_tables(pos):
    half = MLA_ROPE // 2
    inv = ROPE_THETA ** (-jnp.arange(half, dtype=F32) / half)
    ang = pos.astype(F32)[:, None] * inv[None, :]
    cos, sin = jnp.cos(ang), jnp.sin(ang)
    n = pos.shape[0]
    c2 = jnp.concatenate([cos, cos], axis=1)
    s2 = jnp.concatenate([-sin, sin], axis=1)
    padq = lambda t: jnp.pad(t, ((0, 0), (MLA_NOPE, LANES - MLA_QK)))
    padk = lambda t: jnp.pad(t, ((0, 0), (0, LANES - MLA_ROPE)))
    return padq(c2), padq(s2), padk(c2), padk(s2)


def _pad_rows(a, n):
    return jnp.pad(a, ((0, 0), (0, n - a.shape[1]), (0, 0)))


def _layer(x, mod, p, st, rope, prompt):
    B, L, _ = x.shape
    sh1, sc1, g1, sh2, sc2, g2 = [m.reshape(B, 1, D_MODEL) for m in jnp.split(mod, 6, axis=-1)]
    pg, pm, pl_ = _inproj_call(x, sh1, sc1, p['norm_attn_g'], p['w_gdn'], p['w_mla'], p['w_mls'])
    Lp = -(-L // CHUNK) * CHUNK
    if Lp != L:
        pg, pl_ = _pad_rows(pg, Lp), _pad_rows(pl_, Lp)
    o_a, gdn_bd, conv_new = _gdn_call(pg, st['gdn_conv'], _to_bd(st['gdn']), p['conv_w'],
                                      p['alog'], p['dtb'], p['gdn_og'], L)
    o_c, mc_bd, mn_p, mm_p = _mlstm_call(
        pl_, _to_bd(st['mlstm_c']), st['mlstm_n'].reshape(B, MLSTM_HEADS // 2, 1, LANES),
        _rep_heads(st['mlstm_m']).reshape(B, MLSTM_HEADS // 2, 1, LANES), p['ib'], p['fb'], p['mls_og'], L)
    o_a, o_c = o_a[:, :L], o_c[:, :L]
    q, ckv, kr = _mla_q_call(pm, p['qag'], p['wuq'], p['qgain'], p['kvag'], p['krgain'], *rope)
    krp = lambda t: jnp.pad(t, ((0, 0), (0, 0), (MLA_NOPE, LANES - MLA_QK)))
    k, kv = _mla_kv_call(ckv, krp(kr), p['wukv'], p['kgain'])
    if not prompt:
        k_c, kv_c = _mla_kv_call(st['mla_latent'], krp(st['mla_krope']), p['wukv'], p['kgain'])
        k = jnp.concatenate([k_c, k], axis=1)
        kv = jnp.concatenate([kv_c, kv], axis=1)
    o_b = _attn_call(q, k, kv, prompt)
    x1, h2 = _outproj_call(o_a, o_b, o_c, x, g1, sh2, sc2, p['norm_ffn_g'], p['wo_a'], p['wo_b'], p['wo_c'])
    T = B * L
    h2f = h2.reshape(T, D_MODEL)
    idx, gate = _route_call(h2f, p['wq'], p['keys'])
    tb = min(L, 64)
    coef = _peer_u_call(idx, h2f.reshape(T, 8, LANES), gate, p['u_tab'], tb)
    x2 = _peer_v_call(idx, coef, x1.reshape(T, 8, LANES), g2.reshape(B, 8, LANES), p['v_tab'], tb, L)
    new_state = (ckv, kr, _from_bd(gdn_bd), conv_new, _from_bd(mc_bd),
                 mn_p.reshape(B, MLSTM_HEADS, HEAD_W), mm_p.reshape(B, MLSTM_HEADS, HEAD_W)[:, :, 0])
    return x2.reshape(B, L, D_MODEL), new_state


def kernel(x_prompt, x_sample, c_prompt, c_sample, cache_mla_latent, cache_mla_krope, state_gdn, state_gdn_conv, state_mlstm_c, state_mlstm_n, state_mlstm_m, ada_w, ada_b, norm_attn_g, norm_ffn_g, w_in, gdn_conv_w, gdn_a_log, gdn_dt_bias, gdn_out_g, mla_q_a_g, mla_w_uq, mla_kv_a_g, mla_w_ukv, mla_q_gain, mla_k_gain, mlstm_i_bias, mlstm_f_bias, mlstm_out_g, w_out, peer_w_q, peer_sub_keys, peer_u, peer_v):
    w = dict(ada_w=ada_w, ada_b=ada_b, norm_attn_g=norm_attn_g, norm_ffn_g=norm_ffn_g, w_in=w_in,
             gdn_conv_w=gdn_conv_w, gdn_a_log=gdn_a_log, gdn_dt_bias=gdn_dt_bias, gdn_out_g=gdn_out_g,
             mla_q_a_g=mla_q_a_g, mla_w_uq=mla_w_uq, mla_kv_a_g=mla_kv_a_g, mla_w_ukv=mla_w_ukv,
             mla_q_gain=mla_q_gain, mla_k_gain=mla_k_gain, mlstm_i_bias=mlstm_i_bias,
             mlstm_f_bias=mlstm_f_bias, mlstm_out_g=mlstm_out_g, w_out=w_out, peer_w_q=peer_w_q,
             peer_sub_keys=peer_sub_keys, peer_u=peer_u, peer_v=peer_v)
    B, Lp, _ = x_prompt.shape
    Bs, Ls, _ = x_sample.shape
    past = cache_mla_latent.shape[2]
    rope_p = _rope_tables(jnp.arange(Lp, dtype=jnp.int32))
    rope_s = _rope_tables(past + jnp.arange(Ls, dtype=jnp.int32))
    mods = _mod_call(jnp.concatenate([c_prompt, c_sample], axis=0), ada_w, ada_b)
    xp, xs = x_prompt, x_sample
    new_p, new_s = [], []
    for l in range(DEPTH):
        p = _prep_layer(l, w)
        st_p = {
            'gdn': jnp.zeros((B, GDN_HEADS, HEAD_W, HEAD_W), F32),
            'gdn_conv': jnp.zeros((B, CONV_W - 1, GDN_QKV), F32),
            'mlstm_c': jnp.zeros((B, MLSTM_HEADS, HEAD_W, HEAD_W), F32),
            'mlstm_n': jnp.zeros((B, MLSTM_HEADS, HEAD_W), F32),
            'mlstm_m': jnp.zeros((B, MLSTM_HEADS), F32),
        }
        st_s = {
            'mla_latent': cache_mla_latent[l], 'mla_krope': cache_mla_krope[l],
            'gdn': state_gdn[l], 'gdn_conv': state_gdn_conv[l],
            'mlstm_c': state_mlstm_c[l], 'mlstm_n': state_mlstm_n[l], 'mlstm_m': state_mlstm_m[l],
        }
        xp, sp = _layer(xp, mods[l, :B], p, st_p, rope_p, True)
        xs, ss = _layer(xs, mods[l, B:], p, st_s, rope_s, False)
        new_p.append(sp)
        new_s.append(ss)
    outs_p = [jnp.stack([s[i] for s in new_p]) for i in range(7)]
    outs_s = [jnp.stack([s[i] for s in new_s]) for i in range(7)]
    return (xp, xs, *outs_p, *outs_s)
```

```python
import functools
import math

import jax
import jax.numpy as jnp
from jax import lax
from jax.experimental import pallas as pl
from jax.experimental.pallas import tpu as pltpu

F32 = jnp.float32
BF16 = jnp.bfloat16

D_MODEL = 1024
DEPTH = 2
CHUNK = 64
EPS = 1e-6
GDN_HEADS = 6
GDN_DK = 64
GDN_WIDTH = 384
GDN_QKV = 1152
CONV_W = 4
MLA_HEADS = 6
MLA_NOPE = 64
MLA_ROPE = 32
MLA_QK = 96
MLA_Q_LORA = 384
MLA_KV_LORA = 256
ROPE_THETA = 10000.0
MLSTM_HEADS = 4
MLSTM_WIDTH = 256
MLSTM_QKV = 768
PEER_HEADS = 8
N_KEYS = 128
N_EXPERTS = N_KEYS * N_KEYS
PEER_TOPK = 16
PEER_SLOTS = PEER_HEADS * PEER_TOPK

HEAD_W = 64
LANES = 128
NEG = -1e30
VMEM_LIMIT = 56 * 1024 * 1024


def _bf(x):
    return x.astype(BF16)


def _mm(a, b):
    return jnp.dot(_bf(a), _bf(b), preferred_element_type=F32)


def _mm_nt(a, b):
    return lax.dot_general(_bf(a), _bf(b), (((1,), (1,)), ((), ())), preferred_element_type=F32)


def _split3(x):
    hi = _bf(x)
    r = x - hi.astype(F32)
    mid = _bf(r)
    lo = _bf(r - mid.astype(F32))
    return hi, mid, lo


def _mm_sel_exact(sel_bf, x):
    hi, mid, lo = _split3(x)
    d = lambda t: jnp.dot(sel_bf, t, preferred_element_type=F32)
    return d(hi) + d(mid) + d(lo)


def _sigmoid(x):
    return 1.0 / (1.0 + jnp.exp(-x))


def _silu(x):
    return x * _sigmoid(x)


def _softplus(x):
    return jnp.maximum(x, 0.0) + jnp.log1p(jnp.exp(-jnp.abs(x)))


def _iota(shape, axis):
    return lax.broadcasted_iota(jnp.int32, shape, axis)


def _pair_consts(C):
    row = _iota((C, LANES), 0)
    lane = _iota((C, LANES), 1)
    s = lane & (HEAD_W - 1)
    lo_half = lane < HEAD_W
    r2 = _iota((LANES, LANES), 0)
    l2 = _iota((LANES, LANES), 1)
    same_half = (r2 >> 6) == (l2 >> 6)
    return row, lane, s, lo_half, same_half


def _bd(y, lo_half):
    z = jnp.zeros_like(y)
    return jnp.concatenate([jnp.where(lo_half, y, z), jnp.where(lo_half, z, y)], axis=0)


def _diag_row(x, row, s):
    return jnp.sum(jnp.where(row == s, x, 0.0), axis=0, keepdims=True)


def _cparams(sem, vmem=None):
    kw = dict(dimension_semantics=sem)
    if vmem is not None:
        kw["vmem_limit_bytes"] = vmem
    return pltpu.CompilerParams(**kw)


def _mod_kernel(c_ref, w_ref, b_ref, o_ref):
    c = c_ref[...]
    o_ref[0] = _mm(_silu(c), w_ref[0]) + b_ref[0]


def _mod_call(c_all, ada_w, ada_b):
    nb = c_all.shape[0]
    nj = 6
    return pl.pallas_call(
        _mod_kernel,
        out_shape=jax.ShapeDtypeStruct((DEPTH, nb, 6 * D_MODEL), F32),
        grid=(DEPTH, nj),
        in_specs=[
            pl.BlockSpec((nb, D_MODEL), lambda l, j: (0, 0)),
            pl.BlockSpec((1, D_MODEL, D_MODEL), lambda l, j: (l, 0, j)),
            pl.BlockSpec((1, 1, D_MODEL), lambda l, j: (l, 0, j)),
        ],
        out_specs=pl.BlockSpec((1, nb, D_MODEL), lambda l, j: (l, 0, j)),
        compiler_params=_cparams(("arbitrary", "arbitrary")),
        name="adaln_mod",
    )(c_all, ada_w, ada_b.reshape(DEPTH, 1, 6 * D_MODEL))


def _inproj_kernel(x_ref, sh_ref, sc_ref, g_ref, wg_ref, wm_ref, wl_ref, og_ref, om_ref, ol_ref):
    x = x_ref[0]
    y = x * lax.rsqrt(jnp.mean(x * x, axis=-1, keepdims=True) + EPS) * g_ref[...]
    h = _bf(y * (1.0 + sc_ref[0]) + sh_ref[0])
    og_ref[0] = jnp.dot(h, wg_ref[...], preferred_element_type=F32)
    om_ref[0] = jnp.dot(h, wm_ref[...], preferred_element_type=F32)
    ol_ref[0] = jnp.dot(h, wl_ref[...], preferred_element_type=F32)


def _inproj_call(x, sh, sc, g, wg, wm, wl):
    B, L, _ = x.shape
    tb = min(L, 512)
    ng, nm, nl = wg.shape[1], wm.shape[1], wl.shape[1]
    full = lambda shape: pl.BlockSpec(shape, lambda b, i: (0,) * len(shape))
    per_b = pl.BlockSpec((1, 1, D_MODEL), lambda b, i: (b, 0, 0))
    return pl.pallas_call(
        _inproj_kernel,
        out_shape=(jax.ShapeDtypeStruct((B, L, ng), F32),
                   jax.ShapeDtypeStruct((B, L, nm), F32),
                   jax.ShapeDtypeStruct((B, L, nl), F32)),
        grid=(B, L // tb),
        in_specs=[pl.BlockSpec((1, tb, D_MODEL), lambda b, i: (b, i, 0)), per_b, per_b,
                  full((1, D_MODEL)), full(wg.shape), full(wm.shape), full(wl.shape)],
        out_specs=(pl.BlockSpec((1, tb, ng), lambda b, i: (b, i, 0)),
                   pl.BlockSpec((1, tb, nm), lambda b, i: (b, i, 0)),
                   pl.BlockSpec((1, tb, nl), lambda b, i: (b, i, 0))),
        compiler_params=_cparams(("arbitrary", "arbitrary"), VMEM_LIMIT),
        name="in_proj",
    )(x, sh, sc, g, wg, wm, wl)


_GDN_Z0 = GDN_QKV
_GDN_B0 = GDN_QKV + GDN_WIDTH
_GDN_A0 = _GDN_B0 + GDN_WIDTH
_GDN_COLS = _GDN_A0 + GDN_WIDTH
_CONV_PAD = 8


def _gdn_kernel(pg_ref, conv0_ref, s0_ref, convw_ref, alog_ref, dtb_ref, og_ref,
                o_ref, sfin_ref, convfin_ref, xp_ref, s_ref, *, C, n_valid, n_chunks):
    n = pl.program_id(1)
    hist = CONV_W - 1

    @pl.when(n == 0)
    def _():
        xp_ref[0:_CONV_PAD, :] = jnp.zeros((_CONV_PAD, GDN_QKV), F32)
        xp_ref[_CONV_PAD - hist:_CONV_PAD, :] = conv0_ref[0]
        s_ref[...] = s0_ref[0]

    a_pre = pg_ref[0, :, 0:GDN_QKV]
    xp_ref[_CONV_PAD:_CONV_PAD + C, :] = a_pre
    w = convw_ref[...]
    y = w[CONV_W - 1:CONV_W] * a_pre
    for j in range(CONV_W - 1):
        y = y + w[j:j + 1] * xp_ref[_CONV_PAD - hist + j:_CONV_PAD - hist + j + C, :]
    y = _silu(y)

    row, lane, s, lo_half, same_half = _pair_consts(C)
    ones_blk = jnp.where(same_half, 1.0, 0.0).astype(BF16)
    tri = jnp.where(_iota((C, C), 1) <= _iota((C, C), 0), 1.0, 0.0).astype(BF16)
    causal = s <= row
    strict = s < row
    eye = s == row

    def hsum(x):
        return jnp.dot(_bf(x), ones_blk, preferred_element_type=F32)

    beta_all = _sigmoid(pg_ref[0, :, _GDN_B0:_GDN_A0])
    gl_all = -jnp.exp(alog_ref[...]) * _softplus(pg_ref[0, :, _GDN_A0:_GDN_COLS] + dtb_ref[...])
    if n_valid < C * n_chunks:
        valid = (_iota((C, GDN_WIDTH), 0) + n * C) < n_valid
        beta_all = jnp.where(valid, beta_all, 0.0)
        gl_all = jnp.where(valid, gl_all, 0.0)
    g_all = _mm_sel_exact(tri, gl_all)

    P = range(GDN_HEADS // 2)
    grp = lambda a, base, j: a[:, base + LANES * j:base + LANES * (j + 1)]
    bd = lambda a: _bd(a, lo_half)
    qk_raw = [jnp.concatenate([grp(y, 0, j), grp(y, GDN_WIDTH, j)], axis=0) for j in P]
    ssq = [hsum(t * t) for t in qk_raw]
    q2 = [qk_raw[j][0:C] * lax.rsqrt(ssq[j][0:C] + EPS) * (GDN_DK ** -0.5) for j in P]
    k2 = [qk_raw[j][C:2 * C] * lax.rsqrt(ssq[j][C:2 * C] + EPS) for j in P]
    v2 = [grp(y, 2 * GDN_WIDTH, j) for j in P]
    beta = [grp(beta_all, 0, j) for j in P]
    G = [grp(g_all, 0, j) for j in P]
    eG = [jnp.exp(G[j]) for j in P]
    decay = []
    for j in P:
        diff = G[j] - _diag_row(G[j], row, s)
        decay.append(jnp.where(causal, jnp.exp(jnp.where(causal, diff, 0.0)), 0.0))
    kbd = [bd(k2[j]) for j in P]
    kq = [_mm_nt(jnp.concatenate([k2[j], q2[j]], axis=0), kbd[j]) for j in P]
    A = [jnp.where(strict, beta[j] * kq[j][0:C] * decay[j], 0.0) for j in P]
    qk = [kq[j][C:2 * C] * decay[j] for j in P]
    first = ((row & 1) == 1) & (s == row - 1)
    T = [jnp.where(eye, 1.0, 0.0) - jnp.where(first, A[j], 0.0) for j in P]
    for lg in range(1, 6):
        sub = (((row >> lg) & 1) == 1) & ((s >> lg) == (row >> lg) - 1)
        t1 = [_mm(T[j], bd(jnp.where(sub, A[j], 0.0))) for j in P]
        t2 = [_mm(t1[j], bd(T[j])) for j in P]
        T = [T[j] - t2[j] for j in P]
    uw = [_mm(T[j], jnp.concatenate([bd(beta[j] * v2[j]), bd(beta[j] * eG[j] * k2[j])], axis=1)) for j in P]
    S = [s_ref[j] for j in P]
    ws = [_mm(jnp.concatenate([uw[j][:, LANES:2 * LANES], q2[j] * eG[j]], axis=0), S[j]) for j in P]
    wn = [uw[j][:, 0:LANES] - ws[j][0:C] for j in P]
    o = [ws[j][C:2 * C] + _mm(qk[j], bd(wn[j])) for j in P]
    g_last = [G[j][C - 1:C, :] for j in P]
    kend_t = [(k2[j] * jnp.exp(g_last[j] - G[j])).T for j in P]
    upd = [_mm(kend_t[j], wn[j]) for j in P]
    oss = [hsum(o[j] * o[j]) for j in P]
    for j in P:
        s_ref[j] = jnp.exp(g_last[j]) * S[j] + jnp.where(same_half, upd[j], 0.0)
        on = o[j] * lax.rsqrt(oss[j] * (1.0 / HEAD_W) + EPS) * og_ref[...]
        z = pg_ref[0, :, _GDN_Z0 + LANES * j:_GDN_Z0 + LANES * (j + 1)]
        o_ref[0, :, LANES * j:LANES * (j + 1)] = on * _silu(z)

    @pl.when(n < n_chunks - 1)
    def _():
        xp_ref[_CONV_PAD - hist:_CONV_PAD, :] = xp_ref[_CONV_PAD + C - hist:_CONV_PAD + C, :]

    @pl.when(n == n_chunks - 1)
    def _():
        lv = n_valid - (n_chunks - 1) * C
        convfin_ref[0] = xp_ref[_CONV_PAD + lv - hist:_CONV_PAD + lv, :]
        sfin_ref[0] = s_ref[...]


def _gdn_call(pg, conv0, s0_bd, conv_w, alog_rep, dtb_rep, og_rep, n_valid):
    B, Lp, _ = pg.shape
    C = CHUNK
    n_chunks = Lp // C
    npair = GDN_HEADS // 2
    kern = functools.partial(_gdn_kernel, C=C, n_valid=n_valid, n_chunks=n_chunks)
    full = lambda shape: pl.BlockSpec(shape, lambda b, n: (0,) * len(shape))
    return pl.pallas_call(
        kern,
        out_shape=(jax.ShapeDtypeStruct((B, Lp, GDN_WIDTH), F32),
                   jax.ShapeDtypeStruct((B, npair, LANES, LANES), F32),
                   jax.ShapeDtypeStruct((B, CONV_W - 1, GDN_QKV), F32)),
        grid=(B, n_chunks),
        in_specs=[pl.BlockSpec((1, C, _GDN_COLS), lambda b, n: (b, n, 0)),
                  pl.BlockSpec((1, CONV_W - 1, GDN_QKV), lambda b, n: (b, 0, 0)),
                  pl.BlockSpec((1, npair, LANES, LANES), lambda b, n: (b, 0, 0, 0)),
                  full((CONV_W, GDN_QKV)), full((1, GDN_WIDTH)), full((1, GDN_WIDTH)), full((1, LANES))],
        out_specs=(pl.BlockSpec((1, C, GDN_WIDTH), lambda b, n: (b, n, 0)),
                   pl.BlockSpec((1, npair, LANES, LANES), lambda b, n: (b, 0, 0, 0)),
                   pl.BlockSpec((1, CONV_W - 1, GDN_QKV), lambda b, n: (b, 0, 0))),
        scratch_shapes=[pltpu.VMEM((_CONV_PAD + C, GDN_QKV), F32),
                        pltpu.VMEM((npair, LANES, LANES), F32)],
        compiler_params=_cparams(("arbitrary", "arbitrary")),
        name="gdn_chunks",
    )(pg, conv0, s0_bd, conv_w, alog_rep, dtb_rep, og_rep)


_ML_O0 = MLSTM_QKV
_ML_I0 = _ML_O0 + MLSTM_WIDTH
_ML_F0 = _ML_I0 + MLSTM_WIDTH
_ML_COLS = _ML_F0 + MLSTM_WIDTH


def _mlstm_kernel(pm_ref, c0_ref, n0_ref, m0_ref, ib_ref, fb_ref, og_ref,
                  o_ref, cfin_ref, nfin_ref, mfin_ref, c_ref, n_ref, m_ref, *, C, n_valid, n_chunks):
    n = pl.program_id(1)

    @pl.when(n == 0)
    def _():
        c_ref[...] = c0_ref[0]
        n_ref[...] = n0_ref[0]
        m_ref[...] = m0_ref[0]

    row, lane, s, lo_half, same_half = _pair_consts(C)
    ones_blk = jnp.where(same_half, 1.0, 0.0).astype(BF16)
    tri = jnp.where(_iota((C, C), 1) <= _iota((C, C), 0), 1.0, 0.0).astype(BF16)
    causal = s <= row

    def hsum(x):
        return jnp.dot(_bf(x), ones_blk, preferred_element_type=F32)

    ig_all = pm_ref[0, :, _ML_I0:_ML_F0] + ib_ref[...]
    lf_all = -_softplus(-(pm_ref[0, :, _ML_F0:_ML_COLS] + fb_ref[...]))
    if n_valid < C * n_chunks:
        valid = (_iota((C, MLSTM_WIDTH), 0) + n * C) < n_valid
        ig_all = jnp.where(valid, ig_all, NEG)
        lf_all = jnp.where(valid, lf_all, 0.0)
    f_all = _mm_sel_exact(tri, lf_all)

    P = range(MLSTM_HEADS // 2)
    col = lambda base, j: pm_ref[0, :, base + LANES * j:base + LANES * (j + 1)]
    bd = lambda a: _bd(a, lo_half)
    q2 = [col(0, j) for j in P]
    k2 = [col(MLSTM_WIDTH, j) * (HEAD_W ** -0.5) for j in P]
    v2 = [col(2 * MLSTM_WIDTH, j) for j in P]
    ig = [ig_all[:, LANES * j:LANES * (j + 1)] for j in P]
    F = [f_all[:, LANES * j:LANES * (j + 1)] for j in P]
    dm, dmax = [], []
    for j in P:
        d = jnp.where(causal, F[j] - _diag_row(F[j], row, s) + _diag_row(ig[j], row, s), NEG)
        d_e = jnp.max(jnp.where(lo_half, d, NEG), axis=1, keepdims=True)
        d_o = jnp.max(jnp.where(lo_half, NEG, d), axis=1, keepdims=True)
        dm.append(d)
        dmax.append(jnp.where(lo_half, d_e, d_o))
    ms = [m_ref[j] for j in P]
    cs = [c_ref[j] for j in P]
    ns = [n_ref[j] for j in P]
    mt = [jnp.maximum(F[j] + ms[j], dmax[j]) for j in P]
    inter = [jnp.exp(F[j] + ms[j] - mt[j]) for j in P]
    qk = [_mm_nt(q2[j], bd(k2[j])) for j in P]
    qc = [_mm(q2[j], cs[j]) for j in P]
    qn = [hsum(q2[j] * ns[j]) for j in P]
    wgt = [jnp.exp(dm[j] - mt[j]) * qk[j] for j in P]
    wv = [_mm(wgt[j], bd(v2[j])) for j in P]
    wsum = [hsum(wgt[j]) for j in P]
    fe = [F[j][C - 1:C, :] for j in P]
    se = [fe[j] - F[j] + ig[j] for j in P]
    m_new = [jnp.maximum(fe[j] + ms[j], jnp.max(se[j], axis=0, keepdims=True)) for j in P]
    kw = [k2[j] * jnp.exp(se[j] - m_new[j]) for j in P]
    upd = [_mm(kw[j].T, v2[j]) for j in P]
    o = []
    for j in P:
        den = inter[j] * qn[j] + wsum[j]
        h = (inter[j] * qc[j] + wv[j]) / jnp.maximum(jnp.abs(den), jnp.exp(-mt[j]))
        o.append(_sigmoid(col(_ML_O0, j)) * h)
    oss = [hsum(o[j] * o[j]) for j in P]
    for j in P:
        sc = jnp.exp(fe[j] + ms[j] - m_new[j])
        c_ref[j] = sc * cs[j] + jnp.where(same_half, upd[j], 0.0)
        n_ref[j] = sc * ns[j] + jnp.sum(kw[j], axis=0, keepdims=True)
        m_ref[j] = m_new[j]
        o_ref[0, :, LANES * j:LANES * (j + 1)] = (o[j] * lax.rsqrt(oss[j] * (1.0 / HEAD_W) + EPS)
                                                  * og_ref[:, LANES * j:LANES * (j + 1)])

    @pl.when(n == n_chunks - 1)
    def _():
        cfin_ref[0] = c_ref[...]
        nfin_ref[0] = n_ref[...]
        mfin_ref[0] = m_ref[...]


def _mlstm_call(pm, c0_bd, n0, m0_rep, ib_rep, fb_rep, og, n_valid):
    B, Lp, _ = pm.shape
    C = CHUNK
    n_chunks = Lp // C
    npair = MLSTM_HEADS // 2
    kern = functools.partial(_mlstm_kernel, C=C, n_valid=n_valid, n_chunks=n_chunks)
    full = lambda shape: pl.BlockSpec(shape, lambda b, n: (0,) * len(shape))
    st_c = pl.BlockSpec((1, npair, LANES, LANES), lambda b, n: (b, 0, 0, 0))
    st_v = pl.BlockSpec((1, npair, 1, LANES), lambda b, n: (b, 0, 0, 0))
    return pl.pallas_call(
        kern,
        out_shape=(jax.ShapeDtypeStruct((B, Lp, MLSTM_WIDTH), F32),
                   jax.ShapeDtypeStruct((B, npair, LANES, LANES), F32),
                   jax.ShapeDtypeStruct((B, npair, 1, LANES), F32),
                   jax.ShapeDtypeStruct((B, npair, 1, LANES), F32)),
        grid=(B, n_chunks),
        in_specs=[pl.BlockSpec((1, C, _ML_COLS), lambda b, n: (b, n, 0)), st_c, st_v, st_v,
                  full((1, MLSTM_WIDTH)), full((1, MLSTM_WIDTH)), full((1, MLSTM_WIDTH))],
        out_specs=(pl.BlockSpec((1, C, MLSTM_WIDTH), lambda b, n: (b, n, 0)), st_c, st_v, st_v),
        scratch_shapes=[pltpu.VMEM((npair, LANES, LANES), F32),
                        pltpu.VMEM((npair, 1, LANES), F32),
                        pltpu.VMEM((npair, 1, LANES), F32)],
        compiler_params=_cparams(("arbitrary", "arbitrary")),
        name="mlstm_chunks",
    )(pm, c0_bd, n0, m0_rep, ib_rep, fb_rep, og)


_MLA_C0 = MLA_Q_LORA
_MLA_R0 = MLA_Q_LORA + MLA_KV_LORA
_MLA_COLS = _MLA_R0 + LANES
_MLA_HW = LANES
_Q_SCALE = MLA_QK ** -0.5 * math.log2(math.e)


def _swap_halves(x, lane, base, half):
    up = pltpu.roll(x, LANES - half, 1)
    dn = pltpu.roll(x, half, 1)
    first = (lane >= base) & (lane < base + half)
    second = (lane >= base + half) & (lane < base + 2 * half)
    return jnp.where(first, up, jnp.where(second, dn, 0.0))


def _mla_q_kernel(pm_ref, qag_ref, wuq_ref, qgain_ref, kvag_ref, krgain_ref,
                  cosq_ref, sinq_ref, cosk_ref, sink_ref, q_ref, lat_ref, kr_ref, *, tb):
    lane = _iota((tb, LANES), 1)
    ql = pm_ref[0, :, 0:_MLA_C0]
    ql = ql * lax.rsqrt(jnp.mean(ql * ql, axis=-1, keepdims=True) + EPS) * qag_ref[...]
    q = jnp.dot(_bf(ql), wuq_ref[...], preferred_element_type=F32)
    nope = lane < MLA_NOPE
    rope = (lane >= MLA_NOPE) & (lane < MLA_QK)
    cosq = cosq_ref[...]
    sinq = sinq_ref[...]
    for h in range(MLA_HEADS):
        qh = q[:, _MLA_HW * h:_MLA_HW * (h + 1)]
        sq = qh * qh
        ss_n = jnp.sum(jnp.where(nope, sq, 0.0), axis=-1, keepdims=True) * (1.0 / MLA_NOPE)
        ss_r = jnp.sum(jnp.where(rope, sq, 0.0), axis=-1, keepdims=True) * (1.0 / MLA_ROPE)
        rinv = jnp.where(nope, lax.rsqrt(ss_n + EPS), lax.rsqrt(ss_r + EPS))
        qn = qh * rinv * qgain_ref[...]
        sw = _swap_halves(qn, lane, MLA_NOPE, MLA_ROPE // 2)
        qr = jnp.where(nope, qn, qn * cosq + sw * sinq)
        q_ref[0, :, _MLA_HW * h:_MLA_HW * (h + 1)] = _bf(qr * _Q_SCALE)
    ckv = pm_ref[0, :, _MLA_C0:_MLA_R0]
    lat_ref[0] = ckv * lax.rsqrt(jnp.mean(ckv * ckv, axis=-1, keepdims=True) + EPS) * kvag_ref[...]
    kx = pm_ref[0, :, _MLA_R0:_MLA_COLS]
    ssk = jnp.sum(kx * kx, axis=-1, keepdims=True) * (1.0 / MLA_ROPE)
    kn = kx * lax.rsqrt(ssk + EPS) * krgain_ref[...]
    swk = _swap_halves(kn, lane, 0, MLA_ROPE // 2)
    kr = kn * cosk_ref[...] + swk * sink_ref[...]
    kr_ref[0] = kr[:, 0:MLA_ROPE]


def _mla_q_call(pm, qag, wuq, qgain, kvag, krgain, cosq, sinq, cosk, sink):
    B, L, _ = pm.shape
    tb = min(L, 512)
    full = lambda shape: pl.BlockSpec(shape, lambda b, i: (0,) * len(shape))
    tab = pl.BlockSpec((tb, LANES), lambda b, i: (i, 0))
    nq = MLA_HEADS * _MLA_HW
    return pl.pallas_call(
        functools.partial(_mla_q_kernel, tb=tb),
        out_shape=(jax.ShapeDtypeStruct((B, L, nq), BF16),
                   jax.ShapeDtypeStruct((B, L, MLA_KV_LORA), F32),
                   jax.ShapeDtypeStruct((B, L, MLA_ROPE), F32)),
        grid=(B, L // tb),
        in_specs=[pl.BlockSpec((1, tb, _MLA_COLS), lambda b, i: (b, i, 0)),
                  full((1, MLA_Q_LORA)), full((MLA_Q_LORA, nq)), full((1, LANES)),
                  full((1, MLA_KV_LORA)), full((1, LANES)), tab, tab, tab, tab],
        out_specs=(pl.BlockSpec((1, tb, nq), lambda b, i: (b, i, 0)),
                   pl.BlockSpec((1, tb, MLA_KV_LORA), lambda b, i: (b, i, 0)),
                   pl.BlockSpec((1, tb, MLA_ROPE), lambda b, i: (b, i, 0))),
        compiler_params=_cparams(("arbitrary", "arbitrary")),
        name="mla_q_proj",
    )(pm, qag, wuq, qgain, kvag, krgain, cosq, sinq, cosk, sink)


def _mla_kv_kernel(ckv_ref, krp_ref, wukv_ref, kgain_ref, k_ref, kv_ref, *, tb):
    lane = _iota((tb, LANES), 1)
    nope = lane < MLA_NOPE
    kv = jnp.dot(_bf(ckv_ref[0]), wukv_ref[...], preferred_element_type=F32)
    krp = krp_ref[0]
    for h in range(MLA_HEADS):
        g = kv[:, _MLA_HW * h:_MLA_HW * (h + 1)]
        ss = jnp.sum(jnp.where(nope, g * g, 0.0), axis=-1, keepdims=True) * (1.0 / MLA_NOPE)
        k_ref[0, :, _MLA_HW * h:_MLA_HW * (h + 1)] = _bf(g * lax.rsqrt(ss + EPS) * kgain_ref[...] + krp)
    kv_ref[0] = _bf(kv)


def _mla_kv_call(ckv_all, krp_all, wukv, kgain):
    B, Lk, _ = ckv_all.shape
    tb = 512 if Lk % 512 == 0 else Lk
    nk = MLA_HEADS * _MLA_HW
    full = lambda shape: pl.BlockSpec(shape, lambda b, i: (0,) * len(shape))
    return pl.pallas_call(
        functools.partial(_mla_kv_kernel, tb=tb),
        out_shape=(jax.ShapeDtypeStruct((B, Lk, nk), BF16), jax.ShapeDtypeStruct((B, Lk, nk), BF16)),
        grid=(B, Lk // tb),
        in_specs=[pl.BlockSpec((1, tb, MLA_KV_LORA), lambda b, i: (b, i, 0)),
                  pl.BlockSpec((1, tb, LANES), lambda b, i: (b, i, 0)),
                  full((MLA_KV_LORA, nk)), full((1, LANES))],
        out_specs=(pl.BlockSpec((1, tb, nk), lambda b, i: (b, i, 0)),
                   pl.BlockSpec((1, tb, nk), lambda b, i: (b, i, 0))),
        compiler_params=_cparams(("arbitrary", "arbitrary"), VMEM_LIMIT),
        name="mla_kv_proj",
    )(ckv_all, krp_all, wukv, kgain)


def _attn_kernel(q_ref, k_ref, kv_ref, o_ref, m_ref, l_ref, acc_ref, *, tq, tk, nk, causal):
    qi = pl.program_id(2)
    ki = pl.program_id(3)
    last = qi if causal else nk - 1

    @pl.when(ki == 0)
    def _():
        m_ref[...] = jnp.full(m_ref.shape, NEG, F32)
        l_ref[...] = jnp.zeros(l_ref.shape, F32)
        acc_ref[...] = jnp.zeros(acc_ref.shape, F32)

    if tk % LANES == 0:
        widen = lambda v: jnp.concatenate([v] * (tk // LANES), axis=1)
    else:
        widen = lambda v: v[:, 0:1]

    def step(masked):
        if masked:
            qc = (qi * tq + _iota((tq, tk), 0)) >> 6
            kc = (ki * tk + _iota((tq, tk), 1)) >> 6
            keep = kc <= qc
        for hh in range(2):
            hs = slice(_MLA_HW * hh, _MLA_HW * (hh + 1))
            sc = lax.dot_general(q_ref[0, :, hs], k_ref[0, :, hs], (((1,), (1,)), ((), ())),
                                 preferred_element_type=F32)
            if masked:
                sc = jnp.where(keep, sc, NEG)
            m_prev = m_ref[hh]
            m_new = jnp.maximum(m_prev, jnp.max(sc, axis=-1, keepdims=True))
            p = jnp.exp2(sc - widen(m_new))
            alpha = jnp.exp2(m_prev - m_new)
            l_ref[hh] = alpha * l_ref[hh] + jnp.sum(p, axis=-1, keepdims=True)
            acc_ref[hh] = alpha * acc_ref[hh] + jnp.dot(_bf(p), kv_ref[0, :, hs],
                                                        preferred_element_type=F32)
            m_ref[hh] = m_new

    if causal:
        pl.when(ki < qi)(lambda: step(False))
        pl.when(ki == qi)(lambda: step(True))
    else:
        step(False)

    @pl.when(ki == last)
    def _():
        lane = _iota((tq, LANES), 1)
        o_e = pltpu.roll(acc_ref[0] / l_ref[0], MLA_NOPE, 1)
        o_o = acc_ref[1] / l_ref[1]
        o_ref[0] = jnp.where(lane < MLA_NOPE, o_e, o_o)


def _attn_call(q, k, kv, causal):
    B, Lq, _ = q.shape
    Lk = k.shape[1]
    if causal:
        tq = tk = min(Lq, 512)
    else:
        tq, tk = Lq, Lk
    nq, nk = Lq // tq, Lk // tk
    w2 = 2 * _MLA_HW
    kmap = (lambda b, hp, qi, ki: (b, jnp.minimum(ki, qi), hp)) if causal else (lambda b, hp, qi, ki: (b, ki, hp))
    return pl.pallas_call(
        functools.partial(_attn_kernel, tq=tq, tk=tk, nk=nk, causal=causal),
        out_shape=jax.ShapeDtypeStruct((B, Lq, MLA_HEADS * MLA_NOPE), F32),
        grid=(B, MLA_HEADS // 2, nq, nk),
        in_specs=[pl.BlockSpec((1, tq, w2), lambda b, hp, qi, ki: (b, qi, hp)),
                  pl.BlockSpec((1, tk, w2), kmap),
                  pl.BlockSpec((1, tk, w2), kmap)],
        out_specs=pl.BlockSpec((1, tq, LANES), lambda b, hp, qi, ki: (b, qi, hp)),
        scratch_shapes=[pltpu.VMEM((2, tq, LANES), F32), pltpu.VMEM((2, tq, LANES), F32),
                        pltpu.VMEM((2, tq, LANES), F32)],
        compiler_params=_cparams(("arbitrary",) * 4, VMEM_LIMIT),
        name="mla_attention",
    )(q, k, kv)


def _outproj_kernel(a_ref, b_ref, c_ref, x_ref, g1_ref, sh_ref, sc_ref, ng_ref, wa_ref, wb_ref, wc_ref,
                    x1_ref, h2_ref):
    y = (jnp.dot(_bf(a_ref[0]), wa_ref[...], preferred_element_type=F32)
         + jnp.dot(_bf(b_ref[0]), wb_ref[...], preferred_element_type=F32)
         + jnp.dot(_bf(c_ref[0]), wc_ref[...], preferred_element_type=F32))
    x1 = x_ref[0] + g1_ref[0] * y
    x1_ref[0] = x1
    n = x1 * lax.rsqrt(jnp.mean(x1 * x1, axis=-1, keepdims=True) + EPS) * ng_ref[...]
    h2_ref[0] = n * (1.0 + sc_ref[0]) + sh_ref[0]


def _outproj_call(oa, ob, oc, x, g1, sh2, sc2, ng, wa, wb, wc):
    B, L, _ = x.shape
    tb = min(L, 512)
    full = lambda shape: pl.BlockSpec(shape, lambda b, i: (0,) * len(shape))
    per_b = pl.BlockSpec((1, 1, D_MODEL), lambda b, i: (b, 0, 0))
    blk = lambda wdt: pl.BlockSpec((1, tb, wdt), lambda b, i: (b, i, 0))
    return pl.pallas_call(
        _outproj_kernel,
        out_shape=(jax.ShapeDtypeStruct((B, L, D_MODEL), F32), jax.ShapeDtypeStruct((B, L, D_MODEL), F32)),
        grid=(B, L // tb),
        in_specs=[blk(GDN_WIDTH), blk(MLA_HEADS * MLA_NOPE), blk(MLSTM_WIDTH), blk(D_MODEL),
                  per_b, per_b, per_b, full((1, D_MODEL)), full(wa.shape), full(wb.shape), full(wc.shape)],
        out_specs=(blk(D_MODEL), blk(D_MODEL)),
        compiler_params=_cparams(("arbitrary", "arbitrary"), VMEM_LIMIT),
        name="out_proj",
    )(oa, ob, oc, x, g1, sh2, sc2, ng, wa, wb, wc)


def _topk_rows(sv, io, k, payload=None):
    vals, outs = [], []
    big = jnp.float32(1e9)
    for _ in range(k):
        m = jnp.max(sv, axis=0, keepdims=True)
        ix = jnp.min(jnp.where(sv == m, io, big), axis=0, keepdims=True)
        hit = io == ix
        sv = jnp.where(hit, -jnp.inf, sv)
        vals.append(m)
        outs.append(ix if payload is None else jnp.max(jnp.where(hit, payload, -1.0), axis=0, keepdims=True))
    return jnp.concatenate(vals, axis=0), jnp.concatenate(outs, axis=0)


_CAND_ROWS = PEER_TOPK + 7 * 8 + 8


def _cand_blocks(t1, t2, combine):
    blocks = [combine(t1[0:1], t2)]
    blocks += [combine(t1[a:a + 1], t2[0:8]) for a in range(1, 8)]
    blocks.append(combine(t1[8:PEER_TOPK], t2[0:1]))
    return jnp.concatenate(blocks, axis=0)


def _route_kernel(h_ref, wq_ref, keys_ref, e_ref, g_ref, qs_ref, es_ref, gs_ref, *, tb):
    q = jnp.dot(_bf(h_ref[...]), wq_ref[...], preferred_element_type=F32)
    for g in range(2 * PEER_HEADS):
        qs_ref[g] = _bf(q[:, LANES * g:LANES * (g + 1)])
    io_k = _iota((N_KEYS, tb), 0).astype(F32)
    r = _iota((_CAND_ROWS, tb), 0)
    mid = r - PEER_TOPK
    io_c = jnp.where(r < PEER_TOPK, r,
                     jnp.where(r < _CAND_ROWS - 8, ((mid >> 3) + 1) * PEER_TOPK + (mid & 7),
                               (r - (_CAND_ROWS - 16)) * PEER_TOPK)).astype(F32)

    def head(h, carry):
        tops = []
        for p in range(2):
            sc = lax.dot_general(keys_ref[2 * h + p], qs_ref[2 * h + p], (((1,), (1,)), ((), ())),
                                 preferred_element_type=F32)
            tops.append(_topk_rows(sc, io_k, PEER_TOPK))
        (v1, i1), (v2, i2) = tops
        cand = _cand_blocks(v1, v2, lambda x, y: x + y)
        expert = _cand_blocks(i1, i2, lambda x, y: x * N_KEYS + y)
        sc, e = _topk_rows(cand, io_c, PEER_TOPK, payload=expert)
        ex = jnp.exp(sc - jnp.max(sc, axis=0, keepdims=True))
        es_ref[h] = e
        gs_ref[h] = ex / jnp.sum(ex, axis=0, keepdims=True)
        return carry

    lax.fori_loop(0, PEER_HEADS, head, 0)
    e_ref[...] = es_ref[...].reshape(PEER_SLOTS, tb).T.astype(jnp.int32)
    g_ref[...] = gs_ref[...].reshape(PEER_SLOTS, tb).T


def _route_call(h2, wq, keys):
    T = h2.shape[0]
    tb = min(T, 256)
    full = lambda shape: pl.BlockSpec(shape, lambda i: (0,) * len(shape))
    return pl.pallas_call(
        functools.partial(_route_kernel, tb=tb),
        out_shape=(jax.ShapeDtypeStruct((T, PEER_SLOTS), jnp.int32), jax.ShapeDtypeStruct((T, PEER_SLOTS), F32)),
        grid=(T // tb,),
        in_specs=[pl.BlockSpec((tb, D_MODEL), lambda i: (i, 0)), full(wq.shape), full(keys.shape)],
        out_specs=(pl.BlockSpec((tb, PEER_SLOTS), lambda i: (i, 0)), pl.BlockSpec((tb, PEER_SLOTS), lambda i: (i, 0))),
        scratch_shapes=[pltpu.VMEM((2 * PEER_HEADS, tb, LANES), BF16),
                        pltpu.VMEM((PEER_HEADS, PEER_TOPK, tb), F32),
                        pltpu.VMEM((PEER_HEADS, PEER_TOPK, tb), F32)],
        compiler_params=_cparams(("arbitrary",), VMEM_LIMIT),
        name="peer_route",
    )(h2, wq, keys)


_ROW_SUB = 4
_GROUP = 8
_BITREV3 = (0, 4, 2, 6, 1, 5, 3, 7)


def _row_f32(w):
    return pltpu.bitcast(w, BF16).astype(F32)


def _gelu(x):
    return 0.5 * x * (1.0 + lax.erf(x * (2.0 ** -0.5)))


def _fold(a, b, h, sub):
    m = (sub & h) == 0
    if h == 4:
        return jnp.where(m, a, b) + pltpu.roll(jnp.where(m, b, a), 4, 0)
    return jnp.where(m, a + pltpu.roll(a, 8 - h, 0), b + pltpu.roll(b, h, 0))


def _peer_u_kernel(idx_ref, x_ref, g_ref, tab_ref, c_ref, part_ref, *, tb):
    ones8 = jnp.ones((8, LANES), BF16)
    sub = _iota((8, LANES), 0)
    nt = lambda a, b: lax.dot_general(a, b, (((1,), (1,)), ((), ())), preferred_element_type=F32)

    def gather(t):
        xv = x_ref[t]
        buf = part_ref.at[t & 1]
        for g in range(PEER_SLOTS // _GROUP):
            p = [_row_f32(tab_ref[idx_ref[t, _GROUP * g + _BITREV3[k]]]) * xv for k in range(_GROUP)]
            z = [_fold(p[2 * i], p[2 * i + 1], 4, sub) for i in range(4)]
            w = [_fold(z[0], z[1], 2, sub), _fold(z[2], z[3], 2, sub)]
            buf[_GROUP * g:_GROUP * (g + 1), :] = _fold(w[0], w[1], 1, sub)

    def finish(t):
        part = part_ref[t & 1]
        p_hi = _bf(part)
        p_lo = _bf(part - p_hi.astype(F32))
        act = (nt(ones8, p_hi) + nt(ones8, p_lo))[0:1]
        c_ref[pl.ds(t, 1), :] = g_ref[pl.ds(t, 1), :] * _gelu(act)

    gather(0)

    def tok(t, carry):
        finish(t - 1)
        gather(t)
        return carry

    lax.fori_loop(1, tb, tok, 0)
    finish(tb - 1)


def _peer_v_kernel(idx_ref, c_ref, x_ref, g2_ref, tab_ref, o_ref, m_ref, *, tb):
    eye = _iota((PEER_SLOTS, LANES), 0) == _iota((PEER_SLOTS, LANES), 1)
    diag = jnp.where(eye[None], c_ref[...][:, None, :], 0.0).reshape(tb * PEER_SLOTS, LANES)
    rep = jnp.dot(_bf(diag), jnp.ones((LANES, LANES), BF16), preferred_element_type=F32)
    m_ref[...] = rep.reshape(tb, PEER_SLOTS, LANES)

    def tok(t, carry):
        y = jnp.zeros((8, LANES), F32)
        for g in range(PEER_SLOTS // _GROUP):
            r = [m_ref[t, pl.ds(_GROUP * g + k, 1), :] * _row_f32(tab_ref[idx_ref[t, _GROUP * g + k]])
                 for k in range(_GROUP)]
            y = y + (((r[0] + r[1]) + (r[2] + r[3])) + ((r[4] + r[5]) + (r[6] + r[7])))
        o_ref[t] = x_ref[t] + g2_ref[0] * y
        return carry

    lax.fori_loop(0, tb, tok, 0)


def _table_spec():
    return pl.BlockSpec((N_EXPERTS, _ROW_SUB, LANES), lambda i: (0, 0, 0), pipeline_mode=pl.Buffered(1))


def _peer_u_call(idx, x3, gate, tab, tb):
    T = idx.shape[0]
    smem = pl.BlockSpec((tb, PEER_SLOTS), lambda i: (i, 0), memory_space=pltpu.SMEM)
    return pl.pallas_call(
        functools.partial(_peer_u_kernel, tb=tb),
        out_shape=jax.ShapeDtypeStruct((T, PEER_SLOTS), F32),
        grid=(T // tb,),
        in_specs=[smem, pl.BlockSpec((tb, 8, LANES), lambda i: (i, 0, 0)),
                  pl.BlockSpec((tb, PEER_SLOTS), lambda i: (i, 0)), _table_spec()],
        out_specs=pl.BlockSpec((tb, PEER_SLOTS), lambda i: (i, 0)),
        scratch_shapes=[pltpu.VMEM((2, PEER_SLOTS, LANES), F32)],
        compiler_params=_cparams(("arbitrary",), VMEM_LIMIT),
        name="peer_u",
    )(idx, x3, gate, tab)


def _peer_v_call(idx, coef, x3, g2, tab, tb, L):
    T = idx.shape[0]
    smem = pl.BlockSpec((tb, PEER_SLOTS), lambda i: (i, 0), memory_space=pltpu.SMEM)
    per_step = L // tb
    return pl.pallas_call(
        functools.partial(_peer_v_kernel, tb=tb),
        out_shape=jax.ShapeDtypeStruct((T, 8, LANES), F32),
        grid=(T // tb,),
        in_specs=[smem, pl.BlockSpec((tb, PEER_SLOTS), lambda i: (i, 0)),
                  pl.BlockSpec((tb, 8, LANES), lambda i: (i, 0, 0)),
                  pl.BlockSpec((1, 8, LANES), lambda i: (i // per_step, 0, 0)), _table_spec()],
        out_specs=pl.BlockSpec((tb, 8, LANES), lambda i: (i, 0, 0)),
        scratch_shapes=[pltpu.VMEM((tb, PEER_SLOTS, LANES), F32)],
        compiler_params=_cparams(("arbitrary",), VMEM_LIMIT),
        name="peer_v",
    )(idx, coef, x3, g2, tab)


def _pack_table(tab):
    bits = lax.bitcast_convert_type(tab.astype(BF16), jnp.uint16).astype(jnp.uint32)
    bits = bits.reshape(tab.shape[0], _ROW_SUB, 2, LANES)
    return bits[:, :, 0, :] | (bits[:, :, 1, :] << 16)


def _rep_heads(w, width=HEAD_W):
    return jnp.repeat(w, width, axis=-1)


def _to_bd(s):
    B, H = s.shape[:2]
    s = s.reshape(B, H // 2, 2, HEAD_W, HEAD_W)
    z = jnp.zeros_like(s[:, :, 0])
    top = jnp.concatenate([s[:, :, 0], z], axis=-1)
    bot = jnp.concatenate([z, s[:, :, 1]], axis=-1)
    return jnp.concatenate([top, bot], axis=-2)


def _from_bd(s):
    B, P = s.shape[:2]
    return jnp.stack([s[:, :, :HEAD_W, :HEAD_W], s[:, :, HEAD_W:, HEAD_W:]], axis=2).reshape(B, 2 * P, HEAD_W, HEAD_W)


def _prep_layer(l, w):
    o = [0]
    for sz in (GDN_QKV, GDN_WIDTH, GDN_HEADS, GDN_HEADS, MLA_Q_LORA, MLA_KV_LORA, MLA_ROPE,
               MLSTM_QKV, MLSTM_WIDTH, MLSTM_HEADS, MLSTM_HEADS):
        o.append(o[-1] + sz)
    wi = w['w_in'][l]
    col = lambda i: wi[:, o[i]:o[i + 1]]
    p = {}
    p['w_gdn'] = _bf(jnp.concatenate([col(0), col(1), _rep_heads(col(2)), _rep_heads(col(3))], axis=1))
    p['w_mla'] = _bf(jnp.concatenate([col(4), col(5), col(6), jnp.zeros((D_MODEL, LANES - MLA_ROPE), F32)], axis=1))
    p['w_mls'] = _bf(jnp.concatenate([col(7), col(8), _rep_heads(col(9)), _rep_heads(col(10))], axis=1))
    p['alog'] = _rep_heads(w['gdn_a_log'][l]).reshape(1, GDN_WIDTH)
    p['dtb'] = _rep_heads(w['gdn_dt_bias'][l]).reshape(1, GDN_WIDTH)
    p['gdn_og'] = jnp.tile(w['gdn_out_g'][l], 2).reshape(1, LANES)
    p['ib'] = _rep_heads(w['mlstm_i_bias'][l]).reshape(1, MLSTM_WIDTH)
    p['fb'] = _rep_heads(w['mlstm_f_bias'][l]).reshape(1, MLSTM_WIDTH)
    p['mls_og'] = w['mlstm_out_g'][l].reshape(1, MLSTM_WIDTH)
    wuq = w['mla_w_uq'][l].reshape(MLA_Q_LORA, MLA_HEADS, MLA_QK)
    p['wuq'] = _bf(jnp.pad(wuq, ((0, 0), (0, 0), (0, _MLA_HW - MLA_QK))).reshape(MLA_Q_LORA, MLA_HEADS * _MLA_HW))
    p['wukv'] = _bf(w['mla_w_ukv'][l])
    p['qag'] = w['mla_q_a_g'][l].reshape(1, MLA_Q_LORA)
    p['kvag'] = w['mla_kv_a_g'][l].reshape(1, MLA_KV_LORA)
    p['qgain'] = jnp.pad(w['mla_q_gain'][l], (0, LANES - MLA_QK)).reshape(1, LANES)
    kg = w['mla_k_gain'][l]
    p['kgain'] = jnp.pad(kg[:MLA_NOPE], (0, LANES - MLA_NOPE)).reshape(1, LANES)
    p['krgain'] = jnp.pad(kg[MLA_NOPE:], (0, LANES - MLA_ROPE)).reshape(1, LANES)
    wo = w['w_out'][l]
    p['wo_a'] = _bf(wo[0:GDN_WIDTH])
    p['wo_b'] = _bf(wo[GDN_WIDTH:2 * GDN_WIDTH])
    p['wo_c'] = _bf(wo[2 * GDN_WIDTH:])
    p['wq'] = _bf(w['peer_w_q'][l])
    p['keys'] = _bf(w['peer_sub_keys'][l].reshape(2 * PEER_HEADS, N_KEYS, LANES))
    p['u_tab'] = _pack_table(w['peer_u'][l])
    p['v_tab'] = _pack_table(w['peer_v'][l])
    p['conv_w'] = w['gdn_conv_w'][l]
    p['norm_attn_g'] = w['norm_attn_g'][l].reshape(1, D_MODEL)
    p['norm_ffn_g'] = w['norm_ffn_g'][l].reshape(1, D_MODEL)
    return p


def _rope_tables(pos):
    half = MLA_ROPE // 2
    inv = ROPE_THETA ** (-jnp.arange(half, dtype=F32) / half)
    ang = pos.astype(F32)[:, None] * inv[None, :]
    cos, sin = jnp.cos(ang), jnp.sin(ang)
    n = pos.shape[0]
    c2 = jnp.concatenate([cos, cos], axis=1)
    s2 = jnp.concatenate([-sin, sin], axis=1)
    padq = lambda t: jnp.pad(t, ((0, 0), (MLA_NOPE, LANES - MLA_QK)))
    padk = lambda t: jnp.pad(t, ((0, 0), (0, LANES - MLA_ROPE)))
    return padq(c2), padq(s2), padk(c2), padk(s2)


def _pad_rows(a, n):
    return jnp.pad(a, ((0, 0), (0, n - a.shape[1]), (0, 0)))


def _layer(x, mod, p, st, rope, prompt):
    B, L, _ = x.shape
    sh1, sc1, g1, sh2, sc2, g2 = [m.reshape(B, 1, D_MODEL) for m in jnp.split(mod, 6, axis=-1)]
    pg, pm, pl_ = _inproj_call(x, sh1, sc1, p['norm_attn_g'], p['w_gdn'], p['w_mla'], p['w_mls'])
    Lp = -(-L // CHUNK) * CHUNK
    if Lp != L:
        pg, pl_ = _pad_rows(pg, Lp), _pad_rows(pl_, Lp)
    o_a, gdn_bd, conv_new = _gdn_call(pg, st['gdn_conv'], _to_bd(st['gdn']), p['conv_w'],
                                      p['alog'], p['dtb'], p['gdn_og'], L)
    o_c, mc_bd, mn_p, mm_p = _mlstm_call(
        pl_, _to_bd(st['mlstm_c']), st['mlstm_n'].reshape(B, MLSTM_HEADS // 2, 1, LANES),
        _rep_heads(st['mlstm_m']).reshape(B, MLSTM_HEADS // 2, 1, LANES), p['ib'], p['fb'], p['mls_og'], L)
    o_a, o_c = o_a[:, :L], o_c[:, :L]
    q, ckv, kr = _mla_q_call(pm, p['qag'], p['wuq'], p['qgain'], p['kvag'], p['krgain'], *rope)
    krp = lambda t: jnp.pad(t, ((0, 0), (0, 0), (MLA_NOPE, LANES - MLA_QK)))
    k, kv = _mla_kv_call(ckv, krp(kr), p['wukv'], p['kgain'])
    if not prompt:
        k_c, kv_c = _mla_kv_call(st['mla_latent'], krp(st['mla_krope']), p['wukv'], p['kgain'])
        k = jnp.concatenate([k_c, k], axis=1)
        kv = jnp.concatenate([kv_c, kv], axis=1)
    o_b = _attn_call(q, k, kv, prompt)
    x1, h2 = _outproj_call(o_a, o_b, o_c, x, g1, sh2, sc2, p['norm_ffn_g'], p['wo_a'], p['wo_b'], p['wo_c'])
    T = B * L
    h2f = h2.reshape(T, D_MODEL)
    idx, gate = _route_call(h2f, p['wq'], p['keys'])
    tb = min(L, 64)
    coef = _peer_u_call(idx, h2f.reshape(T, 8, LANES), gate, p['u_tab'], tb)
    x2 = _peer_v_call(idx, coef, x1.reshape(T, 8, LANES), g2.reshape(B, 8, LANES), p['v_tab'], tb, L)
    new_state = (ckv, kr, _from_bd(gdn_bd), conv_new, _from_bd(mc_bd),
                 mn_p.reshape(B, MLSTM_HEADS, HEAD_W), mm_p.reshape(B, MLSTM_HEADS, HEAD_W)[:, :, 0])
    return x2.reshape(B, L, D_MODEL), new_state


def kernel(x_prompt, x_sample, c_prompt, c_sample, cache_mla_latent, cache_mla_krope, state_gdn, state_gdn_conv, state_mlstm_c, state_mlstm_n, state_mlstm_m, ada_w, ada_b, norm_attn_g, norm_ffn_g, w_in, gdn_conv_w, gdn_a_log, gdn_dt_bias, gdn_out_g, mla_q_a_g, mla_w_uq, mla_kv_a_g, mla_w_ukv, mla_q_gain, mla_k_gain, mlstm_i_bias, mlstm_f_bias, mlstm_out_g, w_out, peer_w_q, peer_sub_keys, peer_u, peer_v):
    w = dict(ada_w=ada_w, ada_b=ada_b, norm_attn_g=norm_attn_g, norm_ffn_g=norm_ffn_g, w_in=w_in,
             gdn_conv_w=gdn_conv_w, gdn_a_log=gdn_a_log, gdn_dt_bias=gdn_dt_bias, gdn_out_g=gdn_out_g,
             mla_q_a_g=mla_q_a_g, mla_w_uq=mla_w_uq, mla_kv_a_g=mla_kv_a_g, mla_w_ukv=mla_w_ukv,
             mla_q_gain=mla_q_gain, mla_k_gain=mla_k_gain, mlstm_i_bias=mlstm_i_bias,
             mlstm_f_bias=mlstm_f_bias, mlstm_out_g=mlstm_out_g, w_out=w_out, peer_w_q=peer_w_q,
             peer_sub_keys=peer_sub_keys, peer_u=peer_u, peer_v=peer_v)
    B, Lp, _ = x_prompt.shape
    Bs, Ls, _ = x_sample.shape
    past = cache_mla_latent.shape[2]
    rope_p = _rope_tables(jnp.arange(Lp, dtype=jnp.int32))
    rope_s = _rope_tables(past + jnp.arange(Ls, dtype=jnp.int32))
    mods = _mod_call(jnp.concatenate([c_prompt, c_sample], axis=0), ada_w, ada_b)
    xp, xs = x_prompt, x_sample
    new_p, new_s = [], []
    for l in range(DEPTH):
        p = _prep_layer(l, w)
        st_p = {
            'gdn': jnp.zeros((B, GDN_HEADS, HEAD_W, HEAD_W), F32),
            'gdn_conv': jnp.zeros((B, CONV_W - 1, GDN_QKV), F32),
            'mlstm_c': jnp.zeros((B, MLSTM_HEADS, HEAD_W, HEAD_W), F32),
            'mlstm_n': jnp.zeros((B, MLSTM_HEADS, HEAD_W), F32),
            'mlstm_m': jnp.zeros((B, MLSTM_HEADS), F32),
        }
        st_s = {
            'mla_latent': cache_mla_latent[l], 'mla_krope': cache_mla_krope[l],
            'gdn': state_gdn[l], 'gdn_conv': state_gdn_conv[l],
            'mlstm_c': state_mlstm_c[l], 'mlstm_n': state_mlstm_n[l], 'mlstm_m': state_mlstm_m[l],
        }
        xp, sp = _layer(xp, mods[l, :B], p, st_p, rope_p, True)
        xs, ss = _layer(xs, mods[l, B:], p, st_s, rope_s, False)
        new_p.append(sp)
        new_s.append(ss)
    outs_p = [jnp.stack([s[i] for s in new_p]) for i in range(7)]
    outs_s = [jnp.stack([s[i] for s in new_s]) for i in range(7)]
    return (xp, xs, *outs_p, *outs_s)
```

```python
import functools
import math

import jax
import jax.numpy as jnp
from jax import lax
from jax.experimental import pallas as pl
from jax.experimental.pallas import tpu as pltpu

F32 = jnp.float32
BF16 = jnp.bfloat16

D_MODEL = 1024
DEPTH = 2
CHUNK = 64
EPS = 1e-6
GDN_HEADS = 6
GDN_DK = 64
GDN_WIDTH = 384
GDN_QKV = 1152
CONV_W = 4
MLA_HEADS = 6
MLA_NOPE = 64
MLA_ROPE = 32
MLA_QK = 96
MLA_Q_LORA = 384
MLA_KV_LORA = 256
ROPE_THETA = 10000.0
MLSTM_HEADS = 4
MLSTM_WIDTH = 256
MLSTM_QKV = 768
PEER_HEADS = 8
N_KEYS = 128
N_EXPERTS = N_KEYS * N_KEYS
PEER_TOPK = 16
PEER_SLOTS = PEER_HEADS * PEER_TOPK

HEAD_W = 64
LANES = 128
NEG = -1e30
VMEM_LIMIT = 56 * 1024 * 1024


def _bf(x):
    return x.astype(BF16)


def _mm(a, b):
    return jnp.dot(_bf(a), _bf(b), preferred_element_type=F32)


def _mm_nt(a, b):
    return lax.dot_general(_bf(a), _bf(b), (((1,), (1,)), ((), ())), preferred_element_type=F32)


def _split3(x):
    hi = _bf(x)
    r = x - hi.astype(F32)
    mid = _bf(r)
    lo = _bf(r - mid.astype(F32))
    return hi, mid, lo


def _mm_sel_exact(sel_bf, x):
    hi, mid, lo = _split3(x)
    d = lambda t: jnp.dot(sel_bf, t, preferred_element_type=F32)
    return d(hi) + d(mid) + d(lo)


def _sigmoid(x):
    return 1.0 / (1.0 + jnp.exp(-x))


def _silu(x):
    return x * _sigmoid(x)


def _softplus(x):
    return jnp.maximum(x, 0.0) + jnp.log1p(jnp.exp(-jnp.abs(x)))


def _iota(shape, axis):
    return lax.broadcasted_iota(jnp.int32, shape, axis)


def _pair_consts(C):
    row = _iota((C, LANES), 0)
    lane = _iota((C, LANES), 1)
    s = lane & (HEAD_W - 1)
    lo_half = lane < HEAD_W
    r2 = _iota((LANES, LANES), 0)
    l2 = _iota((LANES, LANES), 1)
    same_half = (r2 >> 6) == (l2 >> 6)
    return row, lane, s, lo_half, same_half


def _bd(y, lo_half):
    z = jnp.zeros_like(y)
    return jnp.concatenate([jnp.where(lo_half, y, z), jnp.where(lo_half, z, y)], axis=0)


def _diag_row(x, row, s):
    return jnp.sum(jnp.where(row == s, x, 0.0), axis=0, keepdims=True)


def _cparams(sem, vmem=None):
    kw = dict(dimension_semantics=sem)
    if vmem is not None:
        kw["vmem_limit_bytes"] = vmem
    return pltpu.CompilerParams(**kw)


def _mod_kernel(c_ref, w_ref, b_ref, o_ref):
    c = c_ref[...]
    o_ref[0] = _mm(_silu(c), w_ref[0]) + b_ref[0]


def _mod_call(c_all, ada_w, ada_b):
    nb = c_all.shape[0]
    nj = 6
    return pl.pallas_call(
        _mod_kernel,
        out_shape=jax.ShapeDtypeStruct((DEPTH, nb, 6 * D_MODEL), F32),
        grid=(DEPTH, nj),
        in_specs=[
            pl.BlockSpec((nb, D_MODEL), lambda l, j: (0, 0)),
            pl.BlockSpec((1, D_MODEL, D_MODEL), lambda l, j: (l, 0, j)),
            pl.BlockSpec((1, 1, D_MODEL), lambda l, j: (l, 0, j)),
        ],
        out_specs=pl.BlockSpec((1, nb, D_MODEL), lambda l, j: (l, 0, j)),
        compiler_params=_cparams(("arbitrary", "arbitrary")),
        name="adaln_mod",
    )(c_all, ada_w, ada_b.reshape(DEPTH, 1, 6 * D_MODEL))


def _inproj_kernel(x_ref, sh_ref, sc_ref, g_ref, wg_ref, wm_ref, wl_ref, og_ref, om_ref, ol_ref):
    x = x_ref[0]
    y = x * lax.rsqrt(jnp.mean(x * x, axis=-1, keepdims=True) + EPS) * g_ref[...]
    h = _bf(y * (1.0 + sc_ref[0]) + sh_ref[0])
    og_ref[0] = jnp.dot(h, wg_ref[...], preferred_element_type=F32)
    om_ref[0] = jnp.dot(h, wm_ref[...], preferred_element_type=F32)
    ol_ref[0] = jnp.dot(h, wl_ref[...], preferred_element_type=F32)


def _inproj_call(x, sh, sc, g, wg, wm, wl):
    B, L, _ = x.shape
    tb = min(L, 512)
    ng, nm, nl = wg.shape[1], wm.shape[1], wl.shape[1]
    full = lambda shape: pl.BlockSpec(shape, lambda b, i: (0,) * len(shape))
    per_b = pl.BlockSpec((1, 1, D_MODEL), lambda b, i: (b, 0, 0))
    return pl.pallas_call(
        _inproj_kernel,
        out_shape=(jax.ShapeDtypeStruct((B, L, ng), F32),
                   jax.ShapeDtypeStruct((B, L, nm), F32),
                   jax.ShapeDtypeStruct((B, L, nl), F32)),
        grid=(B, L // tb),
        in_specs=[pl.BlockSpec((1, tb, D_MODEL), lambda b, i: (b, i, 0)), per_b, per_b,
                  full((1, D_MODEL)), full(wg.shape), full(wm.shape), full(wl.shape)],
        out_specs=(pl.BlockSpec((1, tb, ng), lambda b, i: (b, i, 0)),
                   pl.BlockSpec((1, tb, nm), lambda b, i: (b, i, 0)),
                   pl.BlockSpec((1, tb, nl), lambda b, i: (b, i, 0))),
        compiler_params=_cparams(("arbitrary", "arbitrary"), VMEM_LIMIT),
        name="in_proj",
    )(x, sh, sc, g, wg, wm, wl)


_GDN_Z0 = GDN_QKV
_GDN_B0 = GDN_QKV + GDN_WIDTH
_GDN_A0 = _GDN_B0 + GDN_WIDTH
_GDN_COLS = _GDN_A0 + GDN_WIDTH
_CONV_PAD = 8


def _gdn_kernel(pg_ref, conv0_ref, s0_ref, convw_ref, alog_ref, dtb_ref, og_ref,
                o_ref, sfin_ref, convfin_ref, xp_ref, s_ref, *, C, cps, n_valid, n_steps):
    n = pl.program_id(1)
    hist = CONV_W - 1
    R = C * cps

    @pl.when(n == 0)
    def _():
        xp_ref[0:_CONV_PAD, :] = jnp.zeros((_CONV_PAD, GDN_QKV), F32)
        xp_ref[_CONV_PAD - hist:_CONV_PAD, :] = conv0_ref[0]
        s_ref[...] = s0_ref[0]

    a_pre = pg_ref[0, :, 0:GDN_QKV]
    xp_ref[_CONV_PAD:_CONV_PAD + R, :] = a_pre
    w = convw_ref[...]
    y = w[CONV_W - 1:CONV_W] * a_pre
    for j in range(CONV_W - 1):
        y = y + w[j:j + 1] * xp_ref[_CONV_PAD - hist + j:_CONV_PAD - hist + j + R, :]
    y = _silu(y)

    row, lane, s, lo_half, same_half = _pair_consts(C)
    ones_blk = jnp.where(same_half, 1.0, 0.0).astype(BF16)
    tr, tc = _iota((R, R), 0), _iota((R, R), 1)
    tri = jnp.where((tc <= tr) & ((tc >> 6) == (tr >> 6)), 1.0, 0.0).astype(BF16)
    causal = s <= row
    strict = s < row
    eye = s == row

    def hsum(x):
        return jnp.dot(_bf(x), ones_blk, preferred_element_type=F32)

    beta_all = _sigmoid(pg_ref[0, :, _GDN_B0:_GDN_A0])
    gl_all = -jnp.exp(alog_ref[...]) * _softplus(pg_ref[0, :, _GDN_A0:_GDN_COLS] + dtb_ref[...])
    if n_valid < R * n_steps:
        valid = (_iota((R, GDN_WIDTH), 0) + n * R) < n_valid
        beta_all = jnp.where(valid, beta_all, 0.0)
        gl_all = jnp.where(valid, gl_all, 0.0)
    g_all = _mm_sel_exact(tri, gl_all)

    items = [(c, j) for c in range(cps) for j in range(GDN_HEADS // 2)]
    I = range(len(items))
    blk = lambda a, base, i: a[C * items[i][0]:C * (items[i][0] + 1),
                               base + LANES * items[i][1]:base + LANES * (items[i][1] + 1)]
    bd = lambda a: _bd(a, lo_half)
    qk_raw = [jnp.concatenate([blk(y, 0, i), blk(y, GDN_WIDTH, i)], axis=0) for i in I]
    ssq = [hsum(t * t) for t in qk_raw]
    q2 = [qk_raw[i][0:C] * lax.rsqrt(ssq[i][0:C] + EPS) * (GDN_DK ** -0.5) for i in I]
    k2 = [qk_raw[i][C:2 * C] * lax.rsqrt(ssq[i][C:2 * C] + EPS) for i in I]
    v2 = [blk(y, 2 * GDN_WIDTH, i) for i in I]
    beta = [blk(beta_all, 0, i) for i in I]
    G = [blk(g_all, 0, i) for i in I]
    eG = [jnp.exp(G[i]) for i in I]
    decay = []
    for i in I:
        diff = G[i] - _diag_row(G[i], row, s)
        decay.append(jnp.where(causal, jnp.exp(jnp.where(causal, diff, 0.0)), 0.0))
    kbd = [bd(k2[i]) for i in I]
    kq = [_mm_nt(jnp.concatenate([k2[i], q2[i]], axis=0), kbd[i]) for i in I]
    A = [jnp.where(strict, beta[i] * kq[i][0:C] * decay[i], 0.0) for i in I]
    qk = [kq[i][C:2 * C] * decay[i] for i in I]
    first = ((row & 1) == 1) & (s == row - 1)
    T = [jnp.where(eye, 1.0, 0.0) - jnp.where(first, A[i], 0.0) for i in I]
    for lg in range(1, 6):
        sub = (((row >> lg) & 1) == 1) & ((s >> lg) == (row >> lg) - 1)
        t1 = [_mm(T[i], bd(jnp.where(sub, A[i], 0.0))) for i in I]
        t2 = [_mm(t1[i], bd(T[i])) for i in I]
        T = [T[i] - t2[i] for i in I]
    uw = [_mm(T[i], jnp.concatenate([bd(beta[i] * v2[i]), bd(beta[i] * eG[i] * k2[i])], axis=1)) for i in I]
    g_last = [G[i][C - 1:C, :] for i in I]
    kend_t = [(k2[i] * jnp.exp(g_last[i] - G[i])).T for i in I]
    P = range(GDN_HEADS // 2)
    S = [s_ref[j] for j in P]
    o = [None] * len(items)
    for c in range(cps):
        ids = [c * len(P) + j for j in P]
        ws = [_mm(jnp.concatenate([uw[i][:, LANES:2 * LANES], q2[i] * eG[i]], axis=0), S[j])
              for j, i in enumerate(ids)]
        wn = [uw[i][:, 0:LANES] - ws[j][0:C] for j, i in enumerate(ids)]
        upd = [_mm(kend_t[i], wn[j]) for j, i in enumerate(ids)]
        for j, i in enumerate(ids):
            o[i] = ws[j][C:2 * C] + _mm(qk[i], bd(wn[j]))
        S = [jnp.exp(g_last[i]) * S[j] + jnp.where(same_half, upd[j], 0.0) for j, i in enumerate(ids)]
    for j in P:
        s_ref[j] = S[j]
    oss = [hsum(o[i] * o[i]) for i in I]
    for i in I:
        c, j = items[i]
        on = o[i] * lax.rsqrt(oss[i] * (1.0 / HEAD_W) + EPS) * og_ref[...]
        z = pg_ref[0, C * c:C * (c + 1), _GDN_Z0 + LANES * j:_GDN_Z0 + LANES * (j + 1)]
        o_ref[0, C * c:C * (c + 1), LANES * j:LANES * (j + 1)] = on * _silu(z)

    @pl.when(n < n_steps - 1)
    def _():
        xp_ref[_CONV_PAD - hist:_CONV_PAD, :] = xp_ref[_CONV_PAD + R - hist:_CONV_PAD + R, :]

    @pl.when(n == n_steps - 1)
    def _():
        lv = n_valid - (n_steps - 1) * R
        convfin_ref[0] = xp_ref[_CONV_PAD + lv - hist:_CONV_PAD + lv, :]
        sfin_ref[0] = s_ref[...]


def _gdn_call(pg, conv0, s0_bd, conv_w, alog_rep, dtb_rep, og_rep, n_valid):
    B, Lp, _ = pg.shape
    C = CHUNK
    n_chunks = Lp // C
    cps = 4 if n_chunks % 4 == 0 else (2 if n_chunks % 2 == 0 else 1)
    n_steps = n_chunks // cps
    R = C * cps
    npair = GDN_HEADS // 2
    kern = functools.partial(_gdn_kernel, C=C, cps=cps, n_valid=n_valid, n_steps=n_steps)
    full = lambda shape: pl.BlockSpec(shape, lambda b, n: (0,) * len(shape))
    return pl.pallas_call(
        kern,
        out_shape=(jax.ShapeDtypeStruct((B, Lp, GDN_WIDTH), F32),
                   jax.ShapeDtypeStruct((B, npair, LANES, LANES), F32),
                   jax.ShapeDtypeStruct((B, CONV_W - 1, GDN_QKV), F32)),
        grid=(B, n_steps),
        in_specs=[pl.BlockSpec((1, R, _GDN_COLS), lambda b, n: (b, n, 0)),
                  pl.BlockSpec((1, CONV_W - 1, GDN_QKV), lambda b, n: (b, 0, 0)),
                  pl.BlockSpec((1, npair, LANES, LANES), lambda b, n: (b, 0, 0, 0)),
                  full((CONV_W, GDN_QKV)), full((1, GDN_WIDTH)), full((1, GDN_WIDTH)), full((1, LANES))],
        out_specs=(pl.BlockSpec((1, R, GDN_WIDTH), lambda b, n: (b, n, 0)),
                   pl.BlockSpec((1, npair, LANES, LANES), lambda b, n: (b, 0, 0, 0)),
                   pl.BlockSpec((1, CONV_W - 1, GDN_QKV), lambda b, n: (b, 0, 0))),
        scratch_shapes=[pltpu.VMEM((_CONV_PAD + R, GDN_QKV), F32),
                        pltpu.VMEM((npair, LANES, LANES), F32)],
        compiler_params=_cparams(("arbitrary", "arbitrary")),
        name="gdn_chunks",
    )(pg, conv0, s0_bd, conv_w, alog_rep, dtb_rep, og_rep)


_ML_O0 = MLSTM_QKV
_ML_I0 = _ML_O0 + MLSTM_WIDTH
_ML_F0 = _ML_I0 + MLSTM_WIDTH
_ML_COLS = _ML_F0 + MLSTM_WIDTH


def _mlstm_kernel(pm_ref, c0_ref, n0_ref, m0_ref, ib_ref, fb_ref, og_ref,
                  o_ref, cfin_ref, nfin_ref, mfin_ref, c_ref, n_ref, m_ref, *, C, n_valid, n_chunks):
    n = pl.program_id(1)

    @pl.when(n == 0)
    def _():
        c_ref[...] = c0_ref[0]
        n_ref[...] = n0_ref[0]
        m_ref[...] = m0_ref[0]

    row, lane, s, lo_half, same_half = _pair_consts(C)
    ones_blk = jnp.where(same_half, 1.0, 0.0).astype(BF16)
    tri = jnp.where(_iota((C, C), 1) <= _iota((C, C), 0), 1.0, 0.0).astype(BF16)
    causal = s <= row

    def hsum(x):
        return jnp.dot(_bf(x), ones_blk, preferred_element_type=F32)

    ig_all = pm_ref[0, :, _ML_I0:_ML_F0] + ib_ref[...]
    lf_all = -_softplus(-(pm_ref[0, :, _ML_F0:_ML_COLS] + fb_ref[...]))
    if n_valid < C * n_chunks:
        valid = (_iota((C, MLSTM_WIDTH), 0) + n * C) < n_valid
        ig_all = jnp.where(valid, ig_all, NEG)
        lf_all = jnp.where(valid, lf_all, 0.0)
    f_all = _mm_sel_exact(tri, lf_all)

    P = range(MLSTM_HEADS // 2)
    col = lambda base, j: pm_ref[0, :, base + LANES * j:base + LANES * (j + 1)]
    bd = lambda a: _bd(a, lo_half)
    q2 = [col(0, j) for j in P]
    k2 = [col(MLSTM_WIDTH, j) * (HEAD_W ** -0.5) for j in P]
    v2 = [col(2 * MLSTM_WIDTH, j) for j in P]
    ig = [ig_all[:, LANES * j:LANES * (j + 1)] for j in P]
    F = [f_all[:, LANES * j:LANES * (j + 1)] for j in P]
    dm, dmax = [], []
    for j in P:
        d = jnp.where(causal, F[j] - _diag_row(F[j], row, s) + _diag_row(ig[j], row, s), NEG)
        d_e = jnp.max(jnp.where(lo_half, d, NEG), axis=1, keepdims=True)
        d_o = jnp.max(jnp.where(lo_half, NEG, d), axis=1, keepdims=True)
        dm.append(d)
        dmax.append(jnp.where(lo_half, d_e, d_o))
    ms = [m_ref[j] for j in P]
    cs = [c_ref[j] for j in P]
    ns = [n_ref[j] for j in P]
    mt = [jnp.maximum(F[j] + ms[j], dmax[j]) for j in P]
    inter = [jnp.exp(F[j] + ms[j] - mt[j]) for j in P]
    qk = [_mm_nt(q2[j], bd(k2[j])) for j in P]
    qc = [_mm(q2[j], cs[j]) for j in P]
    qn = [hsum(q2[j] * ns[j]) for j in P]
    wgt = [jnp.exp(dm[j] - mt[j]) * qk[j] for j in P]
    wv = [_mm(wgt[j], bd(v2[j])) for j in P]
    wsum = [hsum(wgt[j]) for j in P]
    fe = [F[j][C - 1:C, :] for j in P]
    se = [fe[j] - F[j] + ig[j] for j in P]
    m_new = [jnp.maximum(fe[j] + ms[j], jnp.max(se[j], axis=0, keepdims=True)) for j in P]
    kw = [k2[j] * jnp.exp(se[j] - m_new[j]) for j in P]
    upd = [_mm(kw[j].T, v2[j]) for j in P]
    o = []
    for j in P:
        den = inter[j] * qn[j] + wsum[j]
        h = (inter[j] * qc[j] + wv[j]) / jnp.maximum(jnp.abs(den), jnp.exp(-mt[j]))
        o.append(_sigmoid(col(_ML_O0, j)) * h)
    oss = [hsum(o[j] * o[j]) for j in P]
    for j in P:
        sc = jnp.exp(fe[j] + ms[j] - m_new[j])
        c_ref[j] = sc * cs[j] + jnp.where(same_half, upd[j], 0.0)
        n_ref[j] = sc * ns[j] + jnp.sum(kw[j], axis=0, keepdims=True)
        m_ref[j] = m_new[j]
        o_ref[0, :, LANES * j:LANES * (j + 1)] = (o[j] * lax.rsqrt(oss[j] * (1.0 / HEAD_W) + EPS)
                                                  * og_ref[:, LANES * j:LANES * (j + 1)])

    @pl.when(n == n_chunks - 1)
    def _():
        cfin_ref[0] = c_ref[...]
        nfin_ref[0] = n_ref[...]
        mfin_ref[0] = m_ref[...]


def _mlstm_call(pm, c0_bd, n0, m0_rep, ib_rep, fb_rep, og, n_valid):
    B, Lp, _ = pm.shape
    C = CHUNK
    n_chunks = Lp // C
    npair = MLSTM_HEADS // 2
    kern = functools.partial(_mlstm_kernel, C=C, n_valid=n_valid, n_chunks=n_chunks)
    full = lambda shape: pl.BlockSpec(shape, lambda b, n: (0,) * len(shape))
    st_c = pl.BlockSpec((1, npair, LANES, LANES), lambda b, n: (b, 0, 0, 0))
    st_v = pl.BlockSpec((1, npair, 1, LANES), lambda b, n: (b, 0, 0, 0))
    return pl.pallas_call(
        kern,
        out_shape=(jax.ShapeDtypeStruct((B, Lp, MLSTM_WIDTH), F32),
                   jax.ShapeDtypeStruct((B, npair, LANES, LANES), F32),
                   jax.ShapeDtypeStruct((B, npair, 1, LANES), F32),
                   jax.ShapeDtypeStruct((B, npair, 1, LANES), F32)),
        grid=(B, n_chunks),
        in_specs=[pl.BlockSpec((1, C, _ML_COLS), lambda b, n: (b, n, 0)), st_c, st_v, st_v,
                  full((1, MLSTM_WIDTH)), full((1, MLSTM_WIDTH)), full((1, MLSTM_WIDTH))],
        out_specs=(pl.BlockSpec((1, C, MLSTM_WIDTH), lambda b, n: (b, n, 0)), st_c, st_v, st_v),
        scratch_shapes=[pltpu.VMEM((npair, LANES, LANES), F32),
                        pltpu.VMEM((npair, 1, LANES), F32),
                        pltpu.VMEM((npair, 1, LANES), F32)],
        compiler_params=_cparams(("arbitrary", "arbitrary")),
        name="mlstm_chunks",
    )(pm, c0_bd, n0, m0_rep, ib_rep, fb_rep, og)


_MLA_C0 = MLA_Q_LORA
_MLA_R0 = MLA_Q_LORA + MLA_KV_LORA
_MLA_COLS = _MLA_R0 + LANES
_MLA_HW = LANES
_Q_SCALE = MLA_QK ** -0.5 * math.log2(math.e)


def _swap_halves(x, lane, base, half):
    up = pltpu.roll(x, LANES - half, 1)
    dn = pltpu.roll(x, half, 1)
    first = (lane >= base) & (lane < base + half)
    second = (lane >= base + half) & (lane < base + 2 * half)
    return jnp.where(first, up, jnp.where(second, dn, 0.0))


def _mla_q_kernel(pm_ref, qag_ref, wuq_ref, qgain_ref, kvag_ref, krgain_ref,
                  cosq_ref, sinq_ref, cosk_ref, sink_ref, q_ref, lat_ref, kr_ref, *, tb):
    lane = _iota((tb, LANES), 1)
    ql = pm_ref[0, :, 0:_MLA_C0]
    ql = ql * lax.rsqrt(jnp.mean(ql * ql, axis=-1, keepdims=True) + EPS) * qag_ref[...]
    q = jnp.dot(_bf(ql), wuq_ref[...], preferred_element_type=F32)
    nope = lane < MLA_NOPE
    rope = (lane >= MLA_NOPE) & (lane < MLA_QK)
    cosq = cosq_ref[...]
    sinq = sinq_ref[...]
    for h in range(MLA_HEADS):
        qh = q[:, _MLA_HW * h:_MLA_HW * (h + 1)]
        sq = qh * qh
        ss_n = jnp.sum(jnp.where(nope, sq, 0.0), axis=-1, keepdims=True) * (1.0 / MLA_NOPE)
        ss_r = jnp.sum(jnp.where(rope, sq, 0.0), axis=-1, keepdims=True) * (1.0 / MLA_ROPE)
        rinv = jnp.where(nope, lax.rsqrt(ss_n + EPS), lax.rsqrt(ss_r + EPS))
        qn = qh * rinv * qgain_ref[...]
        sw = _swap_halves(qn, lane, MLA_NOPE, MLA_ROPE // 2)
        qr = jnp.where(nope, qn, qn * cosq + sw * sinq)
        q_ref[0, :, _MLA_HW * h:_MLA_HW * (h + 1)] = _bf(qr * _Q_SCALE)
    ckv = pm_ref[0, :, _MLA_C0:_MLA_R0]
    lat_ref[0] = ckv * lax.rsqrt(jnp.mean(ckv * ckv, axis=-1, keepdims=True) + EPS) * kvag_ref[...]
    kx = pm_ref[0, :, _MLA_R0:_MLA_COLS]
    ssk = jnp.sum(kx * kx, axis=-1, keepdims=True) * (1.0 / MLA_ROPE)
    kn = kx * lax.rsqrt(ssk + EPS) * krgain_ref[...]
    swk = _swap_halves(kn, lane, 0, MLA_ROPE // 2)
    kr = kn * cosk_ref[...] + swk * sink_ref[...]
    kr_ref[0] = kr[:, 0:MLA_ROPE]


def _mla_q_call(pm, qag, wuq, qgain, kvag, krgain, cosq, sinq, cosk, sink):
    B, L, _ = pm.shape
    tb = min(L, 512)
    full = lambda shape: pl.BlockSpec(shape, lambda b, i: (0,) * len(shape))
    tab = pl.BlockSpec((tb, LANES), lambda b, i: (i, 0))
    nq = MLA_HEADS * _MLA_HW
    return pl.pallas_call(
        functools.partial(_mla_q_kernel, tb=tb),
        out_shape=(jax.ShapeDtypeStruct((B, L, nq), BF16),
                   jax.ShapeDtypeStruct((B, L, MLA_KV_LORA), F32),
                   jax.ShapeDtypeStruct((B, L, MLA_ROPE), F32)),
        grid=(B, L // tb),
        in_specs=[pl.BlockSpec((1, tb, _MLA_COLS), lambda b, i: (b, i, 0)),
                  full((1, MLA_Q_LORA)), full((MLA_Q_LORA, nq)), full((1, LANES)),
                  full((1, MLA_KV_LORA)), full((1, LANES)), tab, tab, tab, tab],
        out_specs=(pl.BlockSpec((1, tb, nq), lambda b, i: (b, i, 0)),
                   pl.BlockSpec((1, tb, MLA_KV_LORA), lambda b, i: (b, i, 0)),
                   pl.BlockSpec((1, tb, MLA_ROPE), lambda b, i: (b, i, 0))),
        compiler_params=_cparams(("arbitrary", "arbitrary")),
        name="mla_q_proj",
    )(pm, qag, wuq, qgain, kvag, krgain, cosq, sinq, cosk, sink)


def _mla_kv_kernel(ckv_ref, krp_ref, wukv_ref, kgain_ref, k_ref, kv_ref, *, tb):
    lane = _iota((tb, LANES), 1)
    nope = lane < MLA_NOPE
    kv = jnp.dot(_bf(ckv_ref[0]), wukv_ref[...], preferred_element_type=F32)
    krp = krp_ref[0]
    for h in range(MLA_HEADS):
        g = kv[:, _MLA_HW * h:_MLA_HW * (h + 1)]
        ss = jnp.sum(jnp.where(nope, g * g, 0.0), axis=-1, keepdims=True) * (1.0 / MLA_NOPE)
        k_ref[0, :, _MLA_HW * h:_MLA_HW * (h + 1)] = _bf(g * lax.rsqrt(ss + EPS) * kgain_ref[...] + krp)
    kv_ref[0] = _bf(kv)


def _mla_kv_call(ckv_all, krp_all, wukv, kgain):
    B, Lk, _ = ckv_all.shape
    tb = 512 if Lk % 512 == 0 else Lk
    nk = MLA_HEADS * _MLA_HW
    full = lambda shape: pl.BlockSpec(shape, lambda b, i: (0,) * len(shape))
    return pl.pallas_call(
        functools.partial(_mla_kv_kernel, tb=tb),
        out_shape=(jax.ShapeDtypeStruct((B, Lk, nk), BF16), jax.ShapeDtypeStruct((B, Lk, nk), BF16)),
        grid=(B, Lk // tb),
        in_specs=[pl.BlockSpec((1, tb, MLA_KV_LORA), lambda b, i: (b, i, 0)),
                  pl.BlockSpec((1, tb, LANES), lambda b, i: (b, i, 0)),
                  full((MLA_KV_LORA, nk)), full((1, LANES))],
        out_specs=(pl.BlockSpec((1, tb, nk), lambda b, i: (b, i, 0)),
                   pl.BlockSpec((1, tb, nk), lambda b, i: (b, i, 0))),
        compiler_params=_cparams(("arbitrary", "arbitrary"), VMEM_LIMIT),
        name="mla_kv_proj",
    )(ckv_all, krp_all, wukv, kgain)


def _attn_kernel(q_ref, k_ref, kv_ref, o_ref, m_ref, l_ref, acc_ref, *, tq, tk, nk, causal):
    qi = pl.program_id(2)
    ki = pl.program_id(3)
    last = qi if causal else nk - 1

    @pl.when(ki == 0)
    def _():
        m_ref[...] = jnp.full(m_ref.shape, NEG, F32)
        l_ref[...] = jnp.zeros(l_ref.shape, F32)
        acc_ref[...] = jnp.zeros(acc_ref.shape, F32)

    if tk % LANES == 0:
        widen = lambda v: jnp.concatenate([v] * (tk // LANES), axis=1)
    else:
        widen = lambda v: v[:, 0:1]

    def step(masked):
        if masked:
            qc = (qi * tq + _iota((tq, tk), 0)) >> 6
            kc = (ki * tk + _iota((tq, tk), 1)) >> 6
            keep = kc <= qc
        H = range(2)
        hs = [slice(_MLA_HW * hh, _MLA_HW * (hh + 1)) for hh in H]
        sc = [lax.dot_general(q_ref[0, :, hs[hh]], k_ref[0, :, hs[hh]], (((1,), (1,)), ((), ())),
                              preferred_element_type=F32) for hh in H]
        if masked:
            sc = [jnp.where(keep, t, NEG) for t in sc]
        m_prev = [m_ref[hh] for hh in H]
        m_new = [jnp.maximum(m_prev[hh], jnp.max(sc[hh], axis=-1, keepdims=True)) for hh in H]
        p = [jnp.exp2(sc[hh] - widen(m_new[hh])) for hh in H]
        alpha = [jnp.exp2(m_prev[hh] - m_new[hh]) for hh in H]
        pv = [jnp.dot(_bf(p[hh]), kv_ref[0, :, hs[hh]], preferred_element_type=F32) for hh in H]
        for hh in H:
            l_ref[hh] = alpha[hh] * l_ref[hh] + jnp.sum(p[hh], axis=-1, keepdims=True)
            acc_ref[hh] = alpha[hh] * acc_ref[hh] + pv[hh]
            m_ref[hh] = m_new[hh]

    if causal:
        pl.when(ki < qi)(lambda: step(False))
        pl.when(ki == qi)(lambda: step(True))
    else:
        step(False)

    @pl.when(ki == last)
    def _():
        lane = _iota((tq, LANES), 1)
        o_e = pltpu.roll(acc_ref[0] / l_ref[0], MLA_NOPE, 1)
        o_o = acc_ref[1] / l_ref[1]
        o_ref[0] = jnp.where(lane < MLA_NOPE, o_e, o_o)


def _attn_call(q, k, kv, causal):
    B, Lq, _ = q.shape
    Lk = k.shape[1]
    if causal:
        tq = tk = min(Lq, 512)
    else:
        tq, tk = Lq, Lk
    nq, nk = Lq // tq, Lk // tk
    w2 = 2 * _MLA_HW
    kmap = (lambda b, hp, qi, ki: (b, jnp.minimum(ki, qi), hp)) if causal else (lambda b, hp, qi, ki: (b, ki, hp))
    return pl.pallas_call(
        functools.partial(_attn_kernel, tq=tq, tk=tk, nk=nk, causal=causal),
        out_shape=jax.ShapeDtypeStruct((B, Lq, MLA_HEADS * MLA_NOPE), F32),
        grid=(B, MLA_HEADS // 2, nq, nk),
        in_specs=[pl.BlockSpec((1, tq, w2), lambda b, hp, qi, ki: (b, qi, hp)),
                  pl.BlockSpec((1, tk, w2), kmap),
                  pl.BlockSpec((1, tk, w2), kmap)],
        out_specs=pl.BlockSpec((1, tq, LANES), lambda b, hp, qi, ki: (b, qi, hp)),
        scratch_shapes=[pltpu.VMEM((2, tq, LANES), F32), pltpu.VMEM((2, tq, LANES), F32),
                        pltpu.VMEM((2, tq, LANES), F32)],
        compiler_params=_cparams(("arbitrary",) * 4, VMEM_LIMIT),
        name="mla_attention",
    )(q, k, kv)


def _outproj_kernel(a_ref, b_ref, c_ref, x_ref, g1_ref, sh_ref, sc_ref, ng_ref, wa_ref, wb_ref, wc_ref,
                    x1_ref, h2_ref):
    y = (jnp.dot(_bf(a_ref[0]), wa_ref[...], preferred_element_type=F32)
         + jnp.dot(_bf(b_ref[0]), wb_ref[...], preferred_element_type=F32)
         + jnp.dot(_bf(c_ref[0]), wc_ref[...], preferred_element_type=F32))
    x1 = x_ref[0] + g1_ref[0] * y
    x1_ref[0] = x1
    n = x1 * lax.rsqrt(jnp.mean(x1 * x1, axis=-1, keepdims=True) + EPS) * ng_ref[...]
    h2_ref[0] = n * (1.0 + sc_ref[0]) + sh_ref[0]


def _outproj_call(oa, ob, oc, x, g1, sh2, sc2, ng, wa, wb, wc):
    B, L, _ = x.shape
    tb = min(L, 512)
    full = lambda shape: pl.BlockSpec(shape, lambda b, i: (0,) * len(shape))
    per_b = pl.BlockSpec((1, 1, D_MODEL), lambda b, i: (b, 0, 0))
    blk = lambda wdt: pl.BlockSpec((1, tb, wdt), lambda b, i: (b, i, 0))
    return pl.pallas_call(
        _outproj_kernel,
        out_shape=(jax.ShapeDtypeStruct((B, L, D_MODEL), F32), jax.ShapeDtypeStruct((B, L, D_MODEL), F32)),
        grid=(B, L // tb),
        in_specs=[blk(GDN_WIDTH), blk(MLA_HEADS * MLA_NOPE), blk(MLSTM_WIDTH), blk(D_MODEL),
                  per_b, per_b, per_b, full((1, D_MODEL)), full(wa.shape), full(wb.shape), full(wc.shape)],
        out_specs=(blk(D_MODEL), blk(D_MODEL)),
        compiler_params=_cparams(("arbitrary", "arbitrary"), VMEM_LIMIT),
        name="out_proj",
    )(oa, ob, oc, x, g1, sh2, sc2, ng, wa, wb, wc)


def _topk_rows(svs, io, k, payloads=None):
    svs = list(svs)
    vals = [[] for _ in svs]
    outs = [[] for _ in svs]
    big = jnp.float32(1e9)
    for _ in range(k):
        for a in range(len(svs)):
            m = jnp.max(svs[a], axis=0, keepdims=True)
            ix = jnp.min(jnp.where(svs[a] == m, io, big), axis=0, keepdims=True)
            hit = io == ix
            svs[a] = jnp.where(hit, -jnp.inf, svs[a])
            vals[a].append(m)
            outs[a].append(ix if payloads is None
                           else jnp.max(jnp.where(hit, payloads[a], -1.0), axis=0, keepdims=True))
    return [(jnp.concatenate(v, axis=0), jnp.concatenate(o, axis=0)) for v, o in zip(vals, outs)]


_CAND_ROWS = PEER_TOPK + 7 * 8 + 8


def _cand_blocks(t1, t2, combine):
    blocks = [combine(t1[0:1], t2)]
    blocks += [combine(t1[a:a + 1], t2[0:8]) for a in range(1, 8)]
    blocks.append(combine(t1[8:PEER_TOPK], t2[0:1]))
    return jnp.concatenate(blocks, axis=0)


def _route_kernel(h_ref, wq_ref, keys_ref, e_ref, g_ref, qs_ref, es_ref, gs_ref, *, tb):
    q = jnp.dot(_bf(h_ref[...]), wq_ref[...], preferred_element_type=F32)
    for g in range(2 * PEER_HEADS):
        qs_ref[g] = _bf(q[:, LANES * g:LANES * (g + 1)])
    io_k = _iota((N_KEYS, tb), 0).astype(F32)
    r = _iota((_CAND_ROWS, tb), 0)
    mid = r - PEER_TOPK
    io_c = jnp.where(r < PEER_TOPK, r,
                     jnp.where(r < _CAND_ROWS - 8, ((mid >> 3) + 1) * PEER_TOPK + (mid & 7),
                               (r - (_CAND_ROWS - 16)) * PEER_TOPK)).astype(F32)

    def head(h, carry):
        scores = [lax.dot_general(keys_ref[2 * h + p], qs_ref[2 * h + p], (((1,), (1,)), ((), ())),
                                  preferred_element_type=F32) for p in range(2)]
        (v1, i1), (v2, i2) = _topk_rows(scores, io_k, PEER_TOPK)
        cand = _cand_blocks(v1, v2, lambda x, y: x + y)
        expert = _cand_blocks(i1, i2, lambda x, y: x * N_KEYS + y)
        (sc, e), = _topk_rows([cand], io_c, PEER_TOPK, payloads=[expert])
        ex = jnp.exp(sc - jnp.max(sc, axis=0, keepdims=True))
        es_ref[h] = e * _ROW_SUB
        gs_ref[h] = ex / jnp.sum(ex, axis=0, keepdims=True)
        return carry

    lax.fori_loop(0, PEER_HEADS, head, 0)
    e_ref[...] = es_ref[...].reshape(PEER_SLOTS, tb).T.astype(jnp.int32)
    g_ref[...] = gs_ref[...].reshape(PEER_SLOTS, tb).T


def _route_call(h2, wq, keys):
    T = h2.shape[0]
    tb = min(T, 256)
    full = lambda shape: pl.BlockSpec(shape, lambda i: (0,) * len(shape))
    return pl.pallas_call(
        functools.partial(_route_kernel, tb=tb),
        out_shape=(jax.ShapeDtypeStruct((T, PEER_SLOTS), jnp.int32), jax.ShapeDtypeStruct((T, PEER_SLOTS), F32)),
        grid=(T // tb,),
        in_specs=[pl.BlockSpec((tb, D_MODEL), lambda i: (i, 0)), full(wq.shape), full(keys.shape)],
        out_specs=(pl.BlockSpec((tb, PEER_SLOTS), lambda i: (i, 0)), pl.BlockSpec((tb, PEER_SLOTS), lambda i: (i, 0))),
        scratch_shapes=[pltpu.VMEM((2 * PEER_HEADS, tb, LANES), BF16),
                        pltpu.VMEM((PEER_HEADS, PEER_TOPK, tb), F32),
                        pltpu.VMEM((PEER_HEADS, PEER_TOPK, tb), F32)],
        compiler_params=_cparams(("arbitrary",), VMEM_LIMIT),
        name="peer_route",
    )(h2, wq, keys)


_ROW_SUB = 4
_GROUP = 8
_BITREV3 = (0, 4, 2, 6, 1, 5, 3, 7)


def _expert_row(tab_ref, row0):
    return pltpu.bitcast(tab_ref[pl.ds(row0, _ROW_SUB), :], BF16).astype(F32)


def _gelu(x):
    return 0.5 * x * (1.0 + lax.erf(x * (2.0 ** -0.5)))


def _fold(a, b, h, sub):
    m = (sub & h) == 0
    if h == 4:
        return jnp.where(m, a, b) + pltpu.roll(jnp.where(m, b, a), 4, 0)
    return jnp.where(m, a + pltpu.roll(a, 8 - h, 0), b + pltpu.roll(b, h, 0))


def _peer_u_kernel(idx_ref, x_ref, g_ref, tab_ref, c_ref, part_ref, *, tb):
    ones8 = jnp.ones((8, LANES), BF16)
    sub = _iota((8, LANES), 0)
    nt = lambda a, b: lax.dot_general(a, b, (((1,), (1,)), ((), ())), preferred_element_type=F32)

    def gather(t, slot, k):
        xv = x_ref[t]
        buf = part_ref.at[slot]
        for g in range(PEER_SLOTS // _GROUP):
            p = [_expert_row(tab_ref, idx_ref[t, _GROUP * g + _BITREV3[i]]) * xv for i in range(_GROUP)]
            z = [_fold(p[2 * i], p[2 * i + 1], 4, sub) for i in range(4)]
            w = [_fold(z[0], z[1], 2, sub), _fold(z[2], z[3], 2, sub)]
            r0 = PEER_SLOTS * k + _GROUP * g
            buf[r0:r0 + _GROUP, :] = _fold(w[0], w[1], 1, sub)

    def finish(grp, slot):
        res = nt(ones8, _bf(part_ref[slot]))
        act = res[:, 0:LANES]
        for k in range(1, 8):
            act = jnp.where(sub == k, res[:, LANES * k:LANES * (k + 1)], act)
        rows = pl.ds(pl.multiple_of(grp * 8, 8), 8)
        c_ref[rows, :] = g_ref[rows, :] * _gelu(act)

    for k in range(8):
        gather(k, 0, k)

    def trip(grp, carry):
        slot = grp & 1
        finish(grp - 1, 1 - slot)
        for k in range(8):
            gather(grp * 8 + k, slot, k)
        return carry

    n_grp = tb // 8
    lax.fori_loop(1, n_grp, trip, 0)
    finish(n_grp - 1, (n_grp - 1) & 1)


def _peer_v_kernel(idx_ref, c_ref, x_ref, g2_ref, tab_ref, o_ref, m_ref, *, tb):
    eye = _iota((PEER_SLOTS, LANES), 0) == _iota((PEER_SLOTS, LANES), 1)
    diag = jnp.where(eye[None], c_ref[...][:, None, :], 0.0).reshape(tb * PEER_SLOTS, LANES)
    rep = jnp.dot(_bf(diag), jnp.ones((LANES, LANES), BF16), preferred_element_type=F32)
    m_ref[...] = rep.reshape(tb, PEER_SLOTS, LANES)

    def tok(t):
        y = jnp.zeros((8, LANES), F32)
        base = t * PEER_SLOTS
        for g in range(PEER_SLOTS // _GROUP):
            rows = [idx_ref[base + k] for k in range(_GROUP)]
            r = [m_ref[t, pl.ds(_GROUP * g + k, 1), :] * _expert_row(tab_ref, rows[k]) for k in range(_GROUP)]
            y = y + (((r[0] + r[1]) + (r[2] + r[3])) + ((r[4] + r[5]) + (r[6] + r[7])))
            base = base + _GROUP + lax.shift_right_arithmetic(rows[-1], jnp.int32(31))
        o_ref[t] = x_ref[t] + g2_ref[0] * y

    per_trip = 2

    def trip(i, carry):
        for k in range(per_trip):
            tok(i * per_trip + k)
        return carry

    lax.fori_loop(0, tb // per_trip, trip, 0)


def _table_spec():
    return pl.BlockSpec((N_EXPERTS * _ROW_SUB, LANES), lambda i: (0, 0), pipeline_mode=pl.Buffered(1))


def _peer_u_call(idx, x3, gate, tab, tb):
    T = idx.shape[0]
    smem = pl.BlockSpec((tb, PEER_SLOTS), lambda i: (i, 0), memory_space=pltpu.SMEM)
    return pl.pallas_call(
        functools.partial(_peer_u_kernel, tb=tb),
        out_shape=jax.ShapeDtypeStruct((T, PEER_SLOTS), F32),
        grid=(T // tb,),
        in_specs=[smem, pl.BlockSpec((tb, 8, LANES), lambda i: (i, 0, 0)),
                  pl.BlockSpec((tb, PEER_SLOTS), lambda i: (i, 0)), _table_spec()],
        out_specs=pl.BlockSpec((tb, PEER_SLOTS), lambda i: (i, 0)),
        scratch_shapes=[pltpu.VMEM((2, 8 * PEER_SLOTS, LANES), F32)],
        compiler_params=_cparams(("arbitrary",), VMEM_LIMIT),
        name="peer_u",
    )(idx, x3, gate, tab)


def _peer_v_call(idx, coef, x3, g2, tab, tb, L):
    T = idx.shape[0]
    smem = pl.BlockSpec((tb * PEER_SLOTS,), lambda i: (i,), memory_space=pltpu.SMEM)
    per_step = L // tb
    idx = idx.reshape(T * PEER_SLOTS)
    return pl.pallas_call(
        functools.partial(_peer_v_kernel, tb=tb),
        out_shape=jax.ShapeDtypeStruct((T, 8, LANES), F32),
        grid=(T // tb,),
        in_specs=[smem, pl.BlockSpec((tb, PEER_SLOTS), lambda i: (i, 0)),
                  pl.BlockSpec((tb, 8, LANES), lambda i: (i, 0, 0)),
                  pl.BlockSpec((1, 8, LANES), lambda i: (i // per_step, 0, 0)), _table_spec()],
        out_specs=pl.BlockSpec((tb, 8, LANES), lambda i: (i, 0, 0)),
        scratch_shapes=[pltpu.VMEM((tb, PEER_SLOTS, LANES), F32)],
        compiler_params=_cparams(("arbitrary",), VMEM_LIMIT),
        name="peer_v",
    )(idx, coef, x3, g2, tab)


def _pack_table(tab):
    bits = lax.bitcast_convert_type(tab.astype(BF16), jnp.uint16).astype(jnp.uint32)
    bits = bits.reshape(tab.shape[0], _ROW_SUB, 2, LANES)
    return (bits[:, :, 0, :] | (bits[:, :, 1, :] << 16)).reshape(tab.shape[0] * _ROW_SUB, LANES)


def _rep_heads(w, width=HEAD_W):
    return jnp.repeat(w, width, axis=-1)


def _to_bd(s):
    B, H = s.shape[:2]
    s = s.reshape(B, H // 2, 2, HEAD_W, HEAD_W)
    z = jnp.zeros_like(s[:, :, 0])
    top = jnp.concatenate([s[:, :, 0], z], axis=-1)
    bot = jnp.concatenate([z, s[:, :, 1]], axis=-1)
    return jnp.concatenate([top, bot], axis=-2)


def _from_bd(s):
    B, P = s.shape[:2]
    return jnp.stack([s[:, :, :HEAD_W, :HEAD_W], s[:, :, HEAD_W:, HEAD_W:]], axis=2).reshape(B, 2 * P, HEAD_W, HEAD_W)


def _prep_layer(l, w):
    o = [0]
    for sz in (GDN_QKV, GDN_WIDTH, GDN_HEADS, GDN_HEADS, MLA_Q_LORA, MLA_KV_LORA, MLA_ROPE,
               MLSTM_QKV, MLSTM_WIDTH, MLSTM_HEADS, MLSTM_HEADS):
        o.append(o[-1] + sz)
    wi = w['w_in'][l]
    col = lambda i: wi[:, o[i]:o[i + 1]]
    p = {}
    p['w_gdn'] = _bf(jnp.concatenate([col(0), col(1), _rep_heads(col(2)), _rep_heads(col(3))], axis=1))
    p['w_mla'] = _bf(jnp.concatenate([col(4), col(5), col(6), jnp.zeros((D_MODEL, LANES - MLA_ROPE), F32)], axis=1))
    p['w_mls'] = _bf(jnp.concatenate([col(7), col(8), _rep_heads(col(9)), _rep_heads(col(10))], axis=1))
    p['alog'] = _rep_heads(w['gdn_a_log'][l]).reshape(1, GDN_WIDTH)
    p['dtb'] = _rep_heads(w['gdn_dt_bias'][l]).reshape(1, GDN_WIDTH)
    p['gdn_og'] = jnp.tile(w['gdn_out_g'][l], 2).reshape(1, LANES)
    p['ib'] = _rep_heads(w['mlstm_i_bias'][l]).reshape(1, MLSTM_WIDTH)
    p['fb'] = _rep_heads(w['mlstm_f_bias'][l]).reshape(1, MLSTM_WIDTH)
    p['mls_og'] = w['mlstm_out_g'][l].reshape(1, MLSTM_WIDTH)
    wuq = w['mla_w_uq'][l].reshape(MLA_Q_LORA, MLA_HEADS, MLA_QK)
    p['wuq'] = _bf(jnp.pad(wuq, ((0, 0), (0, 0), (0, _MLA_HW - MLA_QK))).reshape(MLA_Q_LORA, MLA_HEADS * _MLA_HW))
    p['wukv'] = _bf(w['mla_w_ukv'][l])
    p['qag'] = w['mla_q_a_g'][l].reshape(1, MLA_Q_LORA)
    p['kvag'] = w['mla_kv_a_g'][l].reshape(1, MLA_KV_LORA)
    p['qgain'] = jnp.pad(w['mla_q_gain'][l], (0, LANES - MLA_QK)).reshape(1, LANES)
    kg = w['mla_k_gain'][l]
    p['kgain'] = jnp.pad(kg[:MLA_NOPE], (0, LANES - MLA_NOPE)).reshape(1, LANES)
    p['krgain'] = jnp.pad(kg[MLA_NOPE:], (0, LANES - MLA_ROPE)).reshape(1, LANES)
    wo = w['w_out'][l]
    p['wo_a'] = _bf(wo[0:GDN_WIDTH])
    p['wo_b'] = _bf(wo[GDN_WIDTH:2 * GDN_WIDTH])
    p['wo_c'] = _bf(wo[2 * GDN_WIDTH:])
    p['wq'] = _bf(w['peer_w_q'][l])
    p['keys'] = _bf(w['peer_sub_keys'][l].reshape(2 * PEER_HEADS, N_KEYS, LANES))
    p['u_tab'] = _pack_table(w['peer_u'][l])
    p['v_tab'] = _pack_table(w['peer_v'][l])
    p['conv_w'] = w['gdn_conv_w'][l]
    p['norm_attn_g'] = w['norm_attn_g'][l].reshape(1, D_MODEL)
    p['norm_ffn_g'] = w['norm_ffn_g'][l].reshape(1, D_MODEL)
    return p


def _rope_tables(pos):
    half = MLA_ROPE // 2
    inv = ROPE_THETA ** (-jnp.arange(half, dtype=F32) / half)
    ang = pos.astype(F32)[:, None] * inv[None, :]
    cos, sin = jnp.cos(ang), jnp.sin(ang)
    n = pos.shape[0]
    c2 = jnp.concatenate([cos, cos], axis=1)
    s2 = jnp.concatenate([-sin, sin], axis=1)
    padq = lambda t: jnp.pad(t, ((0, 0), (MLA_NOPE, LANES - MLA_QK)))
    padk = lambda t: jnp.pad(t, ((0, 0), (0, LANES - MLA_ROPE)))
    return padq(c2), padq(s2), padk(c2), padk(s2)


def _pad_rows(a, n):
    return jnp.pad(a, ((0, 0), (0, n - a.shape[1]), (0, 0)))


def _layer(x, mod, p, st, rope, prompt):
    B, L, _ = x.shape
    sh1, sc1, g1, sh2, sc2, g2 = [m.reshape(B, 1, D_MODEL) for m in jnp.split(mod, 6, axis=-1)]
    pg, pm, pl_ = _inproj_call(x, sh1, sc1, p['norm_attn_g'], p['w_gdn'], p['w_mla'], p['w_mls'])
    Lp = -(-L // CHUNK) * CHUNK
    if Lp != L:
        pg, pl_ = _pad_rows(pg, Lp), _pad_rows(pl_, Lp)
    o_a, gdn_bd, conv_new = _gdn_call(pg, st['gdn_conv'], _to_bd(st['gdn']), p['conv_w'],
                                      p['alog'], p['dtb'], p['gdn_og'], L)
    o_c, mc_bd, mn_p, mm_p = _mlstm_call(
        pl_, _to_bd(st['mlstm_c']), st['mlstm_n'].reshape(B, MLSTM_HEADS // 2, 1, LANES),
        _rep_heads(st['mlstm_m']).reshape(B, MLSTM_HEADS // 2, 1, LANES), p['ib'], p['fb'], p['mls_og'], L)
    o_a, o_c = o_a[:, :L], o_c[:, :L]
    q, ckv, kr = _mla_q_call(pm, p['qag'], p['wuq'], p['qgain'], p['kvag'], p['krgain'], *rope)
    krp = lambda t: jnp.pad(t, ((0, 0), (0, 0), (MLA_NOPE, LANES - MLA_QK)))
    k, kv = _mla_kv_call(ckv, krp(kr), p['wukv'], p['kgain'])
    if not prompt:
        k_c, kv_c = _mla_kv_call(st['mla_latent'], krp(st['mla_krope']), p['wukv'], p['kgain'])
        k = jnp.concatenate([k_c, k], axis=1)
        kv = jnp.concatenate([kv_c, kv], axis=1)
    o_b = _attn_call(q, k, kv, prompt)
    x1, h2 = _outproj_call(o_a, o_b, o_c, x, g1, sh2, sc2, p['norm_ffn_g'], p['wo_a'], p['wo_b'], p['wo_c'])
    T = B * L
    h2f = h2.reshape(T, D_MODEL)
    idx, gate = _route_call(h2f, p['wq'], p['keys'])
    tb = min(L, 64)
    coef = _peer_u_call(idx, h2f.reshape(T, 8, LANES), gate, p['u_tab'], tb)
    x2 = _peer_v_call(idx, coef, x1.reshape(T, 8, LANES), g2.reshape(B, 8, LANES), p['v_tab'], tb, L)
    new_state = (ckv, kr, _from_bd(gdn_bd), conv_new, _from_bd(mc_bd),
                 mn_p.reshape(B, MLSTM_HEADS, HEAD_W), mm_p.reshape(B, MLSTM_HEADS, HEAD_W)[:, :, 0])
    return x2.reshape(B, L, D_MODEL), new_state


def kernel(x_prompt, x_sample, c_prompt, c_sample, cache_mla_latent, cache_mla_krope, state_gdn, state_gdn_conv, state_mlstm_c, state_mlstm_n, state_mlstm_m, ada_w, ada_b, norm_attn_g, norm_ffn_g, w_in, gdn_conv_w, gdn_a_log, gdn_dt_bias, gdn_out_g, mla_q_a_g, mla_w_uq, mla_kv_a_g, mla_w_ukv, mla_q_gain, mla_k_gain, mlstm_i_bias, mlstm_f_bias, mlstm_out_g, w_out, peer_w_q, peer_sub_keys, peer_u, peer_v):
    w = dict(ada_w=ada_w, ada_b=ada_b, norm_attn_g=norm_attn_g, norm_ffn_g=norm_ffn_g, w_in=w_in,
             gdn_conv_w=gdn_conv_w, gdn_a_log=gdn_a_log, gdn_dt_bias=gdn_dt_bias, gdn_out_g=gdn_out_g,
             mla_q_a_g=mla_q_a_g, mla_w_uq=mla_w_uq, mla_kv_a_g=mla_kv_a_g, mla_w_ukv=mla_w_ukv,
             mla_q_gain=mla_q_gain, mla_k_gain=mla_k_gain, mlstm_i_bias=mlstm_i_bias,
             mlstm_f_bias=mlstm_f_bias, mlstm_out_g=mlstm_out_g, w_out=w_out, peer_w_q=peer_w_q,
             peer_sub_keys=peer_sub_keys, peer_u=peer_u, peer_v=peer_v)
    B, Lp, _ = x_prompt.shape
    Bs, Ls, _ = x_sample.shape
    past = cache_mla_latent.shape[2]
    rope_p = _rope_tables(jnp.arange(Lp, dtype=jnp.int32))
    rope_s = _rope_tables(past + jnp.arange(Ls, dtype=jnp.int32))
    mods = _mod_call(jnp.concatenate([c_prompt, c_sample], axis=0), ada_w, ada_b)
    xp, xs = x_prompt, x_sample
    new_p, new_s = [], []
    for l in range(DEPTH):
        p = _prep_layer(l, w)
        st_p = {
            'gdn': jnp.zeros((B, GDN_HEADS, HEAD_W, HEAD_W), F32),
            'gdn_conv': jnp.zeros((B, CONV_W - 1, GDN_QKV), F32),
            'mlstm_c': jnp.zeros((B, MLSTM_HEADS, HEAD_W, HEAD_W), F32),
            'mlstm_n': jnp.zeros((B, MLSTM_HEADS, HEAD_W), F32),
            'mlstm_m': jnp.zeros((B, MLSTM_HEADS), F32),
        }
        st_s = {
            'mla_latent': cache_mla_latent[l], 'mla_krope': cache_mla_krope[l],
            'gdn': state_gdn[l], 'gdn_conv': state_gdn_conv[l],
            'mlstm_c': state_mlstm_c[l], 'mlstm_n': state_mlstm_n[l], 'mlstm_m': state_mlstm_m[l],
        }
        xp, sp = _layer(xp, mods[l, :B], p, st_p, rope_p, True)
        xs, ss = _layer(xs, mods[l, B:], p, st_s, rope_s, False)
        new_p.append(sp)
        new_s.append(ss)
    outs_p = [jnp.stack([s[i] for s in new_p]) for i in range(7)]
    outs_s = [jnp.stack([s[i] for s in new_s]) for i in range(7)]
    return (xp, xs, *outs_p, *outs_s)
```

```python
import functools
import math

import jax
import jax.numpy as jnp
from jax import lax
from jax.experimental import pallas as pl
from jax.experimental.pallas import tpu as pltpu

F32 = jnp.float32
BF16 = jnp.bfloat16

D_MODEL = 1024
DEPTH = 2
CHUNK = 64
EPS = 1e-6
GDN_HEADS = 6
GDN_DK = 64
GDN_WIDTH = 384
GDN_QKV = 1152
CONV_W = 4
MLA_HEADS = 6
MLA_NOPE = 64
MLA_ROPE = 32
MLA_QK = 96
MLA_Q_LORA = 384
MLA_KV_LORA = 256
ROPE_THETA = 10000.0
MLSTM_HEADS = 4
MLSTM_WIDTH = 256
MLSTM_QKV = 768
PEER_HEADS = 8
N_KEYS = 128
N_EXPERTS = N_KEYS * N_KEYS
PEER_TOPK = 16
PEER_SLOTS = PEER_HEADS * PEER_TOPK

HEAD_W = 64
LANES = 128
NEG = -1e30
VMEM_LIMIT = 56 * 1024 * 1024


def _bf(x):
    return x.astype(BF16)


def _mm(a, b):
    return jnp.dot(_bf(a), _bf(b), preferred_element_type=F32)


def _mm_nt(a, b):
    return lax.dot_general(_bf(a), _bf(b), (((1,), (1,)), ((), ())), preferred_element_type=F32)


def _split3(x):
    hi = _bf(x)
    r = x - hi.astype(F32)
    mid = _bf(r)
    lo = _bf(r - mid.astype(F32))
    return hi, mid, lo


def _mm_sel_exact(sel_bf, x):
    hi, mid, lo = _split3(x)
    d = lambda t: jnp.dot(sel_bf, t, preferred_element_type=F32)
    return d(hi) + d(mid) + d(lo)


def _sigmoid(x):
    return 1.0 / (1.0 + jnp.exp(-x))


def _silu(x):
    return x * _sigmoid(x)


def _softplus(x):
    return jnp.maximum(x, 0.0) + jnp.log1p(jnp.exp(-jnp.abs(x)))


def _iota(shape, axis):
    return lax.broadcasted_iota(jnp.int32, shape, axis)


def _pair_consts(C):
    row = _iota((C, LANES), 0)
    lane = _iota((C, LANES), 1)
    s = lane & (HEAD_W - 1)
    lo_half = lane < HEAD_W
    r2 = _iota((LANES, LANES), 0)
    l2 = _iota((LANES, LANES), 1)
    same_half = (r2 >> 6) == (l2 >> 6)
    return row, lane, s, lo_half, same_half


def _bd(y, lo_half):
    z = jnp.zeros_like(y)
    return jnp.concatenate([jnp.where(lo_half, y, z), jnp.where(lo_half, z, y)], axis=0)


def _diag_row(x, row, s):
    return jnp.sum(jnp.where(row == s, x, 0.0), axis=0, keepdims=True)


def _cparams(sem, vmem=None):
    kw = dict(dimension_semantics=sem)
    if vmem is not None:
        kw["vmem_limit_bytes"] = vmem
    return pltpu.CompilerParams(**kw)


def _mod_kernel(c_ref, w_ref, b_ref, o_ref):
    c = c_ref[...]
    o_ref[0] = _mm(_silu(c), w_ref[0]) + b_ref[0]


def _mod_call(c_all, ada_w, ada_b):
    nb = c_all.shape[0]
    nj = 6
    return pl.pallas_call(
        _mod_kernel,
        out_shape=jax.ShapeDtypeStruct((DEPTH, nb, 6 * D_MODEL), F32),
        grid=(DEPTH, nj),
        in_specs=[
            pl.BlockSpec((nb, D_MODEL), lambda l, j: (0, 0)),
            pl.BlockSpec((1, D_MODEL, D_MODEL), lambda l, j: (l, 0, j)),
            pl.BlockSpec((1, 1, D_MODEL), lambda l, j: (l, 0, j)),
        ],
        out_specs=pl.BlockSpec((1, nb, D_MODEL), lambda l, j: (l, 0, j)),
        compiler_params=_cparams(("arbitrary", "arbitrary")),
        name="adaln_mod",
    )(c_all, ada_w, ada_b.reshape(DEPTH, 1, 6 * D_MODEL))


def _inproj_kernel(x_ref, sh_ref, sc_ref, g_ref, wg_ref, wm_ref, wl_ref, og_ref, om_ref, ol_ref):
    x = x_ref[0]
    y = x * lax.rsqrt(jnp.mean(x * x, axis=-1, keepdims=True) + EPS) * g_ref[...]
    h = _bf(y * (1.0 + sc_ref[0]) + sh_ref[0])
    og_ref[0] = jnp.dot(h, wg_ref[...], preferred_element_type=F32)
    om_ref[0] = jnp.dot(h, wm_ref[...], preferred_element_type=F32)
    ol_ref[0] = jnp.dot(h, wl_ref[...], preferred_element_type=F32)


def _inproj_call(x, sh, sc, g, wg, wm, wl):
    B, L, _ = x.shape
    tb = min(L, 512)
    ng, nm, nl = wg.shape[1], wm.shape[1], wl.shape[1]
    full = lambda shape: pl.BlockSpec(shape, lambda b, i: (0,) * len(shape))
    per_b = pl.BlockSpec((1, 1, D_MODEL), lambda b, i: (b, 0, 0))
    return pl.pallas_call(
        _inproj_kernel,
        out_shape=(jax.ShapeDtypeStruct((B, L, ng), F32),
                   jax.ShapeDtypeStruct((B, L, nm), F32),
                   jax.ShapeDtypeStruct((B, L, nl), F32)),
        grid=(B, L // tb),
        in_specs=[pl.BlockSpec((1, tb, D_MODEL), lambda b, i: (b, i, 0)), per_b, per_b,
                  full((1, D_MODEL)), full(wg.shape), full(wm.shape), full(wl.shape)],
        out_specs=(pl.BlockSpec((1, tb, ng), lambda b, i: (b, i, 0)),
                   pl.BlockSpec((1, tb, nm), lambda b, i: (b, i, 0)),
                   pl.BlockSpec((1, tb, nl), lambda b, i: (b, i, 0))),
        compiler_params=_cparams(("arbitrary", "arbitrary"), VMEM_LIMIT),
        name="in_proj",
    )(x, sh, sc, g, wg, wm, wl)


_GDN_Z0 = GDN_QKV
_GDN_B0 = GDN_QKV + GDN_WIDTH
_GDN_A0 = _GDN_B0 + GDN_WIDTH
_GDN_COLS = _GDN_A0 + GDN_WIDTH
_CONV_PAD = 8


def _gdn_kernel(pg_ref, conv0_ref, s0_ref, convw_ref, alog_ref, dtb_ref, og_ref,
                o_ref, sfin_ref, convfin_ref, xp_ref, s_ref, *, C, cps, n_valid, n_steps):
    n = pl.program_id(1)
    hist = CONV_W - 1
    R = C * cps

    @pl.when(n == 0)
    def _():
        xp_ref[0:_CONV_PAD, :] = jnp.zeros((_CONV_PAD, GDN_QKV), F32)
        xp_ref[_CONV_PAD - hist:_CONV_PAD, :] = conv0_ref[0]
        s_ref[...] = s0_ref[0]

    a_pre = pg_ref[0, :, 0:GDN_QKV]
    xp_ref[_CONV_PAD:_CONV_PAD + R, :] = a_pre
    w = convw_ref[...]
    y = w[CONV_W - 1:CONV_W] * a_pre
    for j in range(CONV_W - 1):
        y = y + w[j:j + 1] * xp_ref[_CONV_PAD - hist + j:_CONV_PAD - hist + j + R, :]
    y = _silu(y)

    row, lane, s, lo_half, same_half = _pair_consts(C)
    ones_blk = jnp.where(same_half, 1.0, 0.0).astype(BF16)
    tr, tc = _iota((R, R), 0), _iota((R, R), 1)
    tri = jnp.where((tc <= tr) & ((tc >> 6) == (tr >> 6)), 1.0, 0.0).astype(BF16)
    causal = s <= row
    strict = s < row
    eye = s == row

    def hsum(x):
        return jnp.dot(_bf(x), ones_blk, preferred_element_type=F32)

    beta_all = _sigmoid(pg_ref[0, :, _GDN_B0:_GDN_A0])
    gl_all = -jnp.exp(alog_ref[...]) * _softplus(pg_ref[0, :, _GDN_A0:_GDN_COLS] + dtb_ref[...])
    if n_valid < R * n_steps:
        valid = (_iota((R, GDN_WIDTH), 0) + n * R) < n_valid
        beta_all = jnp.where(valid, beta_all, 0.0)
        gl_all = jnp.where(valid, gl_all, 0.0)
    g_all = _mm_sel_exact(tri, gl_all)

    items = [(c, j) for c in range(cps) for j in range(GDN_HEADS // 2)]
    I = range(len(items))
    blk = lambda a, base, i: a[C * items[i][0]:C * (items[i][0] + 1),
                               base + LANES * items[i][1]:base + LANES * (items[i][1] + 1)]
    bd = lambda a: _bd(a, lo_half)
    qk_raw = [jnp.concatenate([blk(y, 0, i), blk(y, GDN_WIDTH, i)], axis=0) for i in I]
    ssq = [hsum(t * t) for t in qk_raw]
    q2 = [qk_raw[i][0:C] * lax.rsqrt(ssq[i][0:C] + EPS) * (GDN_DK ** -0.5) for i in I]
    k2 = [qk_raw[i][C:2 * C] * lax.rsqrt(ssq[i][C:2 * C] + EPS) for i in I]
    v2 = [blk(y, 2 * GDN_WIDTH, i) for i in I]
    beta = [blk(beta_all, 0, i) for i in I]
    G = [blk(g_all, 0, i) for i in I]
    eG = [jnp.exp(G[i]) for i in I]
    decay = []
    for i in I:
        diff = G[i] - _diag_row(G[i], row, s)
        decay.append(jnp.where(causal, jnp.exp(jnp.where(causal, diff, 0.0)), 0.0))
    kbd = [bd(k2[i]) for i in I]
    kq = [_mm_nt(jnp.concatenate([k2[i], q2[i]], axis=0), kbd[i]) for i in I]
    A = [jnp.where(strict, beta[i] * kq[i][0:C] * decay[i], 0.0) for i in I]
    qk = [kq[i][C:2 * C] * decay[i] for i in I]
    first = ((row & 1) == 1) & (s == row - 1)
    T = [jnp.where(eye, 1.0, 0.0) - jnp.where(first, A[i], 0.0) for i in I]
    for lg in range(1, 6):
        sub = (((row >> lg) & 1) == 1) & ((s >> lg) == (row >> lg) - 1)
        t1 = [_mm(T[i], bd(jnp.where(sub, A[i], 0.0))) for i in I]
        t2 = [_mm(t1[i], bd(T[i])) for i in I]
        T = [T[i] - t2[i] for i in I]
    uw = [_mm(T[i], jnp.concatenate([bd(beta[i] * v2[i]), bd(beta[i] * eG[i] * k2[i])], axis=1)) for i in I]
    g_last = [G[i][C - 1:C, :] for i in I]
    kend_t = [(k2[i] * jnp.exp(g_last[i] - G[i])).T for i in I]
    P = range(GDN_HEADS // 2)
    S = [s_ref[j] for j in P]
    o = [None] * len(items)
    for c in range(cps):
        ids = [c * len(P) + j for j in P]
        ws = [_mm(jnp.concatenate([uw[i][:, LANES:2 * LANES], q2[i] * eG[i]], axis=0), S[j])
              for j, i in enumerate(ids)]
        wn = [uw[i][:, 0:LANES] - ws[j][0:C] for j, i in enumerate(ids)]
        upd = [_mm(kend_t[i], wn[j]) for j, i in enumerate(ids)]
        for j, i in enumerate(ids):
            o[i] = ws[j][C:2 * C] + _mm(qk[i], bd(wn[j]))
        S = [jnp.exp(g_last[i]) * S[j] + jnp.where(same_half, upd[j], 0.0) for j, i in enumerate(ids)]
    for j in P:
        s_ref[j] = S[j]
    oss = [hsum(o[i] * o[i]) for i in I]
    for i in I:
        c, j = items[i]
        on = o[i] * lax.rsqrt(oss[i] * (1.0 / HEAD_W) + EPS) * og_ref[...]
        z = pg_ref[0, C * c:C * (c + 1), _GDN_Z0 + LANES * j:_GDN_Z0 + LANES * (j + 1)]
        o_ref[0, C * c:C * (c + 1), LANES * j:LANES * (j + 1)] = on * _silu(z)

    @pl.when(n < n_steps - 1)
    def _():
        xp_ref[_CONV_PAD - hist:_CONV_PAD, :] = xp_ref[_CONV_PAD + R - hist:_CONV_PAD + R, :]

    @pl.when(n == n_steps - 1)
    def _():
        lv = n_valid - (n_steps - 1) * R
        convfin_ref[0] = xp_ref[_CONV_PAD + lv - hist:_CONV_PAD + lv, :]
        sfin_ref[0] = s_ref[...]


def _gdn_call(pg, conv0, s0_bd, conv_w, alog_rep, dtb_rep, og_rep, n_valid):
    B, Lp, _ = pg.shape
    C = CHUNK
    n_chunks = Lp // C
    cps = 4 if n_chunks % 4 == 0 else (2 if n_chunks % 2 == 0 else 1)
    n_steps = n_chunks // cps
    R = C * cps
    npair = GDN_HEADS // 2
    kern = functools.partial(_gdn_kernel, C=C, cps=cps, n_valid=n_valid, n_steps=n_steps)
    full = lambda shape: pl.BlockSpec(shape, lambda b, n: (0,) * len(shape))
    return pl.pallas_call(
        kern,
        out_shape=(jax.ShapeDtypeStruct((B, Lp, GDN_WIDTH), F32),
                   jax.ShapeDtypeStruct((B, npair, LANES, LANES), F32),
                   jax.ShapeDtypeStruct((B, CONV_W - 1, GDN_QKV), F32)),
        grid=(B, n_steps),
        in_specs=[pl.BlockSpec((1, R, _GDN_COLS), lambda b, n: (b, n, 0)),
                  pl.BlockSpec((1, CONV_W - 1, GDN_QKV), lambda b, n: (b, 0, 0)),
                  pl.BlockSpec((1, npair, LANES, LANES), lambda b, n: (b, 0, 0, 0)),
                  full((CONV_W, GDN_QKV)), full((1, GDN_WIDTH)), full((1, GDN_WIDTH)), full((1, LANES))],
        out_specs=(pl.BlockSpec((1, R, GDN_WIDTH), lambda b, n: (b, n, 0)),
                   pl.BlockSpec((1, npair, LANES, LANES), lambda b, n: (b, 0, 0, 0)),
                   pl.BlockSpec((1, CONV_W - 1, GDN_QKV), lambda b, n: (b, 0, 0))),
        scratch_shapes=[pltpu.VMEM((_CONV_PAD + R, GDN_QKV), F32),
                        pltpu.VMEM((npair, LANES, LANES), F32)],
        compiler_params=_cparams(("arbitrary", "arbitrary")),
        name="gdn_chunks",
    )(pg, conv0, s0_bd, conv_w, alog_rep, dtb_rep, og_rep)


_ML_O0 = MLSTM_QKV
_ML_I0 = _ML_O0 + MLSTM_WIDTH
_ML_F0 = _ML_I0 + MLSTM_WIDTH
_ML_COLS = _ML_F0 + MLSTM_WIDTH


def _mlstm_kernel(pm_ref, c0_ref, n0_ref, m0_ref, ib_ref, fb_ref, og_ref,
                  o_ref, cfin_ref, nfin_ref, mfin_ref, c_ref, n_ref, m_ref, *, C, cps, n_valid, n_steps):
    n = pl.program_id(1)
    R = C * cps

    @pl.when(n == 0)
    def _():
        c_ref[...] = c0_ref[0]
        n_ref[...] = n0_ref[0]
        m_ref[...] = m0_ref[0]

    row, lane, s, lo_half, same_half = _pair_consts(C)
    ones_blk = jnp.where(same_half, 1.0, 0.0).astype(BF16)
    tr, tc = _iota((R, R), 0), _iota((R, R), 1)
    tri = jnp.where((tc <= tr) & ((tc >> 6) == (tr >> 6)), 1.0, 0.0).astype(BF16)
    causal = s <= row

    def hsum(x):
        return jnp.dot(_bf(x), ones_blk, preferred_element_type=F32)

    ig_all = pm_ref[0, :, _ML_I0:_ML_F0] + ib_ref[...]
    lf_all = -_softplus(-(pm_ref[0, :, _ML_F0:_ML_COLS] + fb_ref[...]))
    if n_valid < R * n_steps:
        valid = (_iota((R, MLSTM_WIDTH), 0) + n * R) < n_valid
        ig_all = jnp.where(valid, ig_all, NEG)
        lf_all = jnp.where(valid, lf_all, 0.0)
    f_all = _mm_sel_exact(tri, lf_all)

    P = range(MLSTM_HEADS // 2)
    items = [(c, j) for c in range(cps) for j in P]
    I = range(len(items))
    col = lambda base, i: pm_ref[0, C * items[i][0]:C * (items[i][0] + 1),
                                 base + LANES * items[i][1]:base + LANES * (items[i][1] + 1)]
    sub = lambda a, i: a[C * items[i][0]:C * (items[i][0] + 1), LANES * items[i][1]:LANES * (items[i][1] + 1)]
    bd = lambda a: _bd(a, lo_half)
    q2 = [col(0, i) for i in I]
    k2 = [col(MLSTM_WIDTH, i) * (HEAD_W ** -0.5) for i in I]
    v2 = [col(2 * MLSTM_WIDTH, i) for i in I]
    ig = [sub(ig_all, i) for i in I]
    F = [sub(f_all, i) for i in I]
    dm, dmax = [], []
    for i in I:
        d = jnp.where(causal, F[i] - _diag_row(F[i], row, s) + _diag_row(ig[i], row, s), NEG)
        d_e = jnp.max(jnp.where(lo_half, d, NEG), axis=1, keepdims=True)
        d_o = jnp.max(jnp.where(lo_half, NEG, d), axis=1, keepdims=True)
        dm.append(d)
        dmax.append(jnp.where(lo_half, d_e, d_o))
    qk = [_mm_nt(q2[i], bd(k2[i])) for i in I]
    fe = [F[i][C - 1:C, :] for i in I]
    se = [fe[i] - F[i] + ig[i] for i in I]
    se_max = [jnp.max(se[i], axis=0, keepdims=True) for i in I]
    ms, m_new = [None] * len(items), [None] * len(items)
    m_run = [m_ref[j] for j in P]
    for i in I:
        j = items[i][1]
        ms[i] = m_run[j]
        m_new[i] = jnp.maximum(fe[i] + ms[i], se_max[i])
        m_run[j] = m_new[i]
    mt = [jnp.maximum(F[i] + ms[i], dmax[i]) for i in I]
    inter = [jnp.exp(F[i] + ms[i] - mt[i]) for i in I]
    wgt = [jnp.exp(dm[i] - mt[i]) * qk[i] for i in I]
    wv = [_mm(wgt[i], bd(v2[i])) for i in I]
    wsum = [hsum(wgt[i]) for i in I]
    kw = [k2[i] * jnp.exp(se[i] - m_new[i]) for i in I]
    upd = [_mm(kw[i].T, v2[i]) for i in I]
    ksum = [jnp.sum(kw[i], axis=0, keepdims=True) for i in I]
    sc = [jnp.exp(fe[i] + ms[i] - m_new[i]) for i in I]
    cs = [c_ref[j] for j in P]
    ns = [n_ref[j] for j in P]
    o = [None] * len(items)
    for i in I:
        j = items[i][1]
        den = inter[i] * hsum(q2[i] * ns[j]) + wsum[i]
        h = (inter[i] * _mm(q2[i], cs[j]) + wv[i]) / jnp.maximum(jnp.abs(den), jnp.exp(-mt[i]))
        o[i] = _sigmoid(col(_ML_O0, i)) * h
        cs[j] = sc[i] * cs[j] + jnp.where(same_half, upd[i], 0.0)
        ns[j] = sc[i] * ns[j] + ksum[i]
    oss = [hsum(o[i] * o[i]) for i in I]
    for j in P:
        c_ref[j] = cs[j]
        n_ref[j] = ns[j]
        m_ref[j] = m_run[j]
    for i in I:
        c, j = items[i]
        o_ref[0, C * c:C * (c + 1), LANES * j:LANES * (j + 1)] = (
            o[i] * lax.rsqrt(oss[i] * (1.0 / HEAD_W) + EPS) * og_ref[:, LANES * j:LANES * (j + 1)])

    @pl.when(n == n_steps - 1)
    def _():
        cfin_ref[0] = c_ref[...]
        nfin_ref[0] = n_ref[...]
        mfin_ref[0] = m_ref[...]


def _mlstm_call(pm, c0_bd, n0, m0_rep, ib_rep, fb_rep, og, n_valid):
    B, Lp, _ = pm.shape
    C = CHUNK
    n_chunks = Lp // C
    cps = 4 if n_chunks % 4 == 0 else (2 if n_chunks % 2 == 0 else 1)
    n_steps = n_chunks // cps
    R = C * cps
    npair = MLSTM_HEADS // 2
    kern = functools.partial(_mlstm_kernel, C=C, cps=cps, n_valid=n_valid, n_steps=n_steps)
    full = lambda shape: pl.BlockSpec(shape, lambda b, n: (0,) * len(shape))
    st_c = pl.BlockSpec((1, npair, LANES, LANES), lambda b, n: (b, 0, 0, 0))
    st_v = pl.BlockSpec((1, npair, 1, LANES), lambda b, n: (b, 0, 0, 0))
    return pl.pallas_call(
        kern,
        out_shape=(jax.ShapeDtypeStruct((B, Lp, MLSTM_WIDTH), F32),
                   jax.ShapeDtypeStruct((B, npair, LANES, LANES), F32),
                   jax.ShapeDtypeStruct((B, npair, 1, LANES), F32),
                   jax.ShapeDtypeStruct((B, npair, 1, LANES), F32)),
        grid=(B, n_steps),
        in_specs=[pl.BlockSpec((1, R, _ML_COLS), lambda b, n: (b, n, 0)), st_c, st_v, st_v,
                  full((1, MLSTM_WIDTH)), full((1, MLSTM_WIDTH)), full((1, MLSTM_WIDTH))],
        out_specs=(pl.BlockSpec((1, R, MLSTM_WIDTH), lambda b, n: (b, n, 0)), st_c, st_v, st_v),
        scratch_shapes=[pltpu.VMEM((npair, LANES, LANES), F32),
                        pltpu.VMEM((npair, 1, LANES), F32),
                        pltpu.VMEM((npair, 1, LANES), F32)],
        compiler_params=_cparams(("arbitrary", "arbitrary")),
        name="mlstm_chunks",
    )(pm, c0_bd, n0, m0_rep, ib_rep, fb_rep, og)


_MLA_C0 = MLA_Q_LORA
_MLA_R0 = MLA_Q_LORA + MLA_KV_LORA
_MLA_COLS = _MLA_R0 + LANES
_MLA_HW = LANES
_Q_SCALE = MLA_QK ** -0.5 * math.log2(math.e)


def _swap_halves(x, lane, base, half):
    up = pltpu.roll(x, LANES - half, 1)
    dn = pltpu.roll(x, half, 1)
    first = (lane >= base) & (lane < base + half)
    second = (lane >= base + half) & (lane < base + 2 * half)
    return jnp.where(first, up, jnp.where(second, dn, 0.0))


def _mla_q_kernel(pm_ref, qag_ref, wuq_ref, qgain_ref, kvag_ref, krgain_ref,
                  cosq_ref, sinq_ref, cosk_ref, sink_ref, q_ref, lat_ref, kr_ref, *, tb):
    lane = _iota((tb, LANES), 1)
    ql = pm_ref[0, :, 0:_MLA_C0]
    ql = ql * lax.rsqrt(jnp.mean(ql * ql, axis=-1, keepdims=True) + EPS) * qag_ref[...]
    q = jnp.dot(_bf(ql), wuq_ref[...], preferred_element_type=F32)
    nope = lane < MLA_NOPE
    rope = (lane >= MLA_NOPE) & (lane < MLA_QK)
    cosq = cosq_ref[...]
    sinq = sinq_ref[...]
    for h in range(MLA_HEADS):
        qh = q[:, _MLA_HW * h:_MLA_HW * (h + 1)]
        sq = qh * qh
        ss_n = jnp.sum(jnp.where(nope, sq, 0.0), axis=-1, keepdims=True) * (1.0 / MLA_NOPE)
        ss_r = jnp.sum(jnp.where(rope, sq, 0.0), axis=-1, keepdims=True) * (1.0 / MLA_ROPE)
        rinv = jnp.where(nope, lax.rsqrt(ss_n + EPS), lax.rsqrt(ss_r + EPS))
        qn = qh * rinv * qgain_ref[...]
        sw = _swap_halves(qn, lane, MLA_NOPE, MLA_ROPE // 2)
        qr = jnp.where(nope, qn, qn * cosq + sw * sinq)
        q_ref[0, :, _MLA_HW * h:_MLA_HW * (h + 1)] = _bf(qr * _Q_SCALE)
    ckv = pm_ref[0, :, _MLA_C0:_MLA_R0]
    lat_ref[0] = ckv * lax.rsqrt(jnp.mean(ckv * ckv, axis=-1, keepdims=True) + EPS) * kvag_ref[...]
    kx = pm_ref[0, :, _MLA_R0:_MLA_COLS]
    ssk = jnp.sum(kx * kx, axis=-1, keepdims=True) * (1.0 / MLA_ROPE)
    kn = kx * lax.rsqrt(ssk + EPS) * krgain_ref[...]
    swk = _swap_halves(kn, lane, 0, MLA_ROPE // 2)
    kr = kn * cosk_ref[...] + swk * sink_ref[...]
    kr_ref[0] = kr[:, 0:MLA_ROPE]


def _mla_q_call(pm, qag, wuq, qgain, kvag, krgain, cosq, sinq, cosk, sink):
    B, L, _ = pm.shape
    tb = min(L, 512)
    full = lambda shape: pl.BlockSpec(shape, lambda b, i: (0,) * len(shape))
    tab = pl.BlockSpec((tb, LANES), lambda b, i: (i, 0))
    nq = MLA_HEADS * _MLA_HW
    return pl.pallas_call(
        functools.partial(_mla_q_kernel, tb=tb),
        out_shape=(jax.ShapeDtypeStruct((B, L, nq), BF16),
                   jax.ShapeDtypeStruct((B, L, MLA_KV_LORA), F32),
                   jax.ShapeDtypeStruct((B, L, MLA_ROPE), F32)),
        grid=(B, L // tb),
        in_specs=[pl.BlockSpec((1, tb, _MLA_COLS), lambda b, i: (b, i, 0)),
                  full((1, MLA_Q_LORA)), full((MLA_Q_LORA, nq)), full((1, LANES)),
                  full((1, MLA_KV_LORA)), full((1, LANES)), tab, tab, tab, tab],
        out_specs=(pl.BlockSpec((1, tb, nq), lambda b, i: (b, i, 0)),
                   pl.BlockSpec((1, tb, MLA_KV_LORA), lambda b, i: (b, i, 0)),
                   pl.BlockSpec((1, tb, MLA_ROPE), lambda b, i: (b, i, 0))),
        compiler_params=_cparams(("arbitrary", "arbitrary")),
        name="mla_q_proj",
    )(pm, qag, wuq, qgain, kvag, krgain, cosq, sinq, cosk, sink)


def _mla_kv_kernel(ckv_ref, krp_ref, wukv_ref, kgain_ref, k_ref, kv_ref, *, tb):
    lane = _iota((tb, LANES), 1)
    nope = lane < MLA_NOPE
    kv = jnp.dot(_bf(ckv_ref[0]), wukv_ref[...], preferred_element_type=F32)
    krp = krp_ref[0]
    for h in range(MLA_HEADS):
        g = kv[:, _MLA_HW * h:_MLA_HW * (h + 1)]
        ss = jnp.sum(jnp.where(nope, g * g, 0.0), axis=-1, keepdims=True) * (1.0 / MLA_NOPE)
        k_ref[0, :, _MLA_HW * h:_MLA_HW * (h + 1)] = _bf(g * lax.rsqrt(ss + EPS) * kgain_ref[...] + krp)
    kv_ref[0] = _bf(kv)


def _mla_kv_call(ckv_all, krp_all, wukv, kgain):
    B, Lk, _ = ckv_all.shape
    tb = 512 if Lk % 512 == 0 else Lk
    nk = MLA_HEADS * _MLA_HW
    full = lambda shape: pl.BlockSpec(shape, lambda b, i: (0,) * len(shape))
    return pl.pallas_call(
        functools.partial(_mla_kv_kernel, tb=tb),
        out_shape=(jax.ShapeDtypeStruct((B, Lk, nk), BF16), jax.ShapeDtypeStruct((B, Lk, nk), BF16)),
        grid=(B, Lk // tb),
        in_specs=[pl.BlockSpec((1, tb, MLA_KV_LORA), lambda b, i: (b, i, 0)),
                  pl.BlockSpec((1, tb, LANES), lambda b, i: (b, i, 0)),
                  full((MLA_KV_LORA, nk)), full((1, LANES))],
        out_specs=(pl.BlockSpec((1, tb, nk), lambda b, i: (b, i, 0)),
                   pl.BlockSpec((1, tb, nk), lambda b, i: (b, i, 0))),
        compiler_params=_cparams(("arbitrary", "arbitrary"), VMEM_LIMIT),
        name="mla_kv_proj",
    )(ckv_all, krp_all, wukv, kgain)


def _attn_kernel(q_ref, k_ref, kv_ref, o_ref, m_ref, l_ref, acc_ref, *, tq, tk, nk, causal):
    qi = pl.program_id(2)
    ki = pl.program_id(3)
    last = _last_kv_block(qi, tq, tk) if causal else nk - 1
    first_diag = (qi * tq) // tk

    @pl.when(ki == 0)
    def _():
        m_ref[...] = jnp.full(m_ref.shape, NEG, F32)
        l_ref[...] = jnp.zeros(l_ref.shape, F32)
        acc_ref[...] = jnp.zeros(acc_ref.shape, F32)

    if tk % LANES == 0:
        widen = lambda v: jnp.concatenate([v] * (tk // LANES), axis=1)
    else:
        widen = lambda v: v[:, 0:1]

    def step(masked):
        if masked:
            qc = (qi * tq + _iota((tq, tk), 0)) >> 6
            kc = (ki * tk + _iota((tq, tk), 1)) >> 6
            keep = kc <= qc
        H = range(2)
        hs = [slice(_MLA_HW * hh, _MLA_HW * (hh + 1)) for hh in H]
        sc = [lax.dot_general(q_ref[0, :, hs[hh]], k_ref[0, :, hs[hh]], (((1,), (1,)), ((), ())),
                              preferred_element_type=F32) for hh in H]
        if masked:
            sc = [jnp.where(keep, t, NEG) for t in sc]
        m_prev = [m_ref[hh] for hh in H]
        m_new = [jnp.maximum(m_prev[hh], jnp.max(sc[hh], axis=-1, keepdims=True)) for hh in H]
        p = [jnp.exp2(sc[hh] - widen(m_new[hh])) for hh in H]
        alpha = [jnp.exp2(m_prev[hh] - m_new[hh]) for hh in H]
        pv = [jnp.dot(_bf(p[hh]), kv_ref[0, :, hs[hh]], preferred_element_type=F32) for hh in H]
        for hh in H:
            l_ref[hh] = alpha[hh] * l_ref[hh] + jnp.sum(p[hh], axis=-1, keepdims=True)
            acc_ref[hh] = alpha[hh] * acc_ref[hh] + pv[hh]
            m_ref[hh] = m_new[hh]

    if causal:
        pl.when(ki < first_diag)(lambda: step(False))
        pl.when((ki >= first_diag) & (ki <= last))(lambda: step(True))
    else:
        step(False)

    @pl.when(ki == last)
    def _():
        lane = _iota((tq, LANES), 1)
        o_e = pltpu.roll(acc_ref[0] / l_ref[0], MLA_NOPE, 1)
        o_o = acc_ref[1] / l_ref[1]
        o_ref[0] = jnp.where(lane < MLA_NOPE, o_e, o_o)


def _last_kv_block(qi, tq, tk):
    return ((qi + 1) * tq - 1) // tk


def _attn_call(q, k, kv, causal):
    B, Lq, _ = q.shape
    Lk = k.shape[1]
    if causal:
        tq = min(Lq, 512)
        tk = min(Lk, 1024)
    else:
        tq, tk = Lq, Lk
    nq, nk = Lq // tq, Lk // tk
    w2 = 2 * _MLA_HW
    if causal:
        kmap = lambda b, hp, qi, ki: (b, jnp.minimum(ki, _last_kv_block(qi, tq, tk)), hp)
    else:
        kmap = lambda b, hp, qi, ki: (b, ki, hp)
    return pl.pallas_call(
        functools.partial(_attn_kernel, tq=tq, tk=tk, nk=nk, causal=causal),
        out_shape=jax.ShapeDtypeStruct((B, Lq, MLA_HEADS * MLA_NOPE), F32),
        grid=(B, MLA_HEADS // 2, nq, nk),
        in_specs=[pl.BlockSpec((1, tq, w2), lambda b, hp, qi, ki: (b, qi, hp)),
                  pl.BlockSpec((1, tk, w2), kmap),
                  pl.BlockSpec((1, tk, w2), kmap)],
        out_specs=pl.BlockSpec((1, tq, LANES), lambda b, hp, qi, ki: (b, qi, hp)),
        scratch_shapes=[pltpu.VMEM((2, tq, LANES), F32), pltpu.VMEM((2, tq, LANES), F32),
                        pltpu.VMEM((2, tq, LANES), F32)],
        compiler_params=_cparams(("arbitrary",) * 4, VMEM_LIMIT),
        name="mla_attention",
    )(q, k, kv)


def _outproj_kernel(a_ref, b_ref, c_ref, x_ref, g1_ref, sh_ref, sc_ref, ng_ref, wa_ref, wb_ref, wc_ref,
                    x1_ref, h2_ref):
    y = (jnp.dot(_bf(a_ref[0]), wa_ref[...], preferred_element_type=F32)
         + jnp.dot(_bf(b_ref[0]), wb_ref[...], preferred_element_type=F32)
         + jnp.dot(_bf(c_ref[0]), wc_ref[...], preferred_element_type=F32))
    x1 = x_ref[0] + g1_ref[0] * y
    x1_ref[0] = x1
    n = x1 * lax.rsqrt(jnp.mean(x1 * x1, axis=-1, keepdims=True) + EPS) * ng_ref[...]
    h2_ref[0] = n * (1.0 + sc_ref[0]) + sh_ref[0]


def _outproj_call(oa, ob, oc, x, g1, sh2, sc2, ng, wa, wb, wc):
    B, L, _ = x.shape
    tb = min(L, 512)
    full = lambda shape: pl.BlockSpec(shape, lambda b, i: (0,) * len(shape))
    per_b = pl.BlockSpec((1, 1, D_MODEL), lambda b, i: (b, 0, 0))
    blk = lambda wdt: pl.BlockSpec((1, tb, wdt), lambda b, i: (b, i, 0))
    return pl.pallas_call(
        _outproj_kernel,
        out_shape=(jax.ShapeDtypeStruct((B, L, D_MODEL), F32), jax.ShapeDtypeStruct((B, L, D_MODEL), F32)),
        grid=(B, L // tb),
        in_specs=[blk(GDN_WIDTH), blk(MLA_HEADS * MLA_NOPE), blk(MLSTM_WIDTH), blk(D_MODEL),
                  per_b, per_b, per_b, full((1, D_MODEL)), full(wa.shape), full(wb.shape), full(wc.shape)],
        out_specs=(blk(D_MODEL), blk(D_MODEL)),
        compiler_params=_cparams(("arbitrary", "arbitrary"), VMEM_LIMIT),
        name="out_proj",
    )(oa, ob, oc, x, g1, sh2, sc2, ng, wa, wb, wc)


def _topk_rows(svs, io, k, payloads=None):
    svs = list(svs)
    vals = [[] for _ in svs]
    outs = [[] for _ in svs]
    big = jnp.float32(1e9)
    for _ in range(k):
        for a in range(len(svs)):
            m = jnp.max(svs[a], axis=0, keepdims=True)
            ix = jnp.min(jnp.where(svs[a] == m, io, big), axis=0, keepdims=True)
            hit = io == ix
            svs[a] = jnp.where(hit, -jnp.inf, svs[a])
            vals[a].append(m)
            outs[a].append(ix if payloads is None
                           else jnp.max(jnp.where(hit, payloads[a], -1.0), axis=0, keepdims=True))
    return [(jnp.concatenate(v, axis=0), jnp.concatenate(o, axis=0)) for v, o in zip(vals, outs)]


_CAND_ROWS = PEER_TOPK + 7 * 8 + 8


def _cand_blocks(t1, t2, combine):
    blocks = [combine(t1[0:1], t2)]
    blocks += [combine(t1[a:a + 1], t2[0:8]) for a in range(1, 8)]
    blocks.append(combine(t1[8:PEER_TOPK], t2[0:1]))
    return jnp.concatenate(blocks, axis=0)


def _route_kernel(h_ref, wq_ref, keys_ref, e_ref, g_ref, qs_ref, es_ref, gs_ref, *, tb):
    q = jnp.dot(_bf(h_ref[...]), wq_ref[...], preferred_element_type=F32)
    for g in range(2 * PEER_HEADS):
        qs_ref[g] = _bf(q[:, LANES * g:LANES * (g + 1)])
    io_k = _iota((N_KEYS, tb), 0).astype(F32)
    r = _iota((_CAND_ROWS, tb), 0)
    mid = r - PEER_TOPK
    io_c = jnp.where(r < PEER_TOPK, r,
                     jnp.where(r < _CAND_ROWS - 8, ((mid >> 3) + 1) * PEER_TOPK + (mid & 7),
                               (r - (_CAND_ROWS - 16)) * PEER_TOPK)).astype(F32)

    def head(h, carry):
        scores = [lax.dot_general(keys_ref[2 * h + p], qs_ref[2 * h + p], (((1,), (1,)), ((), ())),
                                  preferred_element_type=F32) for p in range(2)]
        (v1, i1), (v2, i2) = _topk_rows(scores, io_k, PEER_TOPK)
        cand = _cand_blocks(v1, v2, lambda x, y: x + y)
        expert = _cand_blocks(i1, i2, lambda x, y: x * N_KEYS + y)
        (sc, e), = _topk_rows([cand], io_c, PEER_TOPK, payloads=[expert])
        ex = jnp.exp(sc - jnp.max(sc, axis=0, keepdims=True))
        es_ref[h] = e * _ROW_SUB
        gs_ref[h] = ex / jnp.sum(ex, axis=0, keepdims=True)
        return carry

    lax.fori_loop(0, PEER_HEADS, head, 0)
    e_ref[...] = es_ref[...].reshape(PEER_SLOTS, tb).T.astype(jnp.int32)
    g_ref[...] = gs_ref[...].reshape(PEER_SLOTS, tb).T


def _route_call(h2, wq, keys):
    T = h2.shape[0]
    tb = min(T, 256)
    full = lambda shape: pl.BlockSpec(shape, lambda i: (0,) * len(shape))
    return pl.pallas_call(
        functools.partial(_route_kernel, tb=tb),
        out_shape=(jax.ShapeDtypeStruct((T, PEER_SLOTS), jnp.int32), jax.ShapeDtypeStruct((T, PEER_SLOTS), F32)),
        grid=(T // tb,),
        in_specs=[pl.BlockSpec((tb, D_MODEL), lambda i: (i, 0)), full(wq.shape), full(keys.shape)],
        out_specs=(pl.BlockSpec((tb, PEER_SLOTS), lambda i: (i, 0)), pl.BlockSpec((tb, PEER_SLOTS), lambda i: (i, 0))),
        scratch_shapes=[pltpu.VMEM((2 * PEER_HEADS, tb, LANES), BF16),
                        pltpu.VMEM((PEER_HEADS, PEER_TOPK, tb), F32),
                        pltpu.VMEM((PEER_HEADS, PEER_TOPK, tb), F32)],
        compiler_params=_cparams(("arbitrary",), VMEM_LIMIT),
        name="peer_route",
    )(h2, wq, keys)


_ROW_SUB = 4
_GROUP = 8
_BITREV3 = (0, 4, 2, 6, 1, 5, 3, 7)


def _expert_row(tab_ref, row0):
    return pltpu.bitcast(tab_ref[pl.ds(row0, _ROW_SUB), :], BF16).astype(F32)


def _gelu(x):
    return 0.5 * x * (1.0 + lax.erf(x * (2.0 ** -0.5)))


def _fold(a, b, h, sub):
    m = (sub & h) == 0
    if h == 4:
        return jnp.where(m, a, b) + pltpu.roll(jnp.where(m, b, a), 4, 0)
    return jnp.where(m, a + pltpu.roll(a, 8 - h, 0), b + pltpu.roll(b, h, 0))


def _peer_u_kernel(idx_ref, x_ref, g_ref, tab_ref, c_ref, part_ref, *, tb):
    ones8 = jnp.ones((8, LANES), BF16)
    sub = _iota((8, LANES), 0)
    nt = lambda a, b: lax.dot_general(a, b, (((1,), (1,)), ((), ())), preferred_element_type=F32)

    def gather(t, slot, k):
        xv = x_ref[t]
        buf = part_ref.at[slot]
        for g in range(PEER_SLOTS // _GROUP):
            p = [_expert_row(tab_ref, idx_ref[t, _GROUP * g + _BITREV3[i]]) * xv for i in range(_GROUP)]
            z = [_fold(p[2 * i], p[2 * i + 1], 4, sub) for i in range(4)]
            w = [_fold(z[0], z[1], 2, sub), _fold(z[2], z[3], 2, sub)]
            r0 = PEER_SLOTS * k + _GROUP * g
            buf[r0:r0 + _GROUP, :] = _fold(w[0], w[1], 1, sub)

    def finish(grp, slot):
        res = nt(ones8, _bf(part_ref[slot]))
        act = res[:, 0:LANES]
        for k in range(1, 8):
            act = jnp.where(sub == k, res[:, LANES * k:LANES * (k + 1)], act)
        rows = pl.ds(pl.multiple_of(grp * 8, 8), 8)
        c_ref[rows, :] = g_ref[rows, :] * _gelu(act)

    for k in range(8):
        gather(k, 0, k)

    def trip(grp, carry):
        slot = grp & 1
        finish(grp - 1, 1 - slot)
        for k in range(8):
            gather(grp * 8 + k, slot, k)
        return carry

    n_grp = tb // 8
    lax.fori_loop(1, n_grp, trip, 0)
    finish(n_grp - 1, (n_grp - 1) & 1)


def _peer_v_kernel(idx_ref, c_ref, x_ref, g2_ref, tab_ref, o_ref, m_ref, *, tb):
    eye = _iota((PEER_SLOTS, LANES), 0) == _iota((PEER_SLOTS, LANES), 1)
    diag = jnp.where(eye[None], c_ref[...][:, None, :], 0.0).reshape(tb * PEER_SLOTS, LANES)
    rep = jnp.dot(_bf(diag), jnp.ones((LANES, LANES), BF16), preferred_element_type=F32)
    m_ref[...] = rep.reshape(tb, PEER_SLOTS, LANES)

    def tok(t):
        y = jnp.zeros((8, LANES), F32)
        base = t * PEER_SLOTS
        for g in range(PEER_SLOTS // _GROUP):
            rows = [idx_ref[base + k] for k in range(_GROUP)]
            r = [m_ref[t, pl.ds(_GROUP * g + k, 1), :] * _expert_row(tab_ref, rows[k]) for k in range(_GROUP)]
            y = y + (((r[0] + r[1]) + (r[2] + r[3])) + ((r[4] + r[5]) + (r[6] + r[7])))
            base = base + _GROUP + lax.shift_right_arithmetic(rows[0], jnp.int32(31))
        o_ref[t] = x_ref[t] + g2_ref[0] * y

    per_trip = 2

    def trip(i, carry):
        for k in range(per_trip):
            tok(i * per_trip + k)
        return carry

    lax.fori_loop(0, tb // per_trip, trip, 0)


def _table_spec():
    return pl.BlockSpec((N_EXPERTS * _ROW_SUB, LANES), lambda i: (0, 0), pipeline_mode=pl.Buffered(1))


def _peer_u_call(idx, x3, gate, tab, tb):
    T = idx.shape[0]
    smem = pl.BlockSpec((tb, PEER_SLOTS), lambda i: (i, 0), memory_space=pltpu.SMEM)
    return pl.pallas_call(
        functools.partial(_peer_u_kernel, tb=tb),
        out_shape=jax.ShapeDtypeStruct((T, PEER_SLOTS), F32),
        grid=(T // tb,),
        in_specs=[smem, pl.BlockSpec((tb, 8, LANES), lambda i: (i, 0, 0)),
                  pl.BlockSpec((tb, PEER_SLOTS), lambda i: (i, 0)), _table_spec()],
        out_specs=pl.BlockSpec((tb, PEER_SLOTS), lambda i: (i, 0)),
        scratch_shapes=[pltpu.VMEM((2, 8 * PEER_SLOTS, LANES), F32)],
        compiler_params=_cparams(("arbitrary",), VMEM_LIMIT),
        name="peer_u",
    )(idx, x3, gate, tab)


def _peer_v_call(idx, coef, x3, g2, tab, tb, L):
    T = idx.shape[0]
    smem = pl.BlockSpec((tb * PEER_SLOTS,), lambda i: (i,), memory_space=pltpu.SMEM)
    per_step = L // tb
    idx = idx.reshape(T * PEER_SLOTS)
    return pl.pallas_call(
        functools.partial(_peer_v_kernel, tb=tb),
        out_shape=jax.ShapeDtypeStruct((T, 8, LANES), F32),
        grid=(T // tb,),
        in_specs=[smem, pl.BlockSpec((tb, PEER_SLOTS), lambda i: (i, 0)),
                  pl.BlockSpec((tb, 8, LANES), lambda i: (i, 0, 0)),
                  pl.BlockSpec((1, 8, LANES), lambda i: (i // per_step, 0, 0)), _table_spec()],
        out_specs=pl.BlockSpec((tb, 8, LANES), lambda i: (i, 0, 0)),
        scratch_shapes=[pltpu.VMEM((tb, PEER_SLOTS, LANES), F32)],
        compiler_params=_cparams(("arbitrary",), VMEM_LIMIT),
        name="peer_v",
    )(idx, coef, x3, g2, tab)


def _pack_table(tab):
    bits = lax.bitcast_convert_type(tab.astype(BF16), jnp.uint16).astype(jnp.uint32)
    bits = bits.reshape(tab.shape[0], _ROW_SUB, 2, LANES)
    return (bits[:, :, 0, :] | (bits[:, :, 1, :] << 16)).reshape(tab.shape[0] * _ROW_SUB, LANES)


def _rep_heads(w, width=HEAD_W):
    return jnp.repeat(w, width, axis=-1)


def _to_bd(s):
    B, H = s.shape[:2]
    s = s.reshape(B, H // 2, 2, HEAD_W, HEAD_W)
    z = jnp.zeros_like(s[:, :, 0])
    top = jnp.concatenate([s[:, :, 0], z], axis=-1)
    bot = jnp.concatenate([z, s[:, :, 1]], axis=-1)
    return jnp.concatenate([top, bot], axis=-2)


def _from_bd(s):
    B, P = s.shape[:2]
    return jnp.stack([s[:, :, :HEAD_W, :HEAD_W], s[:, :, HEAD_W:, HEAD_W:]], axis=2).reshape(B, 2 * P, HEAD_W, HEAD_W)


def _prep_layer(l, w):
    o = [0]
    for sz in (GDN_QKV, GDN_WIDTH, GDN_HEADS, GDN_HEADS, MLA_Q_LORA, MLA_KV_LORA, MLA_ROPE,
               MLSTM_QKV, MLSTM_WIDTH, MLSTM_HEADS, MLSTM_HEADS):
        o.append(o[-1] + sz)
    wi = w['w_in'][l]
    col = lambda i: wi[:, o[i]:o[i + 1]]
    p = {}
    p['w_gdn'] = _bf(jnp.concatenate([col(0), col(1), _rep_heads(col(2)), _rep_heads(col(3))], axis=1))
    p['w_mla'] = _bf(jnp.concatenate([col(4), col(5), col(6), jnp.zeros((D_MODEL, LANES - MLA_ROPE), F32)], axis=1))
    p['w_mls'] = _bf(jnp.concatenate([col(7), col(8), _rep_heads(col(9)), _rep_heads(col(10))], axis=1))
    p['alog'] = _rep_heads(w['gdn_a_log'][l]).reshape(1, GDN_WIDTH)
    p['dtb'] = _rep_heads(w['gdn_dt_bias'][l]).reshape(1, GDN_WIDTH)
    p['gdn_og'] = jnp.tile(w['gdn_out_g'][l], 2).reshape(1, LANES)
    p['ib'] = _rep_heads(w['mlstm_i_bias'][l]).reshape(1, MLSTM_WIDTH)
    p['fb'] = _rep_heads(w['mlstm_f_bias'][l]).reshape(1, MLSTM_WIDTH)
    p['mls_og'] = w['mlstm_out_g'][l].reshape(1, MLSTM_WIDTH)
    wuq = w['mla_w_uq'][l].reshape(MLA_Q_LORA, MLA_HEADS, MLA_QK)
    p['wuq'] = _bf(jnp.pad(wuq, ((0, 0), (0, 0), (0, _MLA_HW - MLA_QK))).reshape(MLA_Q_LORA, MLA_HEADS * _MLA_HW))
    p['wukv'] = _bf(w['mla_w_ukv'][l])
    p['qag'] = w['mla_q_a_g'][l].reshape(1, MLA_Q_LORA)
    p['kvag'] = w['mla_kv_a_g'][l].reshape(1, MLA_KV_LORA)
    p['qgain'] = jnp.pad(w['mla_q_gain'][l], (0, LANES - MLA_QK)).reshape(1, LANES)
    kg = w['mla_k_gain'][l]
    p['kgain'] = jnp.pad(kg[:MLA_NOPE], (0, LANES - MLA_NOPE)).reshape(1, LANES)
    p['krgain'] = jnp.pad(kg[MLA_NOPE:], (0, LANES - MLA_ROPE)).reshape(1, LANES)
    wo = w['w_out'][l]
    p['wo_a'] = _bf(wo[0:GDN_WIDTH])
    p['wo_b'] = _bf(wo[GDN_WIDTH:2 * GDN_WIDTH])
    p['wo_c'] = _bf(wo[2 * GDN_WIDTH:])
    p['wq'] = _bf(w['peer_w_q'][l])
    p['keys'] = _bf(w['peer_sub_keys'][l].reshape(2 * PEER_HEADS, N_KEYS, LANES))
    p['u_tab'] = _pack_table(w['peer_u'][l])
    p['v_tab'] = _pack_table(w['peer_v'][l])
    p['conv_w'] = w['gdn_conv_w'][l]
    p['norm_attn_g'] = w['norm_attn_g'][l].reshape(1, D_MODEL)
    p['norm_ffn_g'] = w['norm_ffn_g'][l].reshape(1, D_MODEL)
    return p


def _rope_tables(pos):
    half = MLA_ROPE // 2
    inv = ROPE_THETA ** (-jnp.arange(half, dtype=F32) / half)
    ang = pos.astype(F32)[:, None] * inv[None, :]
    cos, sin = jnp.cos(ang), jnp.sin(ang)
    n = pos.shape[0]
    c2 = jnp.concatenate([cos, cos], axis=1)
    s2 = jnp.concatenate([-sin, sin], axis=1)
    padq = lambda t: jnp.pad(t, ((0, 0), (MLA_NOPE, LANES - MLA_QK)))
    padk = lambda t: jnp.pad(t, ((0, 0), (0, LANES - MLA_ROPE)))
    return padq(c2), padq(s2), padk(c2), padk(s2)


def _pad_rows(a, n):
    return jnp.pad(a, ((0, 0), (0, n - a.shape[1]), (0, 0)))


def _layer(x, mod, p, st, rope, prompt):
    B, L, _ = x.shape
    sh1, sc1, g1, sh2, sc2, g2 = [m.reshape(B, 1, D_MODEL) for m in jnp.split(mod, 6, axis=-1)]
    pg, pm, pl_ = _inproj_call(x, sh1, sc1, p['norm_attn_g'], p['w_gdn'], p['w_mla'], p['w_mls'])
    Lp = -(-L // CHUNK) * CHUNK
    if Lp != L:
        pg, pl_ = _pad_rows(pg, Lp), _pad_rows(pl_, Lp)
    o_a, gdn_bd, conv_new = _gdn_call(pg, st['gdn_conv'], _to_bd(st['gdn']), p['conv_w'],
                                      p['alog'], p['dtb'], p['gdn_og'], L)
    o_c, mc_bd, mn_p, mm_p = _mlstm_call(
        pl_, _to_bd(st['mlstm_c']), st['mlstm_n'].reshape(B, MLSTM_HEADS // 2, 1, LANES),
        _rep_heads(st['mlstm_m']).reshape(B, MLSTM_HEADS // 2, 1, LANES), p['ib'], p['fb'], p['mls_og'], L)
    o_a, o_c = o_a[:, :L], o_c[:, :L]
    q, ckv, kr = _mla_q_call(pm, p['qag'], p['wuq'], p['qgain'], p['kvag'], p['krgain'], *rope)
    krp = lambda t: jnp.pad(t, ((0, 0), (0, 0), (MLA_NOPE, LANES - MLA_QK)))
    k, kv = _mla_kv_call(ckv, krp(kr), p['wukv'], p['kgain'])
    if not prompt:
        k_c, kv_c = _mla_kv_call(st['mla_latent'], krp(st['mla_krope']), p['wukv'], p['kgain'])
        k = jnp.concatenate([k_c, k], axis=1)
        kv = jnp.concatenate([kv_c, kv], axis=1)
    o_b = _attn_call(q, k, kv, prompt)
    x1, h2 = _outproj_call(o_a, o_b, o_c, x, g1, sh2, sc2, p['norm_ffn_g'], p['wo_a'], p['wo_b'], p['wo_c'])
    T = B * L
    h2f = h2.reshape(T, D_MODEL)
    idx, gate = _route_call(h2f, p['wq'], p['keys'])
    tb = min(L, 64)
    coef = _peer_u_call(idx, h2f.reshape(T, 8, LANES), gate, p['u_tab'], tb)
    x2 = _peer_v_call(idx, coef, x1.reshape(T, 8, LANES), g2.reshape(B, 8, LANES), p['v_tab'], tb, L)
    new_state = (ckv, kr, _from_bd(gdn_bd), conv_new, _from_bd(mc_bd),
                 mn_p.reshape(B, MLSTM_HEADS, HEAD_W), mm_p.reshape(B, MLSTM_HEADS, HEAD_W)[:, :, 0])
    return x2.reshape(B, L, D_MODEL), new_state


def kernel(x_prompt, x_sample, c_prompt, c_sample, cache_mla_latent, cache_mla_krope, state_gdn, state_gdn_conv, state_mlstm_c, state_mlstm_n, state_mlstm_m, ada_w, ada_b, norm_attn_g, norm_ffn_g, w_in, gdn_conv_w, gdn_a_log, gdn_dt_bias, gdn_out_g, mla_q_a_g, mla_w_uq, mla_kv_a_g, mla_w_ukv, mla_q_gain, mla_k_gain, mlstm_i_bias, mlstm_f_bias, mlstm_out_g, w_out, peer_w_q, peer_sub_keys, peer_u, peer_v):
    w = dict(ada_w=ada_w, ada_b=ada_b, norm_attn_g=norm_attn_g, norm_ffn_g=norm_ffn_g, w_in=w_in,
             gdn_conv_w=gdn_conv_w, gdn_a_log=gdn_a_log, gdn_dt_bias=gdn_dt_bias, gdn_out_g=gdn_out_g,
             mla_q_a_g=mla_q_a_g, mla_w_uq=mla_w_uq, mla_kv_a_g=mla_kv_a_g, mla_w_ukv=mla_w_ukv,
             mla_q_gain=mla_q_gain, mla_k_gain=mla_k_gain, mlstm_i_bias=mlstm_i_bias,
             mlstm_f_bias=mlstm_f_bias, mlstm_out_g=mlstm_out_g, w_out=w_out, peer_w_q=peer_w_q,
             peer_sub_keys=peer_sub_keys, peer_u=peer_u, peer_v=peer_v)
    B, Lp, _ = x_prompt.shape
    Bs, Ls, _ = x_sample.shape
    past = cache_mla_latent.shape[2]
    rope_p = _rope_tables(jnp.arange(Lp, dtype=jnp.int32))
    rope_s = _rope_tables(past + jnp.arange(Ls, dtype=jnp.int32))
    mods = _mod_call(jnp.concatenate([c_prompt, c_sample], axis=0), ada_w, ada_b)
    xp, xs = x_prompt, x_sample
    new_p, new_s = [], []
    for l in range(DEPTH):
        p = _prep_layer(l, w)
        st_p = {
            'gdn': jnp.zeros((B, GDN_HEADS, HEAD_W, HEAD_W), F32),
            'gdn_conv': jnp.zeros((B, CONV_W - 1, GDN_QKV), F32),
            'mlstm_c': jnp.zeros((B, MLSTM_HEADS, HEAD_W, HEAD_W), F32),
            'mlstm_n': jnp.zeros((B, MLSTM_HEADS, HEAD_W), F32),
            'mlstm_m': jnp.zeros((B, MLSTM_HEADS), F32),
        }
        st_s = {
            'mla_latent': cache_mla_latent[l], 'mla_krope': cache_mla_krope[l],
            'gdn': state_gdn[l], 'gdn_conv': state_gdn_conv[l],
            'mlstm_c': state_mlstm_c[l], 'mlstm_n': state_mlstm_n[l], 'mlstm_m': state_mlstm_m[l],
        }
        xp, sp = _layer(xp, mods[l, :B], p, st_p, rope_p, True)
        xs, ss = _layer(xs, mods[l, B:], p, st_s, rope_s, False)
        new_p.append(sp)
        new_s.append(ss)
    outs_p = [jnp.stack([s[i] for s in new_p]) for i in range(7)]
    outs_s = [jnp.stack([s[i] for s in new_s]) for i in range(7)]
    return (xp, xs, *outs_p, *outs_s)
```

```python
import functools
import math

import jax
import jax.numpy as jnp
from jax import lax
from jax.experimental import pallas as pl
from jax.experimental.pallas import tpu as pltpu

F32 = jnp.float32
BF16 = jnp.bfloat16

D_MODEL = 1024
DEPTH = 2
CHUNK = 64
EPS = 1e-6
GDN_HEADS = 6
GDN_DK = 64
GDN_WIDTH = 384
GDN_QKV = 1152
CONV_W = 4
MLA_HEADS = 6
MLA_NOPE = 64
MLA_ROPE = 32
MLA_QK = 96
MLA_Q_LORA = 384
MLA_KV_LORA = 256
ROPE_THETA = 10000.0
MLSTM_HEADS = 4
MLSTM_WIDTH = 256
MLSTM_QKV = 768
PEER_HEADS = 8
N_KEYS = 128
N_EXPERTS = N_KEYS * N_KEYS
PEER_TOPK = 16
PEER_SLOTS = PEER_HEADS * PEER_TOPK

HEAD_W = 64
LANES = 128
NEG = -1e30
VMEM_LIMIT = 56 * 1024 * 1024


def _bf(x):
    return x.astype(BF16)


def _mm(a, b):
    return jnp.dot(_bf(a), _bf(b), preferred_element_type=F32)


def _mm_nt(a, b):
    return lax.dot_general(_bf(a), _bf(b), (((1,), (1,)), ((), ())), preferred_element_type=F32)


def _split3(x):
    hi = _bf(x)
    r = x - hi.astype(F32)
    mid = _bf(r)
    lo = _bf(r - mid.astype(F32))
    return hi, mid, lo


def _mm_sel_exact(sel_bf, x):
    hi, mid, lo = _split3(x)
    d = lambda t: jnp.dot(sel_bf, t, preferred_element_type=F32)
    return d(hi) + d(mid) + d(lo)


def _sigmoid(x):
    return 1.0 / (1.0 + jnp.exp(-x))


def _silu(x):
    return x * _sigmoid(x)


def _softplus(x):
    return jnp.maximum(x, 0.0) + jnp.log1p(jnp.exp(-jnp.abs(x)))


def _iota(shape, axis):
    return lax.broadcasted_iota(jnp.int32, shape, axis)


def _pair_consts(C):
    row = _iota((C, LANES), 0)
    lane = _iota((C, LANES), 1)
    s = lane & (HEAD_W - 1)
    lo_half = lane < HEAD_W
    r2 = _iota((LANES, LANES), 0)
    l2 = _iota((LANES, LANES), 1)
    same_half = (r2 >> 6) == (l2 >> 6)
    return row, lane, s, lo_half, same_half


def _bd(y, lo_half):
    z = jnp.zeros_like(y)
    return jnp.concatenate([jnp.where(lo_half, y, z), jnp.where(lo_half, z, y)], axis=0)


def _diag_row(x, row, s):
    return jnp.sum(jnp.where(row == s, x, 0.0), axis=0, keepdims=True)


def _cparams(sem, vmem=None):
    kw = dict(dimension_semantics=sem)
    if vmem is not None:
        kw["vmem_limit_bytes"] = vmem
    return pltpu.CompilerParams(**kw)


def _mod_kernel(c_ref, w_ref, b_ref, o_ref):
    c = c_ref[...]
    o_ref[0] = _mm(_silu(c), w_ref[0]) + b_ref[0]


def _mod_call(c_all, ada_w, ada_b):
    nb = c_all.shape[0]
    nj = 6
    return pl.pallas_call(
        _mod_kernel,
        out_shape=jax.ShapeDtypeStruct((DEPTH, nb, 6 * D_MODEL), F32),
        grid=(DEPTH, nj),
        in_specs=[
            pl.BlockSpec((nb, D_MODEL), lambda l, j: (0, 0)),
            pl.BlockSpec((1, D_MODEL, D_MODEL), lambda l, j: (l, 0, j)),
            pl.BlockSpec((1, 1, D_MODEL), lambda l, j: (l, 0, j)),
        ],
        out_specs=pl.BlockSpec((1, nb, D_MODEL), lambda l, j: (l, 0, j)),
        compiler_params=_cparams(("arbitrary", "arbitrary")),
        name="adaln_mod",
    )(c_all, ada_w, ada_b.reshape(DEPTH, 1, 6 * D_MODEL))


def _inproj_kernel(x_ref, sh_ref, sc_ref, g_ref, wg_ref, wm_ref, wl_ref, og_ref, om_ref, ol_ref):
    x = x_ref[0]
    y = x * lax.rsqrt(jnp.mean(x * x, axis=-1, keepdims=True) + EPS) * g_ref[...]
    h = _bf(y * (1.0 + sc_ref[0]) + sh_ref[0])
    og_ref[0] = jnp.dot(h, wg_ref[...], preferred_element_type=F32)
    om_ref[0] = jnp.dot(h, wm_ref[...], preferred_element_type=F32)
    ol_ref[0] = jnp.dot(h, wl_ref[...], preferred_element_type=F32)


def _inproj_call(x, sh, sc, g, wg, wm, wl):
    B, L, _ = x.shape
    tb = min(L, 512)
    ng, nm, nl = wg.shape[1], wm.shape[1], wl.shape[1]
    full = lambda shape: pl.BlockSpec(shape, lambda b, i: (0,) * len(shape))
    per_b = pl.BlockSpec((1, 1, D_MODEL), lambda b, i: (b, 0, 0))
    return pl.pallas_call(
        _inproj_kernel,
        out_shape=(jax.ShapeDtypeStruct((B, L, ng), F32),
                   jax.ShapeDtypeStruct((B, L, nm), F32),
                   jax.ShapeDtypeStruct((B, L, nl), F32)),
        grid=(B, L // tb),
        in_specs=[pl.BlockSpec((1, tb, D_MODEL), lambda b, i: (b, i, 0)), per_b, per_b,
                  full((1, D_MODEL)), full(wg.shape), full(wm.shape), full(wl.shape)],
        out_specs=(pl.BlockSpec((1, tb, ng), lambda b, i: (b, i, 0)),
                   pl.BlockSpec((1, tb, nm), lambda b, i: (b, i, 0)),
                   pl.BlockSpec((1, tb, nl), lambda b, i: (b, i, 0))),
        compiler_params=_cparams(("arbitrary", "arbitrary"), VMEM_LIMIT),
        name="in_proj",
    )(x, sh, sc, g, wg, wm, wl)


_GDN_Z0 = GDN_QKV
_GDN_B0 = GDN_QKV + GDN_WIDTH
_GDN_A0 = _GDN_B0 + GDN_WIDTH
_GDN_COLS = _GDN_A0 + GDN_WIDTH
_CONV_PAD = 8


def _gdn_kernel(pg_ref, conv0_ref, s0_ref, convw_ref, alog_ref, dtb_ref, og_ref,
                o_ref, sfin_ref, convfin_ref, xp_ref, s_ref, *, C, cps, n_valid, n_steps):
    n = pl.program_id(1)
    hist = CONV_W - 1
    R = C * cps

    @pl.when(n == 0)
    def _():
        xp_ref[0:_CONV_PAD, :] = jnp.zeros((_CONV_PAD, GDN_QKV), F32)
        xp_ref[_CONV_PAD - hist:_CONV_PAD, :] = conv0_ref[0]
        s_ref[...] = s0_ref[0]

    a_pre = pg_ref[0, :, 0:GDN_QKV]
    xp_ref[_CONV_PAD:_CONV_PAD + R, :] = a_pre
    w = convw_ref[...]
    y = w[CONV_W - 1:CONV_W] * a_pre
    for j in range(CONV_W - 1):
        y = y + w[j:j + 1] * xp_ref[_CONV_PAD - hist + j:_CONV_PAD - hist + j + R, :]
    y = _silu(y)

    row, lane, s, lo_half, same_half = _pair_consts(C)
    ones_blk = jnp.where(same_half, 1.0, 0.0).astype(BF16)
    tr, tc = _iota((R, R), 0), _iota((R, R), 1)
    tri = jnp.where((tc <= tr) & ((tc >> 6) == (tr >> 6)), 1.0, 0.0).astype(BF16)
    causal = s <= row
    strict = s < row
    eye = s == row

    def hsum(x):
        return jnp.dot(_bf(x), ones_blk, preferred_element_type=F32)

    beta_all = _sigmoid(pg_ref[0, :, _GDN_B0:_GDN_A0])
    gl_all = -jnp.exp(alog_ref[...]) * _softplus(pg_ref[0, :, _GDN_A0:_GDN_COLS] + dtb_ref[...])
    if n_valid < R * n_steps:
        valid = (_iota((R, GDN_WIDTH), 0) + n * R) < n_valid
        beta_all = jnp.where(valid, beta_all, 0.0)
        gl_all = jnp.where(valid, gl_all, 0.0)
    g_all = _mm_sel_exact(tri, gl_all)

    items = [(c, j) for c in range(cps) for j in range(GDN_HEADS // 2)]
    I = range(len(items))
    blk = lambda a, base, i: a[C * items[i][0]:C * (items[i][0] + 1),
                               base + LANES * items[i][1]:base + LANES * (items[i][1] + 1)]
    bd = lambda a: _bd(a, lo_half)
    qk_raw = [jnp.concatenate([blk(y, 0, i), blk(y, GDN_WIDTH, i)], axis=0) for i in I]
    ssq = [hsum(t * t) for t in qk_raw]
    q2 = [qk_raw[i][0:C] * lax.rsqrt(ssq[i][0:C] + EPS) * (GDN_DK ** -0.5) for i in I]
    k2 = [qk_raw[i][C:2 * C] * lax.rsqrt(ssq[i][C:2 * C] + EPS) for i in I]
    v2 = [blk(y, 2 * GDN_WIDTH, i) for i in I]
    beta = [blk(beta_all, 0, i) for i in I]
    G = [blk(g_all, 0, i) for i in I]
    eG = [jnp.exp(G[i]) for i in I]
    decay = []
    for i in I:
        diff = G[i] - _diag_row(G[i], row, s)
        decay.append(jnp.where(causal, jnp.exp(jnp.where(causal, diff, 0.0)), 0.0))
    kbd = [bd(k2[i]) for i in I]
    kq = [_mm_nt(jnp.concatenate([k2[i], q2[i]], axis=0), kbd[i]) for i in I]
    A = [jnp.where(strict, beta[i] * kq[i][0:C] * decay[i], 0.0) for i in I]
    qk = [kq[i][C:2 * C] * decay[i] for i in I]
    first = ((row & 1) == 1) & (s == row - 1)
    T = [jnp.where(eye, 1.0, 0.0) - jnp.where(first, A[i], 0.0) for i in I]
    for lg in range(1, 6):
        sub = (((row >> lg) & 1) == 1) & ((s >> lg) == (row >> lg) - 1)
        t1 = [_mm(T[i], bd(jnp.where(sub, A[i], 0.0))) for i in I]
        t2 = [_mm(t1[i], bd(T[i])) for i in I]
        T = [T[i] - t2[i] for i in I]
    uw = [_mm(T[i], jnp.concatenate([bd(beta[i] * v2[i]), bd(beta[i] * eG[i] * k2[i])], axis=1)) for i in I]
    g_last = [G[i][C - 1:C, :] for i in I]
    kend_t = [(k2[i] * jnp.exp(g_last[i] - G[i])).T for i in I]
    P = range(GDN_HEADS // 2)
    S = [s_ref[j] for j in P]
    o = [None] * len(items)
    for c in range(cps):
        ids = [c * len(P) + j for j in P]
        ws = [_mm(jnp.concatenate([uw[i][:, LANES:2 * LANES], q2[i] * eG[i]], axis=0), S[j])
              for j, i in enumerate(ids)]
        wn = [uw[i][:, 0:LANES] - ws[j][0:C] for j, i in enumerate(ids)]
        upd = [_mm(kend_t[i], wn[j]) for j, i in enumerate(ids)]
        for j, i in enumerate(ids):
            o[i] = ws[j][C:2 * C] + _mm(qk[i], bd(wn[j]))
        S = [jnp.exp(g_last[i]) * S[j] + jnp.where(same_half, upd[j], 0.0) for j, i in enumerate(ids)]
    for j in P:
        s_ref[j] = S[j]
    oss = [hsum(o[i] * o[i]) for i in I]
    for i in I:
        c, j = items[i]
        on = o[i] * lax.rsqrt(oss[i] * (1.0 / HEAD_W) + EPS) * og_ref[...]
        z = pg_ref[0, C * c:C * (c + 1), _GDN_Z0 + LANES * j:_GDN_Z0 + LANES * (j + 1)]
        o_ref[0, C * c:C * (c + 1), LANES * j:LANES * (j + 1)] = on * _silu(z)

    @pl.when(n < n_steps - 1)
    def _():
        xp_ref[_CONV_PAD - hist:_CONV_PAD, :] = xp_ref[_CONV_PAD + R - hist:_CONV_PAD + R, :]

    @pl.when(n == n_steps - 1)
    def _():
        lv = n_valid - (n_steps - 1) * R
        convfin_ref[0] = xp_ref[_CONV_PAD + lv - hist:_CONV_PAD + lv, :]
        sfin_ref[0] = s_ref[...]


def _gdn_call(pg, conv0, s0_bd, conv_w, alog_rep, dtb_rep, og_rep, n_valid):
    B, Lp, _ = pg.shape
    C = CHUNK
    n_chunks = Lp // C
    cps = 4 if n_chunks % 4 == 0 else (2 if n_chunks % 2 == 0 else 1)
    n_steps = n_chunks // cps
    R = C * cps
    npair = GDN_HEADS // 2
    kern = functools.partial(_gdn_kernel, C=C, cps=cps, n_valid=n_valid, n_steps=n_steps)
    full = lambda shape: pl.BlockSpec(shape, lambda b, n: (0,) * len(shape))
    return pl.pallas_call(
        kern,
        out_shape=(jax.ShapeDtypeStruct((B, Lp, GDN_WIDTH), F32),
                   jax.ShapeDtypeStruct((B, npair, LANES, LANES), F32),
                   jax.ShapeDtypeStruct((B, CONV_W - 1, GDN_QKV), F32)),
        grid=(B, n_steps),
        in_specs=[pl.BlockSpec((1, R, _GDN_COLS), lambda b, n: (b, n, 0)),
                  pl.BlockSpec((1, CONV_W - 1, GDN_QKV), lambda b, n: (b, 0, 0)),
                  pl.BlockSpec((1, npair, LANES, LANES), lambda b, n: (b, 0, 0, 0)),
                  full((CONV_W, GDN_QKV)), full((1, GDN_WIDTH)), full((1, GDN_WIDTH)), full((1, LANES))],
        out_specs=(pl.BlockSpec((1, R, GDN_WIDTH), lambda b, n: (b, n, 0)),
                   pl.BlockSpec((1, npair, LANES, LANES), lambda b, n: (b, 0, 0, 0)),
                   pl.BlockSpec((1, CONV_W - 1, GDN_QKV), lambda b, n: (b, 0, 0))),
        scratch_shapes=[pltpu.VMEM((_CONV_PAD + R, GDN_QKV), F32),
                        pltpu.VMEM((npair, LANES, LANES), F32)],
        compiler_params=_cparams(("arbitrary", "arbitrary")),
        name="gdn_chunks",
    )(pg, conv0, s0_bd, conv_w, alog_rep, dtb_rep, og_rep)


_ML_O0 = MLSTM_QKV
_ML_I0 = _ML_O0 + MLSTM_WIDTH
_ML_F0 = _ML_I0 + MLSTM_WIDTH
_ML_COLS = _ML_F0 + MLSTM_WIDTH


def _mlstm_kernel(pm_ref, c0_ref, n0_ref, m0_ref, ib_ref, fb_ref, og_ref,
                  o_ref, cfin_ref, nfin_ref, mfin_ref, c_ref, n_ref, m_ref, *, C, cps, n_valid, n_steps):
    n = pl.program_id(1)
    R = C * cps

    @pl.when(n == 0)
    def _():
        c_ref[...] = c0_ref[0]
        n_ref[...] = n0_ref[0]
        m_ref[...] = m0_ref[0]

    row, lane, s, lo_half, same_half = _pair_consts(C)
    ones_blk = jnp.where(same_half, 1.0, 0.0).astype(BF16)
    tr, tc = _iota((R, R), 0), _iota((R, R), 1)
    tri = jnp.where((tc <= tr) & ((tc >> 6) == (tr >> 6)), 1.0, 0.0).astype(BF16)
    causal = s <= row

    def hsum(x):
        return jnp.dot(_bf(x), ones_blk, preferred_element_type=F32)

    ig_all = pm_ref[0, :, _ML_I0:_ML_F0] + ib_ref[...]
    lf_all = -_softplus(-(pm_ref[0, :, _ML_F0:_ML_COLS] + fb_ref[...]))
    if n_valid < R * n_steps:
        valid = (_iota((R, MLSTM_WIDTH), 0) + n * R) < n_valid
        ig_all = jnp.where(valid, ig_all, NEG)
        lf_all = jnp.where(valid, lf_all, 0.0)
    f_all = _mm_sel_exact(tri, lf_all)

    P = range(MLSTM_HEADS // 2)
    items = [(c, j) for c in range(cps) for j in P]
    I = range(len(items))
    col = lambda base, i: pm_ref[0, C * items[i][0]:C * (items[i][0] + 1),
                                 base + LANES * items[i][1]:base + LANES * (items[i][1] + 1)]
    sub = lambda a, i: a[C * items[i][0]:C * (items[i][0] + 1), LANES * items[i][1]:LANES * (items[i][1] + 1)]
    bd = lambda a: _bd(a, lo_half)
    q2 = [col(0, i) for i in I]
    k2 = [col(MLSTM_WIDTH, i) * (HEAD_W ** -0.5) for i in I]
    v2 = [col(2 * MLSTM_WIDTH, i) for i in I]
    ig = [sub(ig_all, i) for i in I]
    F = [sub(f_all, i) for i in I]
    dm, dmax = [], []
    for i in I:
        d = jnp.where(causal, F[i] - _diag_row(F[i], row, s) + _diag_row(ig[i], row, s), NEG)
        d_e = jnp.max(jnp.where(lo_half, d, NEG), axis=1, keepdims=True)
        d_o = jnp.max(jnp.where(lo_half, NEG, d), axis=1, keepdims=True)
        dm.append(d)
        dmax.append(jnp.where(lo_half, d_e, d_o))
    qk = [_mm_nt(q2[i], bd(k2[i])) for i in I]
    fe = [F[i][C - 1:C, :] for i in I]
    se = [fe[i] - F[i] + ig[i] for i in I]
    se_max = [jnp.max(se[i], axis=0, keepdims=True) for i in I]
    ms, m_new = [None] * len(items), [None] * len(items)
    m_run = [m_ref[j] for j in P]
    for i in I:
        j = items[i][1]
        ms[i] = m_run[j]
        m_new[i] = jnp.maximum(fe[i] + ms[i], se_max[i])
        m_run[j] = m_new[i]
    mt = [jnp.maximum(F[i] + ms[i], dmax[i]) for i in I]
    inter = [jnp.exp(F[i] + ms[i] - mt[i]) for i in I]
    wgt = [jnp.exp(dm[i] - mt[i]) * qk[i] for i in I]
    wv = [_mm(wgt[i], bd(v2[i])) for i in I]
    wsum = [hsum(wgt[i]) for i in I]
    kw = [k2[i] * jnp.exp(se[i] - m_new[i]) for i in I]
    upd = [_mm(kw[i].T, v2[i]) for i in I]
    ksum = [jnp.sum(kw[i], axis=0, keepdims=True) for i in I]
    sc = [jnp.exp(fe[i] + ms[i] - m_new[i]) for i in I]
    cs = [c_ref[j] for j in P]
    ns = [n_ref[j] for j in P]
    o = [None] * len(items)
    for i in I:
        j = items[i][1]
        den = inter[i] * hsum(q2[i] * ns[j]) + wsum[i]
        h = (inter[i] * _mm(q2[i], cs[j]) + wv[i]) / jnp.maximum(jnp.abs(den), jnp.exp(-mt[i]))
        o[i] = _sigmoid(col(_ML_O0, i)) * h
        cs[j] = sc[i] * cs[j] + jnp.where(same_half, upd[i], 0.0)
        ns[j] = sc[i] * ns[j] + ksum[i]
    oss = [hsum(o[i] * o[i]) for i in I]
    for j in P:
        c_ref[j] = cs[j]
        n_ref[j] = ns[j]
        m_ref[j] = m_run[j]
    for i in I:
        c, j = items[i]
        o_ref[0, C * c:C * (c + 1), LANES * j:LANES * (j + 1)] = (
            o[i] * lax.rsqrt(oss[i] * (1.0 / HEAD_W) + EPS) * og_ref[:, LANES * j:LANES * (j + 1)])

    @pl.when(n == n_steps - 1)
    def _():
        cfin_ref[0] = c_ref[...]
        nfin_ref[0] = n_ref[...]
        mfin_ref[0] = m_ref[...]


def _mlstm_call(pm, c0_bd, n0, m0_rep, ib_rep, fb_rep, og, n_valid):
    B, Lp, _ = pm.shape
    C = CHUNK
    n_chunks = Lp // C
    cps = 4 if n_chunks % 4 == 0 else (2 if n_chunks % 2 == 0 else 1)
    n_steps = n_chunks // cps
    R = C * cps
    npair = MLSTM_HEADS // 2
    kern = functools.partial(_mlstm_kernel, C=C, cps=cps, n_valid=n_valid, n_steps=n_steps)
    full = lambda shape: pl.BlockSpec(shape, lambda b, n: (0,) * len(shape))
    st_c = pl.BlockSpec((1, npair, LANES, LANES), lambda b, n: (b, 0, 0, 0))
    st_v = pl.BlockSpec((1, npair, 1, LANES), lambda b, n: (b, 0, 0, 0))
    return pl.pallas_call(
        kern,
        out_shape=(jax.ShapeDtypeStruct((B, Lp, MLSTM_WIDTH), F32),
                   jax.ShapeDtypeStruct((B, npair, LANES, LANES), F32),
                   jax.ShapeDtypeStruct((B, npair, 1, LANES), F32),
                   jax.ShapeDtypeStruct((B, npair, 1, LANES), F32)),
        grid=(B, n_steps),
        in_specs=[pl.BlockSpec((1, R, _ML_COLS), lambda b, n: (b, n, 0)), st_c, st_v, st_v,
                  full((1, MLSTM_WIDTH)), full((1, MLSTM_WIDTH)), full((1, MLSTM_WIDTH))],
        out_specs=(pl.BlockSpec((1, R, MLSTM_WIDTH), lambda b, n: (b, n, 0)), st_c, st_v, st_v),
        scratch_shapes=[pltpu.VMEM((npair, LANES, LANES), F32),
                        pltpu.VMEM((npair, 1, LANES), F32),
                        pltpu.VMEM((npair, 1, LANES), F32)],
        compiler_params=_cparams(("arbitrary", "arbitrary")),
        name="mlstm_chunks",
    )(pm, c0_bd, n0, m0_rep, ib_rep, fb_rep, og)


_MLA_C0 = MLA_Q_LORA
_MLA_R0 = MLA_Q_LORA + MLA_KV_LORA
_MLA_COLS = _MLA_R0 + LANES
_MLA_HW = LANES
_Q_SCALE = MLA_QK ** -0.5 * math.log2(math.e)


def _swap_halves(x, lane, base, half):
    up = pltpu.roll(x, LANES - half, 1)
    dn = pltpu.roll(x, half, 1)
    first = (lane >= base) & (lane < base + half)
    second = (lane >= base + half) & (lane < base + 2 * half)
    return jnp.where(first, up, jnp.where(second, dn, 0.0))


def _mla_q_kernel(pm_ref, qag_ref, wuq_ref, qgain_ref, kvag_ref, krgain_ref,
                  cosq_ref, sinq_ref, cosk_ref, sink_ref, q_ref, lat_ref, kr_ref, *, tb):
    lane = _iota((tb, LANES), 1)
    ql = pm_ref[0, :, 0:_MLA_C0]
    ql = ql * lax.rsqrt(jnp.mean(ql * ql, axis=-1, keepdims=True) + EPS) * qag_ref[...]
    q = jnp.dot(_bf(ql), wuq_ref[...], preferred_element_type=F32)
    nope = lane < MLA_NOPE
    rope = (lane >= MLA_NOPE) & (lane < MLA_QK)
    cosq = cosq_ref[...]
    sinq = sinq_ref[...]
    for h in range(MLA_HEADS):
        qh = q[:, _MLA_HW * h:_MLA_HW * (h + 1)]
        sq = qh * qh
        ss_n = jnp.sum(jnp.where(nope, sq, 0.0), axis=-1, keepdims=True) * (1.0 / MLA_NOPE)
        ss_r = jnp.sum(jnp.where(rope, sq, 0.0), axis=-1, keepdims=True) * (1.0 / MLA_ROPE)
        rinv = jnp.where(nope, lax.rsqrt(ss_n + EPS), lax.rsqrt(ss_r + EPS))
        qn = qh * rinv * qgain_ref[...]
        sw = _swap_halves(qn, lane, MLA_NOPE, MLA_ROPE // 2)
        qr = jnp.where(nope, qn, qn * cosq + sw * sinq)
        q_ref[0, :, _MLA_HW * h:_MLA_HW * (h + 1)] = _bf(qr * _Q_SCALE)
    ckv = pm_ref[0, :, _MLA_C0:_MLA_R0]
    lat_ref[0] = ckv * lax.rsqrt(jnp.mean(ckv * ckv, axis=-1, keepdims=True) + EPS) * kvag_ref[...]
    kx = pm_ref[0, :, _MLA_R0:_MLA_COLS]
    ssk = jnp.sum(kx * kx, axis=-1, keepdims=True) * (1.0 / MLA_ROPE)
    kn = kx * lax.rsqrt(ssk + EPS) * krgain_ref[...]
    swk = _swap_halves(kn, lane, 0, MLA_ROPE // 2)
    kr = kn * cosk_ref[...] + swk * sink_ref[...]
    kr_ref[0] = kr[:, 0:MLA_ROPE]


def _mla_q_call(pm, qag, wuq, qgain, kvag, krgain, cosq, sinq, cosk, sink):
    B, L, _ = pm.shape
    tb = min(L, 512)
    full = lambda shape: pl.BlockSpec(shape, lambda b, i: (0,) * len(shape))
    tab = pl.BlockSpec((tb, LANES), lambda b, i: (i, 0))
    nq = MLA_HEADS * _MLA_HW
    return pl.pallas_call(
        functools.partial(_mla_q_kernel, tb=tb),
        out_shape=(jax.ShapeDtypeStruct((B, L, nq), BF16),
                   jax.ShapeDtypeStruct((B, L, MLA_KV_LORA), F32),
                   jax.ShapeDtypeStruct((B, L, MLA_ROPE), F32)),
        grid=(B, L // tb),
        in_specs=[pl.BlockSpec((1, tb, _MLA_COLS), lambda b, i: (b, i, 0)),
                  full((1, MLA_Q_LORA)), full((MLA_Q_LORA, nq)), full((1, LANES)),
                  full((1, MLA_KV_LORA)), full((1, LANES)), tab, tab, tab, tab],
        out_specs=(pl.BlockSpec((1, tb, nq), lambda b, i: (b, i, 0)),
                   pl.BlockSpec((1, tb, MLA_KV_LORA), lambda b, i: (b, i, 0)),
                   pl.BlockSpec((1, tb, MLA_ROPE), lambda b, i: (b, i, 0))),
        compiler_params=_cparams(("arbitrary", "arbitrary")),
        name="mla_q_proj",
    )(pm, qag, wuq, qgain, kvag, krgain, cosq, sinq, cosk, sink)


def _mla_kv_kernel(ckv_ref, krp_ref, wukv_ref, kgain_ref, k_ref, kv_ref, *, tb):
    lane = _iota((tb, LANES), 1)
    nope = lane < MLA_NOPE
    kv = jnp.dot(_bf(ckv_ref[0]), wukv_ref[...], preferred_element_type=F32)
    krp = krp_ref[0]
    for h in range(MLA_HEADS):
        g = kv[:, _MLA_HW * h:_MLA_HW * (h + 1)]
        ss = jnp.sum(jnp.where(nope, g * g, 0.0), axis=-1, keepdims=True) * (1.0 / MLA_NOPE)
        k_ref[0, :, _MLA_HW * h:_MLA_HW * (h + 1)] = _bf(g * lax.rsqrt(ss + EPS) * kgain_ref[...] + krp)
    kv_ref[0] = _bf(kv)


def _mla_kv_call(ckv_all, krp_all, wukv, kgain):
    B, Lk, _ = ckv_all.shape
    tb = 512 if Lk % 512 == 0 else Lk
    nk = MLA_HEADS * _MLA_HW
    full = lambda shape: pl.BlockSpec(shape, lambda b, i: (0,) * len(shape))
    return pl.pallas_call(
        functools.partial(_mla_kv_kernel, tb=tb),
        out_shape=(jax.ShapeDtypeStruct((B, Lk, nk), BF16), jax.ShapeDtypeStruct((B, Lk, nk), BF16)),
        grid=(B, Lk // tb),
        in_specs=[pl.BlockSpec((1, tb, MLA_KV_LORA), lambda b, i: (b, i, 0)),
                  pl.BlockSpec((1, tb, LANES), lambda b, i: (b, i, 0)),
                  full((MLA_KV_LORA, nk)), full((1, LANES))],
        out_specs=(pl.BlockSpec((1, tb, nk), lambda b, i: (b, i, 0)),
                   pl.BlockSpec((1, tb, nk), lambda b, i: (b, i, 0))),
        compiler_params=_cparams(("arbitrary", "arbitrary"), VMEM_LIMIT),
        name="mla_kv_proj",
    )(ckv_all, krp_all, wukv, kgain)


def _attn_kernel(q_ref, k_ref, kv_ref, o_ref, m_ref, l_ref, acc_ref, *, tq, tk, nk, causal):
    qi = pl.program_id(2)
    ki = pl.program_id(3)
    last = _last_kv_block(qi, tq, tk) if causal else nk - 1
    first_diag = (qi * tq) // tk

    @pl.when(ki == 0)
    def _():
        m_ref[...] = jnp.full(m_ref.shape, NEG, F32)
        l_ref[...] = jnp.zeros(l_ref.shape, F32)
        acc_ref[...] = jnp.zeros(acc_ref.shape, F32)

    if tk % LANES == 0:
        widen = lambda v: jnp.concatenate([v] * (tk // LANES), axis=1)
    else:
        widen = lambda v: v[:, 0:1]

    def step(masked):
        if masked:
            qc = (qi * tq + _iota((tq, tk), 0)) >> 6
            kc = (ki * tk + _iota((tq, tk), 1)) >> 6
            keep = kc <= qc
        H = range(2)
        hs = [slice(_MLA_HW * hh, _MLA_HW * (hh + 1)) for hh in H]
        sc = [lax.dot_general(q_ref[0, :, hs[hh]], k_ref[0, :, hs[hh]], (((1,), (1,)), ((), ())),
                              preferred_element_type=F32) for hh in H]
        if masked:
            sc = [jnp.where(keep, t, NEG) for t in sc]
        m_prev = [m_ref[hh] for hh in H]
        m_new = [jnp.maximum(m_prev[hh], jnp.max(sc[hh], axis=-1, keepdims=True)) for hh in H]
        p = [jnp.exp2(sc[hh] - widen(m_new[hh])) for hh in H]
        alpha = [jnp.exp2(m_prev[hh] - m_new[hh]) for hh in H]
        pv = [jnp.dot(_bf(p[hh]), kv_ref[0, :, hs[hh]], preferred_element_type=F32) for hh in H]
        for hh in H:
            l_ref[hh] = alpha[hh] * l_ref[hh] + jnp.sum(p[hh], axis=-1, keepdims=True)
            acc_ref[hh] = alpha[hh] * acc_ref[hh] + pv[hh]
            m_ref[hh] = m_new[hh]

    if causal:
        pl.when(ki < first_diag)(lambda: step(False))
        pl.when((ki >= first_diag) & (ki <= last))(lambda: step(True))
    else:
        step(False)

    @pl.when(ki == last)
    def _():
        lane = _iota((tq, LANES), 1)
        o_e = pltpu.roll(acc_ref[0] / l_ref[0], MLA_NOPE, 1)
        o_o = acc_ref[1] / l_ref[1]
        o_ref[0] = jnp.where(lane < MLA_NOPE, o_e, o_o)


def _last_kv_block(qi, tq, tk):
    return ((qi + 1) * tq - 1) // tk


def _attn_call(q, k, kv, causal):
    B, Lq, _ = q.shape
    Lk = k.shape[1]
    if causal:
        tq = min(Lq, 512)
        tk = min(Lk, 1024)
    else:
        tq, tk = Lq, Lk
    nq, nk = Lq // tq, Lk // tk
    w2 = 2 * _MLA_HW
    if causal:
        kmap = lambda b, hp, qi, ki: (b, jnp.minimum(ki, _last_kv_block(qi, tq, tk)), hp)
    else:
        kmap = lambda b, hp, qi, ki: (b, ki, hp)
    return pl.pallas_call(
        functools.partial(_attn_kernel, tq=tq, tk=tk, nk=nk, causal=causal),
        out_shape=jax.ShapeDtypeStruct((B, Lq, MLA_HEADS * MLA_NOPE), F32),
        grid=(B, MLA_HEADS // 2, nq, nk),
        in_specs=[pl.BlockSpec((1, tq, w2), lambda b, hp, qi, ki: (b, qi, hp)),
                  pl.BlockSpec((1, tk, w2), kmap),
                  pl.BlockSpec((1, tk, w2), kmap)],
        out_specs=pl.BlockSpec((1, tq, LANES), lambda b, hp, qi, ki: (b, qi, hp)),
        scratch_shapes=[pltpu.VMEM((2, tq, LANES), F32), pltpu.VMEM((2, tq, LANES), F32),
                        pltpu.VMEM((2, tq, LANES), F32)],
        compiler_params=_cparams(("arbitrary",) * 4, VMEM_LIMIT),
        name="mla_attention",
    )(q, k, kv)


def _outproj_kernel(a_ref, b_ref, c_ref, x_ref, g1_ref, sh_ref, sc_ref, ng_ref, wa_ref, wb_ref, wc_ref,
                    x1_ref, h2_ref):
    y = (jnp.dot(_bf(a_ref[0]), wa_ref[...], preferred_element_type=F32)
         + jnp.dot(_bf(b_ref[0]), wb_ref[...], preferred_element_type=F32)
         + jnp.dot(_bf(c_ref[0]), wc_ref[...], preferred_element_type=F32))
    x1 = x_ref[0] + g1_ref[0] * y
    x1_ref[0] = x1
    n = x1 * lax.rsqrt(jnp.mean(x1 * x1, axis=-1, keepdims=True) + EPS) * ng_ref[...]
    h2_ref[0] = n * (1.0 + sc_ref[0]) + sh_ref[0]


def _outproj_call(oa, ob, oc, x, g1, sh2, sc2, ng, wa, wb, wc):
    B, L, _ = x.shape
    tb = min(L, 512)
    full = lambda shape: pl.BlockSpec(shape, lambda b, i: (0,) * len(shape))
    per_b = pl.BlockSpec((1, 1, D_MODEL), lambda b, i: (b, 0, 0))
    blk = lambda wdt: pl.BlockSpec((1, tb, wdt), lambda b, i: (b, i, 0))
    return pl.pallas_call(
        _outproj_kernel,
        out_shape=(jax.ShapeDtypeStruct((B, L, D_MODEL), F32), jax.ShapeDtypeStruct((B, L, D_MODEL), F32)),
        grid=(B, L // tb),
        in_specs=[blk(GDN_WIDTH), blk(MLA_HEADS * MLA_NOPE), blk(MLSTM_WIDTH), blk(D_MODEL),
                  per_b, per_b, per_b, full((1, D_MODEL)), full(wa.shape), full(wb.shape), full(wc.shape)],
        out_specs=(blk(D_MODEL), blk(D_MODEL)),
        compiler_params=_cparams(("arbitrary", "arbitrary"), VMEM_LIMIT),
        name="out_proj",
    )(oa, ob, oc, x, g1, sh2, sc2, ng, wa, wb, wc)


def _topk_rows(svs, io, k, payloads=None):
    svs = list(svs)
    vals = [[] for _ in svs]
    outs = [[] for _ in svs]
    big = jnp.float32(1e9)
    for _ in range(k):
        for a in range(len(svs)):
            m = jnp.max(svs[a], axis=0, keepdims=True)
            ix = jnp.min(jnp.where(svs[a] == m, io, big), axis=0, keepdims=True)
            hit = io == ix
            svs[a] = jnp.where(hit, -jnp.inf, svs[a])
            vals[a].append(m)
            outs[a].append(ix if payloads is None
                           else jnp.max(jnp.where(hit, payloads[a], -1.0), axis=0, keepdims=True))
    return [(jnp.concatenate(v, axis=0), jnp.concatenate(o, axis=0)) for v, o in zip(vals, outs)]


_CAND_ROWS = PEER_TOPK + 7 * 8 + 8


def _cand_blocks(t1, t2, combine):
    blocks = [combine(t1[0:1], t2)]
    blocks += [combine(t1[a:a + 1], t2[0:8]) for a in range(1, 8)]
    blocks.append(combine(t1[8:PEER_TOPK], t2[0:1]))
    return jnp.concatenate(blocks, axis=0)


def _route_kernel(h_ref, wq_ref, keys_ref, e_ref, g_ref, qs_ref, es_ref, gs_ref, *, tb):
    q = jnp.dot(_bf(h_ref[...]), wq_ref[...], preferred_element_type=F32)
    for g in range(2 * PEER_HEADS):
        qs_ref[g] = _bf(q[:, LANES * g:LANES * (g + 1)])
    io_k = _iota((N_KEYS, tb), 0).astype(F32)
    r = _iota((_CAND_ROWS, tb), 0)
    mid = r - PEER_TOPK
    io_c = jnp.where(r < PEER_TOPK, r,
                     jnp.where(r < _CAND_ROWS - 8, ((mid >> 3) + 1) * PEER_TOPK + (mid & 7),
                               (r - (_CAND_ROWS - 16)) * PEER_TOPK)).astype(F32)

    def head(h, carry):
        scores = [lax.dot_general(keys_ref[2 * h + p], qs_ref[2 * h + p], (((1,), (1,)), ((), ())),
                                  preferred_element_type=F32) for p in range(2)]
        (v1, i1), (v2, i2) = _topk_rows(scores, io_k, PEER_TOPK)
        cand = _cand_blocks(v1, v2, lambda x, y: x + y)
        expert = _cand_blocks(i1, i2, lambda x, y: x * N_KEYS + y)
        (sc, e), = _topk_rows([cand], io_c, PEER_TOPK, payloads=[expert])
        ex = jnp.exp(sc - jnp.max(sc, axis=0, keepdims=True))
        es_ref[h] = e * _ROW_SUB
        gs_ref[h] = ex / jnp.sum(ex, axis=0, keepdims=True)
        return carry

    lax.fori_loop(0, PEER_HEADS, head, 0)
    e_ref[...] = es_ref[...].reshape(PEER_SLOTS, tb).T.astype(jnp.int32)
    g_ref[...] = gs_ref[...].reshape(PEER_SLOTS, tb).T


def _route_call(h2, wq, keys):
    T = h2.shape[0]
    tb = min(T, 256)
    full = lambda shape: pl.BlockSpec(shape, lambda i: (0,) * len(shape))
    return pl.pallas_call(
        functools.partial(_route_kernel, tb=tb),
        out_shape=(jax.ShapeDtypeStruct((T, PEER_SLOTS), jnp.int32), jax.ShapeDtypeStruct((T, PEER_SLOTS), F32)),
        grid=(T // tb,),
        in_specs=[pl.BlockSpec((tb, D_MODEL), lambda i: (i, 0)), full(wq.shape), full(keys.shape)],
        out_specs=(pl.BlockSpec((tb, PEER_SLOTS), lambda i: (i, 0)), pl.BlockSpec((tb, PEER_SLOTS), lambda i: (i, 0))),
        scratch_shapes=[pltpu.VMEM((2 * PEER_HEADS, tb, LANES), BF16),
                        pltpu.VMEM((PEER_HEADS, PEER_TOPK, tb), F32),
                        pltpu.VMEM((PEER_HEADS, PEER_TOPK, tb), F32)],
        compiler_params=_cparams(("arbitrary",), VMEM_LIMIT),
        name="peer_route",
    )(h2, wq, keys)


_ROW_SUB = 4
_GROUP = 8
_BITREV3 = (0, 4, 2, 6, 1, 5, 3, 7)


def _expert_row(tab_ref, row0):
    return pltpu.bitcast(tab_ref[pl.ds(row0, _ROW_SUB), :], BF16).astype(F32)


def _gelu(x):
    return 0.5 * x * (1.0 + lax.erf(x * (2.0 ** -0.5)))


def _fold(a, b, h, sub):
    m = (sub & h) == 0
    if h == 4:
        return jnp.where(m, a, b) + pltpu.roll(jnp.where(m, b, a), 4, 0)
    return jnp.where(m, a + pltpu.roll(a, 8 - h, 0), b + pltpu.roll(b, h, 0))


def _transpose8(v, sub):
    v = list(v)
    for h in (4, 2, 1):
        m = (sub & h) == 0
        for i in range(8):
            if i & h == 0:
                a, b = v[i], v[i | h]
                v[i] = jnp.where(m, a, pltpu.roll(b, h, 0))
                v[i | h] = jnp.where(m, pltpu.roll(a, 8 - h, 0), b)
    return v


def _rows_to_dense(x_ref, dense_ref, tb, sub):
    for g in range(tb // 8):
        cols = [x_ref[8 * g:8 * (g + 1), LANES * c:LANES * (c + 1)] for c in range(8)]
        for k, d in enumerate(_transpose8(cols, sub)):
            dense_ref[8 * g + k] = d


def _dense_to_rows(dense_ref, o_ref, tb, sub):
    for g in range(tb // 8):
        cols = _transpose8([dense_ref[8 * g + k] for k in range(8)], sub)
        for c in range(8):
            o_ref[8 * g:8 * (g + 1), LANES * c:LANES * (c + 1)] = cols[c]


def _peer_u_kernel(idx_ref, xrow_ref, g_ref, tab_ref, c_ref, part_ref, x_ref, *, tb):
    ones8 = jnp.ones((8, LANES), BF16)
    sub = _iota((8, LANES), 0)
    nt = lambda a, b: lax.dot_general(a, b, (((1,), (1,)), ((), ())), preferred_element_type=F32)
    _rows_to_dense(xrow_ref, x_ref, tb, sub)

    def gather(t, slot, k):
        xv = x_ref[t]
        buf = part_ref.at[slot]
        for g in range(PEER_SLOTS // _GROUP):
            p = [_expert_row(tab_ref, idx_ref[t, _GROUP * g + _BITREV3[i]]) * xv for i in range(_GROUP)]
            z = [_fold(p[2 * i], p[2 * i + 1], 4, sub) for i in range(4)]
            w = [_fold(z[0], z[1], 2, sub), _fold(z[2], z[3], 2, sub)]
            r0 = PEER_SLOTS * k + _GROUP * g
            buf[r0:r0 + _GROUP, :] = _fold(w[0], w[1], 1, sub)

    def finish(grp, slot):
        res = nt(ones8, _bf(part_ref[slot]))
        act = res[:, 0:LANES]
        for k in range(1, 8):
            act = jnp.where(sub == k, res[:, LANES * k:LANES * (k + 1)], act)
        rows = pl.ds(pl.multiple_of(grp * 8, 8), 8)
        c_ref[rows, :] = g_ref[rows, :] * _gelu(act)

    for k in range(8):
        gather(k, 0, k)

    def trip(grp, carry):
        slot = grp & 1
        finish(grp - 1, 1 - slot)
        for k in range(8):
            gather(grp * 8 + k, slot, k)
        return carry

    n_grp = tb // 8
    lax.fori_loop(1, n_grp, trip, 0)
    finish(n_grp - 1, (n_grp - 1) & 1)


def _peer_v_kernel(idx_ref, c_ref, xrow_ref, g2_ref, tab_ref, o_ref, m_ref, x_ref, *, tb):
    sub = _iota((8, LANES), 0)
    _rows_to_dense(xrow_ref, x_ref, tb, sub)
    eye = _iota((PEER_SLOTS, LANES), 0) == _iota((PEER_SLOTS, LANES), 1)
    diag = jnp.where(eye[None], c_ref[...][:, None, :], 0.0).reshape(tb * PEER_SLOTS, LANES)
    rep = jnp.dot(_bf(diag), jnp.ones((LANES, LANES), BF16), preferred_element_type=F32)
    m_ref[...] = rep.reshape(tb, PEER_SLOTS, LANES)

    def tok(t):
        y = jnp.zeros((8, LANES), F32)
        base = t * PEER_SLOTS
        for g in range(PEER_SLOTS // _GROUP):
            rows = [idx_ref[base + k] for k in range(_GROUP)]
            r = [m_ref[t, pl.ds(_GROUP * g + k, 1), :] * _expert_row(tab_ref, rows[k]) for k in range(_GROUP)]
            y = y + (((r[0] + r[1]) + (r[2] + r[3])) + ((r[4] + r[5]) + (r[6] + r[7])))
            base = base + _GROUP + lax.shift_right_arithmetic(rows[0], jnp.int32(31))
        x_ref[t] = x_ref[t] + g2_ref[0] * y

    per_trip = 2

    def trip(i, carry):
        for k in range(per_trip):
            tok(i * per_trip + k)
        return carry

    lax.fori_loop(0, tb // per_trip, trip, 0)
    _dense_to_rows(x_ref, o_ref, tb, sub)


def _table_spec():
    return pl.BlockSpec((N_EXPERTS * _ROW_SUB, LANES), lambda i: (0, 0), pipeline_mode=pl.Buffered(1))


def _peer_u_call(idx, x, gate, tab, tb):
    T = idx.shape[0]
    smem = pl.BlockSpec((tb, PEER_SLOTS), lambda i: (i, 0), memory_space=pltpu.SMEM)
    return pl.pallas_call(
        functools.partial(_peer_u_kernel, tb=tb),
        out_shape=jax.ShapeDtypeStruct((T, PEER_SLOTS), F32),
        grid=(T // tb,),
        in_specs=[smem, pl.BlockSpec((tb, D_MODEL), lambda i: (i, 0)),
                  pl.BlockSpec((tb, PEER_SLOTS), lambda i: (i, 0)), _table_spec()],
        out_specs=pl.BlockSpec((tb, PEER_SLOTS), lambda i: (i, 0)),
        scratch_shapes=[pltpu.VMEM((2, 8 * PEER_SLOTS, LANES), F32), pltpu.VMEM((tb, 8, LANES), F32)],
        compiler_params=_cparams(("arbitrary",), VMEM_LIMIT),
        name="peer_u",
    )(idx, x, gate, tab)


def _peer_v_call(idx, coef, x, g2, tab, tb, L):
    T = idx.shape[0]
    smem = pl.BlockSpec((tb * PEER_SLOTS,), lambda i: (i,), memory_space=pltpu.SMEM)
    per_step = L // tb
    idx = idx.reshape(T * PEER_SLOTS)
    return pl.pallas_call(
        functools.partial(_peer_v_kernel, tb=tb),
        out_shape=jax.ShapeDtypeStruct((T, D_MODEL), F32),
        grid=(T // tb,),
        in_specs=[smem, pl.BlockSpec((tb, PEER_SLOTS), lambda i: (i, 0)),
                  pl.BlockSpec((tb, D_MODEL), lambda i: (i, 0)),
                  pl.BlockSpec((1, 8, LANES), lambda i: (i // per_step, 0, 0)), _table_spec()],
        out_specs=pl.BlockSpec((tb, D_MODEL), lambda i: (i, 0)),
        scratch_shapes=[pltpu.VMEM((tb, PEER_SLOTS, LANES), F32), pltpu.VMEM((tb, 8, LANES), F32)],
        compiler_params=_cparams(("arbitrary",), VMEM_LIMIT),
        name="peer_v",
    )(idx, coef, x, g2, tab)


def _pack_table(tab):
    bits = lax.bitcast_convert_type(tab.astype(BF16), jnp.uint16).astype(jnp.uint32)
    bits = bits.reshape(tab.shape[0], _ROW_SUB, 2, LANES)
    return (bits[:, :, 0, :] | (bits[:, :, 1, :] << 16)).reshape(tab.shape[0] * _ROW_SUB, LANES)


def _rep_heads(w, width=HEAD_W):
    return jnp.repeat(w, width, axis=-1)


def _to_bd(s):
    B, H = s.shape[:2]
    s = s.reshape(B, H // 2, 2, HEAD_W, HEAD_W)
    z = jnp.zeros_like(s[:, :, 0])
    top = jnp.concatenate([s[:, :, 0], z], axis=-1)
    bot = jnp.concatenate([z, s[:, :, 1]], axis=-1)
    return jnp.concatenate([top, bot], axis=-2)


def _from_bd(s):
    B, P = s.shape[:2]
    return jnp.stack([s[:, :, :HEAD_W, :HEAD_W], s[:, :, HEAD_W:, HEAD_W:]], axis=2).reshape(B, 2 * P, HEAD_W, HEAD_W)


def _prep_layer(l, w):
    o = [0]
    for sz in (GDN_QKV, GDN_WIDTH, GDN_HEADS, GDN_HEADS, MLA_Q_LORA, MLA_KV_LORA, MLA_ROPE,
               MLSTM_QKV, MLSTM_WIDTH, MLSTM_HEADS, MLSTM_HEADS):
        o.append(o[-1] + sz)
    wi = w['w_in'][l]
    col = lambda i: wi[:, o[i]:o[i + 1]]
    p = {}
    p['w_gdn'] = _bf(jnp.concatenate([col(0), col(1), _rep_heads(col(2)), _rep_heads(col(3))], axis=1))
    p['w_mla'] = _bf(jnp.concatenate([col(4), col(5), col(6), jnp.zeros((D_MODEL, LANES - MLA_ROPE), F32)], axis=1))
    p['w_mls'] = _bf(jnp.concatenate([col(7), col(8), _rep_heads(col(9)), _rep_heads(col(10))], axis=1))
    p['alog'] = _rep_heads(w['gdn_a_log'][l]).reshape(1, GDN_WIDTH)
    p['dtb'] = _rep_heads(w['gdn_dt_bias'][l]).reshape(1, GDN_WIDTH)
    p['gdn_og'] = jnp.tile(w['gdn_out_g'][l], 2).reshape(1, LANES)
    p['ib'] = _rep_heads(w['mlstm_i_bias'][l]).reshape(1, MLSTM_WIDTH)
    p['fb'] = _rep_heads(w['mlstm_f_bias'][l]).reshape(1, MLSTM_WIDTH)
    p['mls_og'] = w['mlstm_out_g'][l].reshape(1, MLSTM_WIDTH)
    wuq = w['mla_w_uq'][l].reshape(MLA_Q_LORA, MLA_HEADS, MLA_QK)
    p['wuq'] = _bf(jnp.pad(wuq, ((0, 0), (0, 0), (0, _MLA_HW - MLA_QK))).reshape(MLA_Q_LORA, MLA_HEADS * _MLA_HW))
    p['wukv'] = _bf(w['mla_w_ukv'][l])
    p['qag'] = w['mla_q_a_g'][l].reshape(1, MLA_Q_LORA)
    p['kvag'] = w['mla_kv_a_g'][l].reshape(1, MLA_KV_LORA)
    p['qgain'] = jnp.pad(w['mla_q_gain'][l], (0, LANES - MLA_QK)).reshape(1, LANES)
    kg = w['mla_k_gain'][l]
    p['kgain'] = jnp.pad(kg[:MLA_NOPE], (0, LANES - MLA_NOPE)).reshape(1, LANES)
    p['krgain'] = jnp.pad(kg[MLA_NOPE:], (0, LANES - MLA_ROPE)).reshape(1, LANES)
    wo = w['w_out'][l]
    p['wo_a'] = _bf(wo[0:GDN_WIDTH])
    p['wo_b'] = _bf(wo[GDN_WIDTH:2 * GDN_WIDTH])
    p['wo_c'] = _bf(wo[2 * GDN_WIDTH:])
    p['wq'] = _bf(w['peer_w_q'][l])
    p['keys'] = _bf(w['peer_sub_keys'][l].reshape(2 * PEER_HEADS, N_KEYS, LANES))
    p['u_tab'] = _pack_table(w['peer_u'][l])
    p['v_tab'] = _pack_table(w['peer_v'][l])
    p['conv_w'] = w['gdn_conv_w'][l]
    p['norm_attn_g'] = w['norm_attn_g'][l].reshape(1, D_MODEL)
    p['norm_ffn_g'] = w['norm_ffn_g'][l].reshape(1, D_MODEL)
    return p


def _rope_tables(pos):
    half = MLA_ROPE // 2
    inv = ROPE_THETA ** (-jnp.arange(half, dtype=F32) / half)
    ang = pos.astype(F32)[:, None] * inv[None, :]
    cos, sin = jnp.cos(ang), jnp.sin(ang)
    n = pos.shape[0]
    c2 = jnp.concatenate([cos, cos], axis=1)
    s2 = jnp.concatenate([-sin, sin], axis=1)
    padq = lambda t: jnp.pad(t, ((0, 0), (MLA_NOPE, LANES - MLA_QK)))
    padk = lambda t: jnp.pad(t, ((0, 0), (0, LANES - MLA_ROPE)))
    return padq(c2), padq(s2), padk(c2), padk(s2)


def _pad_rows(a, n):
    return jnp.pad(a, ((0, 0), (0, n - a.shape[1]), (0, 0)))


def _layer(x, mod, p, st, rope, prompt):
    B, L, _ = x.shape
    sh1, sc1, g1, sh2, sc2, g2 = [m.reshape(B, 1, D_MODEL) for m in jnp.split(mod, 6, axis=-1)]
    pg, pm, pl_ = _inproj_call(x, sh1, sc1, p['norm_attn_g'], p['w_gdn'], p['w_mla'], p['w_mls'])
    Lp = -(-L // CHUNK) * CHUNK
    if Lp != L:
        pg, pl_ = _pad_rows(pg, Lp), _pad_rows(pl_, Lp)
    o_a, gdn_bd, conv_new = _gdn_call(pg, st['gdn_conv'], _to_bd(st['gdn']), p['conv_w'],
                                      p['alog'], p['dtb'], p['gdn_og'], L)
    o_c, mc_bd, mn_p, mm_p = _mlstm_call(
        pl_, _to_bd(st['mlstm_c']), st['mlstm_n'].reshape(B, MLSTM_HEADS // 2, 1, LANES),
        _rep_heads(st['mlstm_m']).reshape(B, MLSTM_HEADS // 2, 1, LANES), p['ib'], p['fb'], p['mls_og'], L)
    o_a, o_c = o_a[:, :L], o_c[:, :L]
    q, ckv, kr = _mla_q_call(pm, p['qag'], p['wuq'], p['qgain'], p['kvag'], p['krgain'], *rope)
    krp = lambda t: jnp.pad(t, ((0, 0), (0, 0), (MLA_NOPE, LANES - MLA_QK)))
    k, kv = _mla_kv_call(ckv, krp(kr), p['wukv'], p['kgain'])
    if not prompt:
        k_c, kv_c = _mla_kv_call(st['mla_latent'], krp(st['mla_krope']), p['wukv'], p['kgain'])
        k = jnp.concatenate([k_c, k], axis=1)
        kv = jnp.concatenate([kv_c, kv], axis=1)
    o_b = _attn_call(q, k, kv, prompt)
    x1, h2 = _outproj_call(o_a, o_b, o_c, x, g1, sh2, sc2, p['norm_ffn_g'], p['wo_a'], p['wo_b'], p['wo_c'])
    T = B * L
    h2f = h2.reshape(T, D_MODEL)
    idx, gate = _route_call(h2f, p['wq'], p['keys'])
    tb = min(L, 128)
    coef = _peer_u_call(idx, h2f, gate, p['u_tab'], tb)
    x2 = _peer_v_call(idx, coef, x1.reshape(T, D_MODEL), g2.reshape(B, 8, LANES), p['v_tab'], tb, L)
    new_state = (ckv, kr, _from_bd(gdn_bd), conv_new, _from_bd(mc_bd),
                 mn_p.reshape(B, MLSTM_HEADS, HEAD_W), mm_p.reshape(B, MLSTM_HEADS, HEAD_W)[:, :, 0])
    return x2.reshape(B, L, D_MODEL), new_state


def kernel(x_prompt, x_sample, c_prompt, c_sample, cache_mla_latent, cache_mla_krope, state_gdn, state_gdn_conv, state_mlstm_c, state_mlstm_n, state_mlstm_m, ada_w, ada_b, norm_attn_g, norm_ffn_g, w_in, gdn_conv_w, gdn_a_log, gdn_dt_bias, gdn_out_g, mla_q_a_g, mla_w_uq, mla_kv_a_g, mla_w_ukv, mla_q_gain, mla_k_gain, mlstm_i_bias, mlstm_f_bias, mlstm_out_g, w_out, peer_w_q, peer_sub_keys, peer_u, peer_v):
    w = dict(ada_w=ada_w, ada_b=ada_b, norm_attn_g=norm_attn_g, norm_ffn_g=norm_ffn_g, w_in=w_in,
             gdn_conv_w=gdn_conv_w, gdn_a_log=gdn_a_log, gdn_dt_bias=gdn_dt_bias, gdn_out_g=gdn_out_g,
             mla_q_a_g=mla_q_a_g, mla_w_uq=mla_w_uq, mla_kv_a_g=mla_kv_a_g, mla_w_ukv=mla_w_ukv,
             mla_q_gain=mla_q_gain, mla_k_gain=mla_k_gain, mlstm_i_bias=mlstm_i_bias,
             mlstm_f_bias=mlstm_f_bias, mlstm_out_g=mlstm_out_g, w_out=w_out, peer_w_q=peer_w_q,
             peer_sub_keys=peer_sub_keys, peer_u=peer_u, peer_v=peer_v)
    B, Lp, _ = x_prompt.shape
    Bs, Ls, _ = x_sample.shape
    past = cache_mla_latent.shape[2]
    rope_p = _rope_tables(jnp.arange(Lp, dtype=jnp.int32))
    rope_s = _rope_tables(past + jnp.arange(Ls, dtype=jnp.int32))
    mods = _mod_call(jnp.concatenate([c_prompt, c_sample], axis=0), ada_w, ada_b)
    xp, xs = x_prompt, x_sample
    new_p, new_s = [], []
    for l in range(DEPTH):
        p = _prep_layer(l, w)
        st_p = {
            'gdn': jnp.zeros((B, GDN_HEADS, HEAD_W, HEAD_W), F32),
            'gdn_conv': jnp.zeros((B, CONV_W - 1, GDN_QKV), F32),
            'mlstm_c': jnp.zeros((B, MLSTM_HEADS, HEAD_W, HEAD_W), F32),
            'mlstm_n': jnp.zeros((B, MLSTM_HEADS, HEAD_W), F32),
            'mlstm_m': jnp.zeros((B, MLSTM_HEADS), F32),
        }
        st_s = {
            'mla_latent': cache_mla_latent[l], 'mla_krope': cache_mla_krope[l],
            'gdn': state_gdn[l], 'gdn_conv': state_gdn_conv[l],
            'mlstm_c': state_mlstm_c[l], 'mlstm_n': state_mlstm_n[l], 'mlstm_m': state_mlstm_m[l],
        }
        xp, sp = _layer(xp, mods[l, :B], p, st_p, rope_p, True)
        xs, ss = _layer(xs, mods[l, B:], p, st_s, rope_s, False)
        new_p.append(sp)
        new_s.append(ss)
    outs_p = [jnp.stack([s[i] for s in new_p]) for i in range(7)]
    outs_s = [jnp.stack([s[i] for s in new_s]) for i in range(7)]
    return (xp, xs, *outs_p, *outs_s)
```

```python
import functools
import math

import jax
import jax.numpy as jnp
from jax import lax
from jax.experimental import pallas as pl
from jax.experimental.pallas import tpu as pltpu

F32 = jnp.float32
BF16 = jnp.bfloat16

D_MODEL = 1024
DEPTH = 2
CHUNK = 64
EPS = 1e-6
GDN_HEADS = 6
GDN_DK = 64
GDN_WIDTH = 384
GDN_QKV = 1152
CONV_W = 4
MLA_HEADS = 6
MLA_NOPE = 64
MLA_ROPE = 32
MLA_QK = 96
MLA_Q_LORA = 384
MLA_KV_LORA = 256
ROPE_THETA = 10000.0
MLSTM_HEADS = 4
MLSTM_WIDTH = 256
MLSTM_QKV = 768
PEER_HEADS = 8
N_KEYS = 128
N_EXPERTS = N_KEYS * N_KEYS
PEER_TOPK = 16
PEER_SLOTS = PEER_HEADS * PEER_TOPK

HEAD_W = 64
LANES = 128
NEG = -1e30
VMEM_LIMIT = 56 * 1024 * 1024


def _bf(x):
    return x.astype(BF16)


def _mm(a, b):
    return jnp.dot(_bf(a), _bf(b), preferred_element_type=F32)


def _mm_nt(a, b):
    return lax.dot_general(_bf(a), _bf(b), (((1,), (1,)), ((), ())), preferred_element_type=F32)


def _split3(x):
    hi = _bf(x)
    r = x - hi.astype(F32)
    mid = _bf(r)
    lo = _bf(r - mid.astype(F32))
    return hi, mid, lo


def _mm_sel_exact(sel_bf, x):
    hi, mid, lo = _split3(x)
    d = lambda t: jnp.dot(sel_bf, t, preferred_element_type=F32)
    return d(hi) + d(mid) + d(lo)


def _sigmoid(x):
    return 1.0 / (1.0 + jnp.exp(-x))


def _silu(x):
    return x * _sigmoid(x)


def _softplus(x):
    return jnp.maximum(x, 0.0) + jnp.log1p(jnp.exp(-jnp.abs(x)))


def _iota(shape, axis):
    return lax.broadcasted_iota(jnp.int32, shape, axis)


def _pair_consts(C):
    row = _iota((C, LANES), 0)
    lane = _iota((C, LANES), 1)
    s = lane & (HEAD_W - 1)
    lo_half = lane < HEAD_W
    r2 = _iota((LANES, LANES), 0)
    l2 = _iota((LANES, LANES), 1)
    same_half = (r2 >> 6) == (l2 >> 6)
    return row, lane, s, lo_half, same_half


def _bd(y, lo_half):
    z = jnp.zeros_like(y)
    return jnp.concatenate([jnp.where(lo_half, y, z), jnp.where(lo_half, z, y)], axis=0)


def _diag_row(x, row, s):
    return jnp.sum(jnp.where(row == s, x, 0.0), axis=0, keepdims=True)


def _cparams(sem, vmem=None):
    kw = dict(dimension_semantics=sem)
    if vmem is not None:
        kw["vmem_limit_bytes"] = vmem
    return pltpu.CompilerParams(**kw)


def _mod_kernel(c_ref, w_ref, b_ref, o_ref):
    c = c_ref[...]
    o_ref[0] = _mm(_silu(c), w_ref[0]) + b_ref[0]


def _mod_call(c_all, ada_w, ada_b):
    nb = c_all.shape[0]
    nj = 6
    return pl.pallas_call(
        _mod_kernel,
        out_shape=jax.ShapeDtypeStruct((DEPTH, nb, 6 * D_MODEL), F32),
        grid=(DEPTH, nj),
        in_specs=[
            pl.BlockSpec((nb, D_MODEL), lambda l, j: (0, 0)),
            pl.BlockSpec((1, D_MODEL, D_MODEL), lambda l, j: (l, 0, j)),
            pl.BlockSpec((1, 1, D_MODEL), lambda l, j: (l, 0, j)),
        ],
        out_specs=pl.BlockSpec((1, nb, D_MODEL), lambda l, j: (l, 0, j)),
        compiler_params=_cparams(("arbitrary", "arbitrary")),
        name="adaln_mod",
    )(c_all, ada_w, ada_b.reshape(DEPTH, 1, 6 * D_MODEL))


def _inproj_kernel(x_ref, sh_ref, sc_ref, g_ref, wg_ref, wm_ref, wl_ref, og_ref, om_ref, ol_ref):
    x = x_ref[0]
    y = x * lax.rsqrt(jnp.mean(x * x, axis=-1, keepdims=True) + EPS) * g_ref[...]
    h = _bf(y * (1.0 + sc_ref[0]) + sh_ref[0])
    og_ref[0] = jnp.dot(h, wg_ref[...], preferred_element_type=F32)
    om_ref[0] = jnp.dot(h, wm_ref[...], preferred_element_type=F32)
    ol_ref[0] = jnp.dot(h, wl_ref[...], preferred_element_type=F32)


def _inproj_call(x, sh, sc, g, wg, wm, wl):
    B, L, _ = x.shape
    tb = min(L, 512)
    ng, nm, nl = wg.shape[1], wm.shape[1], wl.shape[1]
    full = lambda shape: pl.BlockSpec(shape, lambda b, i: (0,) * len(shape))
    per_b = pl.BlockSpec((1, 1, D_MODEL), lambda b, i: (b, 0, 0))
    return pl.pallas_call(
        _inproj_kernel,
        out_shape=(jax.ShapeDtypeStruct((B, L, ng), F32),
                   jax.ShapeDtypeStruct((B, L, nm), F32),
                   jax.ShapeDtypeStruct((B, L, nl), F32)),
        grid=(B, L // tb),
        in_specs=[pl.BlockSpec((1, tb, D_MODEL), lambda b, i: (b, i, 0)), per_b, per_b,
                  full((1, D_MODEL)), full(wg.shape), full(wm.shape), full(wl.shape)],
        out_specs=(pl.BlockSpec((1, tb, ng), lambda b, i: (b, i, 0)),
                   pl.BlockSpec((1, tb, nm), lambda b, i: (b, i, 0)),
                   pl.BlockSpec((1, tb, nl), lambda b, i: (b, i, 0))),
        compiler_params=_cparams(("arbitrary", "arbitrary"), VMEM_LIMIT),
        name="in_proj",
    )(x, sh, sc, g, wg, wm, wl)


_GDN_Z0 = GDN_QKV
_GDN_B0 = GDN_QKV + GDN_WIDTH
_GDN_A0 = _GDN_B0 + GDN_WIDTH
_GDN_COLS = _GDN_A0 + GDN_WIDTH
_CONV_PAD = 8


def _gdn_kernel(pg_ref, conv0_ref, s0_ref, convw_ref, alog_ref, dtb_ref, og_ref,
                o_ref, sfin_ref, convfin_ref, xp_ref, s_ref, *, C, cps, n_valid, n_steps):
    n = pl.program_id(1)
    hist = CONV_W - 1
    R = C * cps

    @pl.when(n == 0)
    def _():
        xp_ref[0:_CONV_PAD, :] = jnp.zeros((_CONV_PAD, GDN_QKV), F32)
        xp_ref[_CONV_PAD - hist:_CONV_PAD, :] = conv0_ref[0]
        s_ref[...] = s0_ref[0]

    a_pre = pg_ref[0, :, 0:GDN_QKV]
    xp_ref[_CONV_PAD:_CONV_PAD + R, :] = a_pre
    w = convw_ref[...]
    y = w[CONV_W - 1:CONV_W] * a_pre
    for j in range(CONV_W - 1):
        y = y + w[j:j + 1] * xp_ref[_CONV_PAD - hist + j:_CONV_PAD - hist + j + R, :]
    y = _silu(y)

    row, lane, s, lo_half, same_half = _pair_consts(C)
    ones_blk = jnp.where(same_half, 1.0, 0.0).astype(BF16)
    tr, tc = _iota((R, R), 0), _iota((R, R), 1)
    tri = jnp.where((tc <= tr) & ((tc >> 6) == (tr >> 6)), 1.0, 0.0).astype(BF16)
    causal = s <= row
    strict = s < row
    eye = s == row

    def hsum(x):
        return jnp.dot(_bf(x), ones_blk, preferred_element_type=F32)

    beta_all = _sigmoid(pg_ref[0, :, _GDN_B0:_GDN_A0])
    gl_all = -jnp.exp(alog_ref[...]) * _softplus(pg_ref[0, :, _GDN_A0:_GDN_COLS] + dtb_ref[...])
    if n_valid < R * n_steps:
        valid = (_iota((R, GDN_WIDTH), 0) + n * R) < n_valid
        beta_all = jnp.where(valid, beta_all, 0.0)
        gl_all = jnp.where(valid, gl_all, 0.0)
    g_all = _mm_sel_exact(tri, gl_all)

    items = [(c, j) for c in range(cps) for j in range(GDN_HEADS // 2)]
    I = range(len(items))
    blk = lambda a, base, i: a[C * items[i][0]:C * (items[i][0] + 1),
                               base + LANES * items[i][1]:base + LANES * (items[i][1] + 1)]
    bd = lambda a: _bd(a, lo_half)
    qk_raw = [jnp.concatenate([blk(y, 0, i), blk(y, GDN_WIDTH, i)], axis=0) for i in I]
    ssq = [hsum(t * t) for t in qk_raw]
    q2 = [qk_raw[i][0:C] * lax.rsqrt(ssq[i][0:C] + EPS) * (GDN_DK ** -0.5) for i in I]
    k2 = [qk_raw[i][C:2 * C] * lax.rsqrt(ssq[i][C:2 * C] + EPS) for i in I]
    v2 = [blk(y, 2 * GDN_WIDTH, i) for i in I]
    beta = [blk(beta_all, 0, i) for i in I]
    G = [blk(g_all, 0, i) for i in I]
    eG = [jnp.exp(G[i]) for i in I]
    decay = []
    for i in I:
        diff = G[i] - _diag_row(G[i], row, s)
        decay.append(jnp.where(causal, jnp.exp(jnp.where(causal, diff, 0.0)), 0.0))
    kbd = [bd(k2[i]) for i in I]
    kq = [_mm_nt(jnp.concatenate([k2[i], q2[i]], axis=0), kbd[i]) for i in I]
    A = [jnp.where(strict, beta[i] * kq[i][0:C] * decay[i], 0.0) for i in I]
    qk = [kq[i][C:2 * C] * decay[i] for i in I]
    first = ((row & 1) == 1) & (s == row - 1)
    T = [jnp.where(eye, 1.0, 0.0) - jnp.where(first, A[i], 0.0) for i in I]
    for lg in range(1, 6):
        sub = (((row >> lg) & 1) == 1) & ((s >> lg) == (row >> lg) - 1)
        t1 = [_mm(T[i], bd(jnp.where(sub, A[i], 0.0))) for i in I]
        t2 = [_mm(t1[i], bd(T[i])) for i in I]
        T = [T[i] - t2[i] for i in I]
    uw = [_mm(T[i], jnp.concatenate([bd(beta[i] * v2[i]), bd(beta[i] * eG[i] * k2[i])], axis=1)) for i in I]
    g_last = [G[i][C - 1:C, :] for i in I]
    kend_t = [(k2[i] * jnp.exp(g_last[i] - G[i])).T for i in I]
    P = range(GDN_HEADS // 2)
    S = [s_ref[j] for j in P]
    o = [None] * len(items)
    for c in range(cps):
        ids = [c * len(P) + j for j in P]
        ws = [_mm(jnp.concatenate([uw[i][:, LANES:2 * LANES], q2[i] * eG[i]], axis=0), S[j])
              for j, i in enumerate(ids)]
        wn = [uw[i][:, 0:LANES] - ws[j][0:C] for j, i in enumerate(ids)]
        upd = [_mm(kend_t[i], wn[j]) for j, i in enumerate(ids)]
        for j, i in enumerate(ids):
            o[i] = ws[j][C:2 * C] + _mm(qk[i], bd(wn[j]))
        S = [jnp.exp(g_last[i]) * S[j] + jnp.where(same_half, upd[j], 0.0) for j, i in enumerate(ids)]
    for j in P:
        s_ref[j] = S[j]
    oss = [hsum(o[i] * o[i]) for i in I]
    for i in I:
        c, j = items[i]
        on = o[i] * lax.rsqrt(oss[i] * (1.0 / HEAD_W) + EPS) * og_ref[...]
        z = pg_ref[0, C * c:C * (c + 1), _GDN_Z0 + LANES * j:_GDN_Z0 + LANES * (j + 1)]
        o_ref[0, C * c:C * (c + 1), LANES * j:LANES * (j + 1)] = on * _silu(z)

    @pl.when(n < n_steps - 1)
    def _():
        xp_ref[_CONV_PAD - hist:_CONV_PAD, :] = xp_ref[_CONV_PAD + R - hist:_CONV_PAD + R, :]

    @pl.when(n == n_steps - 1)
    def _():
        lv = n_valid - (n_steps - 1) * R
        convfin_ref[0] = xp_ref[_CONV_PAD + lv - hist:_CONV_PAD + lv, :]
        sfin_ref[0] = s_ref[...]


def _gdn_call(pg, conv0, s0_bd, conv_w, alog_rep, dtb_rep, og_rep, n_valid):
    B, Lp, _ = pg.shape
    C = CHUNK
    n_chunks = Lp // C
    cps = 4 if n_chunks % 4 == 0 else (2 if n_chunks % 2 == 0 else 1)
    n_steps = n_chunks // cps
    R = C * cps
    npair = GDN_HEADS // 2
    kern = functools.partial(_gdn_kernel, C=C, cps=cps, n_valid=n_valid, n_steps=n_steps)
    full = lambda shape: pl.BlockSpec(shape, lambda b, n: (0,) * len(shape))
    return pl.pallas_call(
        kern,
        out_shape=(jax.ShapeDtypeStruct((B, Lp, GDN_WIDTH), F32),
                   jax.ShapeDtypeStruct((B, npair, LANES, LANES), F32),
                   jax.ShapeDtypeStruct((B, CONV_W - 1, GDN_QKV), F32)),
        grid=(B, n_steps),
        in_specs=[pl.BlockSpec((1, R, _GDN_COLS), lambda b, n: (b, n, 0)),
                  pl.BlockSpec((1, CONV_W - 1, GDN_QKV), lambda b, n: (b, 0, 0)),
                  pl.BlockSpec((1, npair, LANES, LANES), lambda b, n: (b, 0, 0, 0)),
                  full((CONV_W, GDN_QKV)), full((1, GDN_WIDTH)), full((1, GDN_WIDTH)), full((1, LANES))],
        out_specs=(pl.BlockSpec((1, R, GDN_WIDTH), lambda b, n: (b, n, 0)),
                   pl.BlockSpec((1, npair, LANES, LANES), lambda b, n: (b, 0, 0, 0)),
                   pl.BlockSpec((1, CONV_W - 1, GDN_QKV), lambda b, n: (b, 0, 0))),
        scratch_shapes=[pltpu.VMEM((_CONV_PAD + R, GDN_QKV), F32),
                        pltpu.VMEM((npair, LANES, LANES), F32)],
        compiler_params=_cparams(("arbitrary", "arbitrary")),
        name="gdn_chunks",
    )(pg, conv0, s0_bd, conv_w, alog_rep, dtb_rep, og_rep)


_ML_O0 = MLSTM_QKV
_ML_I0 = _ML_O0 + MLSTM_WIDTH
_ML_F0 = _ML_I0 + MLSTM_WIDTH
_ML_COLS = _ML_F0 + MLSTM_WIDTH


def _mlstm_kernel(pm_ref, c0_ref, n0_ref, m0_ref, ib_ref, fb_ref, og_ref,
                  o_ref, cfin_ref, nfin_ref, mfin_ref, c_ref, n_ref, m_ref, *, C, cps, n_valid, n_steps):
    n = pl.program_id(1)
    R = C * cps

    @pl.when(n == 0)
    def _():
        c_ref[...] = c0_ref[0]
        n_ref[...] = n0_ref[0]
        m_ref[...] = m0_ref[0]

    row, lane, s, lo_half, same_half = _pair_consts(C)
    ones_blk = jnp.where(same_half, 1.0, 0.0).astype(BF16)
    tr, tc = _iota((R, R), 0), _iota((R, R), 1)
    tri = jnp.where((tc <= tr) & ((tc >> 6) == (tr >> 6)), 1.0, 0.0).astype(BF16)
    causal = s <= row

    def hsum(x):
        return jnp.dot(_bf(x), ones_blk, preferred_element_type=F32)

    ig_all = pm_ref[0, :, _ML_I0:_ML_F0] + ib_ref[...]
    lf_all = -_softplus(-(pm_ref[0, :, _ML_F0:_ML_COLS] + fb_ref[...]))
    if n_valid < R * n_steps:
        valid = (_iota((R, MLSTM_WIDTH), 0) + n * R) < n_valid
        ig_all = jnp.where(valid, ig_all, NEG)
        lf_all = jnp.where(valid, lf_all, 0.0)
    f_all = _mm_sel_exact(tri, lf_all)

    P = range(MLSTM_HEADS // 2)
    items = [(c, j) for c in range(cps) for j in P]
    I = range(len(items))
    col = lambda base, i: pm_ref[0, C * items[i][0]:C * (items[i][0] + 1),
                                 base + LANES * items[i][1]:base + LANES * (items[i][1] + 1)]
    sub = lambda a, i: a[C * items[i][0]:C * (items[i][0] + 1), LANES * items[i][1]:LANES * (items[i][1] + 1)]
    bd = lambda a: _bd(a, lo_half)
    q2 = [col(0, i) for i in I]
    k2 = [col(MLSTM_WIDTH, i) * (HEAD_W ** -0.5) for i in I]
    v2 = [col(2 * MLSTM_WIDTH, i) for i in I]
    ig = [sub(ig_all, i) for i in I]
    F = [sub(f_all, i) for i in I]
    dm, dmax = [], []
    for i in I:
        d = jnp.where(causal, F[i] - _diag_row(F[i], row, s) + _diag_row(ig[i], row, s), NEG)
        d_e = jnp.max(jnp.where(lo_half, d, NEG), axis=1, keepdims=True)
        d_o = jnp.max(jnp.where(lo_half, NEG, d), axis=1, keepdims=True)
        dm.append(d)
        dmax.append(jnp.where(lo_half, d_e, d_o))
    qk = [_mm_nt(q2[i], bd(k2[i])) for i in I]
    fe = [F[i][C - 1:C, :] for i in I]
    se = [fe[i] - F[i] + ig[i] for i in I]
    se_max = [jnp.max(se[i], axis=0, keepdims=True) for i in I]
    ms, m_new = [None] * len(items), [None] * len(items)
    m_run = [m_ref[j] for j in P]
    for i in I:
        j = items[i][1]
        ms[i] = m_run[j]
        m_new[i] = jnp.maximum(fe[i] + ms[i], se_max[i])
        m_run[j] = m_new[i]
    mt = [jnp.maximum(F[i] + ms[i], dmax[i]) for i in I]
    inter = [jnp.exp(F[i] + ms[i] - mt[i]) for i in I]
    wgt = [jnp.exp(dm[i] - mt[i]) * qk[i] for i in I]
    wv = [_mm(wgt[i], bd(v2[i])) for i in I]
    wsum = [hsum(wgt[i]) for i in I]
    kw = [k2[i] * jnp.exp(se[i] - m_new[i]) for i in I]
    upd = [_mm(kw[i].T, v2[i]) for i in I]
    ksum = [jnp.sum(kw[i], axis=0, keepdims=True) for i in I]
    sc = [jnp.exp(fe[i] + ms[i] - m_new[i]) for i in I]
    cs = [c_ref[j] for j in P]
    ns = [n_ref[j] for j in P]
    o = [None] * len(items)
    for i in I:
        j = items[i][1]
        den = inter[i] * hsum(q2[i] * ns[j]) + wsum[i]
        h = (inter[i] * _mm(q2[i], cs[j]) + wv[i]) / jnp.maximum(jnp.abs(den), jnp.exp(-mt[i]))
        o[i] = _sigmoid(col(_ML_O0, i)) * h
        cs[j] = sc[i] * cs[j] + jnp.where(same_half, upd[i], 0.0)
        ns[j] = sc[i] * ns[j] + ksum[i]
    oss = [hsum(o[i] * o[i]) for i in I]
    for j in P:
        c_ref[j] = cs[j]
        n_ref[j] = ns[j]
        m_ref[j] = m_run[j]
    for i in I:
        c, j = items[i]
        o_ref[0, C * c:C * (c + 1), LANES * j:LANES * (j + 1)] = (
            o[i] * lax.rsqrt(oss[i] * (1.0 / HEAD_W) + EPS) * og_ref[:, LANES * j:LANES * (j + 1)])

    @pl.when(n == n_steps - 1)
    def _():
        cfin_ref[0] = c_ref[...]
        nfin_ref[0] = n_ref[...]
        mfin_ref[0] = m_ref[...]


def _mlstm_call(pm, c0_bd, n0, m0_rep, ib_rep, fb_rep, og, n_valid):
    B, Lp, _ = pm.shape
    C = CHUNK
    n_chunks = Lp // C
    cps = 4 if n_chunks % 4 == 0 else (2 if n_chunks % 2 == 0 else 1)
    n_steps = n_chunks // cps
    R = C * cps
    npair = MLSTM_HEADS // 2
    kern = functools.partial(_mlstm_kernel, C=C, cps=cps, n_valid=n_valid, n_steps=n_steps)
    full = lambda shape: pl.BlockSpec(shape, lambda b, n: (0,) * len(shape))
    st_c = pl.BlockSpec((1, npair, LANES, LANES), lambda b, n: (b, 0, 0, 0))
    st_v = pl.BlockSpec((1, npair, 1, LANES), lambda b, n: (b, 0, 0, 0))
    return pl.pallas_call(
        kern,
        out_shape=(jax.ShapeDtypeStruct((B, Lp, MLSTM_WIDTH), F32),
                   jax.ShapeDtypeStruct((B, npair, LANES, LANES), F32),
                   jax.ShapeDtypeStruct((B, npair, 1, LANES), F32),
                   jax.ShapeDtypeStruct((B, npair, 1, LANES), F32)),
        grid=(B, n_steps),
        in_specs=[pl.BlockSpec((1, R, _ML_COLS), lambda b, n: (b, n, 0)), st_c, st_v, st_v,
                  full((1, MLSTM_WIDTH)), full((1, MLSTM_WIDTH)), full((1, MLSTM_WIDTH))],
        out_specs=(pl.BlockSpec((1, R, MLSTM_WIDTH), lambda b, n: (b, n, 0)), st_c, st_v, st_v),
        scratch_shapes=[pltpu.VMEM((npair, LANES, LANES), F32),
                        pltpu.VMEM((npair, 1, LANES), F32),
                        pltpu.VMEM((npair, 1, LANES), F32)],
        compiler_params=_cparams(("arbitrary", "arbitrary")),
        name="mlstm_chunks",
    )(pm, c0_bd, n0, m0_rep, ib_rep, fb_rep, og)


_MLA_C0 = MLA_Q_LORA
_MLA_R0 = MLA_Q_LORA + MLA_KV_LORA
_MLA_COLS = _MLA_R0 + LANES
_MLA_HW = LANES
_Q_SCALE = MLA_QK ** -0.5 * math.log2(math.e)


def _swap_halves(x, lane, base, half):
    up = pltpu.roll(x, LANES - half, 1)
    dn = pltpu.roll(x, half, 1)
    first = (lane >= base) & (lane < base + half)
    second = (lane >= base + half) & (lane < base + 2 * half)
    return jnp.where(first, up, jnp.where(second, dn, 0.0))


def _mla_q_kernel(pm_ref, qag_ref, wuq_ref, qgain_ref, kvag_ref, krgain_ref,
                  cosq_ref, sinq_ref, cosk_ref, sink_ref, q_ref, lat_ref, kr_ref, *, tb):
    lane = _iota((tb, LANES), 1)
    ql = pm_ref[0, :, 0:_MLA_C0]
    ql = ql * lax.rsqrt(jnp.mean(ql * ql, axis=-1, keepdims=True) + EPS) * qag_ref[...]
    q = jnp.dot(_bf(ql), wuq_ref[...], preferred_element_type=F32)
    nope = lane < MLA_NOPE
    rope = (lane >= MLA_NOPE) & (lane < MLA_QK)
    cosq = cosq_ref[...]
    sinq = sinq_ref[...]
    for h in range(MLA_HEADS):
        qh = q[:, _MLA_HW * h:_MLA_HW * (h + 1)]
        sq = qh * qh
        ss_n = jnp.sum(jnp.where(nope, sq, 0.0), axis=-1, keepdims=True) * (1.0 / MLA_NOPE)
        ss_r = jnp.sum(jnp.where(rope, sq, 0.0), axis=-1, keepdims=True) * (1.0 / MLA_ROPE)
        rinv = jnp.where(nope, lax.rsqrt(ss_n + EPS), lax.rsqrt(ss_r + EPS))
        qn = qh * rinv * qgain_ref[...]
        sw = _swap_halves(qn, lane, MLA_NOPE, MLA_ROPE // 2)
        qr = jnp.where(nope, qn, qn * cosq + sw * sinq)
        q_ref[0, :, _MLA_HW * h:_MLA_HW * (h + 1)] = _bf(qr * _Q_SCALE)
    ckv = pm_ref[0, :, _MLA_C0:_MLA_R0]
    lat_ref[0] = ckv * lax.rsqrt(jnp.mean(ckv * ckv, axis=-1, keepdims=True) + EPS) * kvag_ref[...]
    kx = pm_ref[0, :, _MLA_R0:_MLA_COLS]
    ssk = jnp.sum(kx * kx, axis=-1, keepdims=True) * (1.0 / MLA_ROPE)
    kn = kx * lax.rsqrt(ssk + EPS) * krgain_ref[...]
    swk = _swap_halves(kn, lane, 0, MLA_ROPE // 2)
    kr = kn * cosk_ref[...] + swk * sink_ref[...]
    kr_ref[0] = kr[:, 0:MLA_ROPE]


def _mla_q_call(pm, qag, wuq, qgain, kvag, krgain, cosq, sinq, cosk, sink):
    B, L, _ = pm.shape
    tb = min(L, 512)
    full = lambda shape: pl.BlockSpec(shape, lambda b, i: (0,) * len(shape))
    tab = pl.BlockSpec((tb, LANES), lambda b, i: (i, 0))
    nq = MLA_HEADS * _MLA_HW
    return pl.pallas_call(
        functools.partial(_mla_q_kernel, tb=tb),
        out_shape=(jax.ShapeDtypeStruct((B, L, nq), BF16),
                   jax.ShapeDtypeStruct((B, L, MLA_KV_LORA), F32),
                   jax.ShapeDtypeStruct((B, L, MLA_ROPE), F32)),
        grid=(B, L // tb),
        in_specs=[pl.BlockSpec((1, tb, _MLA_COLS), lambda b, i: (b, i, 0)),
                  full((1, MLA_Q_LORA)), full((MLA_Q_LORA, nq)), full((1, LANES)),
                  full((1, MLA_KV_LORA)), full((1, LANES)), tab, tab, tab, tab],
        out_specs=(pl.BlockSpec((1, tb, nq), lambda b, i: (b, i, 0)),
                   pl.BlockSpec((1, tb, MLA_KV_LORA), lambda b, i: (b, i, 0)),
                   pl.BlockSpec((1, tb, MLA_ROPE), lambda b, i: (b, i, 0))),
        compiler_params=_cparams(("arbitrary", "arbitrary")),
        name="mla_q_proj",
    )(pm, qag, wuq, qgain, kvag, krgain, cosq, sinq, cosk, sink)


def _mla_kv_kernel(ckv_ref, krp_ref, wukv_ref, kgain_ref, k_ref, kv_ref, *, tb):
    lane = _iota((tb, LANES), 1)
    nope = lane < MLA_NOPE
    kv = jnp.dot(_bf(ckv_ref[0]), wukv_ref[...], preferred_element_type=F32)
    krp = krp_ref[0]
    for h in range(MLA_HEADS):
        g = kv[:, _MLA_HW * h:_MLA_HW * (h + 1)]
        ss = jnp.sum(jnp.where(nope, g * g, 0.0), axis=-1, keepdims=True) * (1.0 / MLA_NOPE)
        k_ref[0, :, _MLA_HW * h:_MLA_HW * (h + 1)] = _bf(g * lax.rsqrt(ss + EPS) * kgain_ref[...] + krp)
        kv_ref[0, :, _MLA_HW * h:_MLA_HW * (h + 1)] = _bf(jnp.where(nope, 1.0, g))


def _mla_kv_call(ckv_all, krp_all, wukv, kgain):
    B, Lk, _ = ckv_all.shape
    tb = 512 if Lk % 512 == 0 else Lk
    nk = MLA_HEADS * _MLA_HW
    full = lambda shape: pl.BlockSpec(shape, lambda b, i: (0,) * len(shape))
    return pl.pallas_call(
        functools.partial(_mla_kv_kernel, tb=tb),
        out_shape=(jax.ShapeDtypeStruct((B, Lk, nk), BF16), jax.ShapeDtypeStruct((B, Lk, nk), BF16)),
        grid=(B, Lk // tb),
        in_specs=[pl.BlockSpec((1, tb, MLA_KV_LORA), lambda b, i: (b, i, 0)),
                  pl.BlockSpec((1, tb, LANES), lambda b, i: (b, i, 0)),
                  full((MLA_KV_LORA, nk)), full((1, LANES))],
        out_specs=(pl.BlockSpec((1, tb, nk), lambda b, i: (b, i, 0)),
                   pl.BlockSpec((1, tb, nk), lambda b, i: (b, i, 0))),
        compiler_params=_cparams(("arbitrary", "arbitrary"), VMEM_LIMIT),
        name="mla_kv_proj",
    )(ckv_all, krp_all, wukv, kgain)


def _attn_kernel(q_ref, k_ref, kv_ref, o_ref, m_ref, acc_ref, *, tq, tk, nk, causal):
    qi = pl.program_id(2)
    ki = pl.program_id(3)
    last = _last_kv_block(qi, tq, tk) if causal else nk - 1
    first_diag = (qi * tq) // tk

    @pl.when(ki == 0)
    def _():
        m_ref[...] = jnp.full(m_ref.shape, NEG, F32)
        acc_ref[...] = jnp.zeros(acc_ref.shape, F32)

    if tk % LANES == 0:
        widen = lambda v: jnp.concatenate([v] * (tk // LANES), axis=1)
    else:
        widen = lambda v: v[:, 0:1]

    def step(masked):
        if masked:
            qc = (qi * tq + _iota((tq, tk), 0)) >> 6
            kc = (ki * tk + _iota((tq, tk), 1)) >> 6
            keep = kc <= qc
        H = range(2)
        hs = [slice(_MLA_HW * hh, _MLA_HW * (hh + 1)) for hh in H]
        sc = [lax.dot_general(q_ref[0, :, hs[hh]], k_ref[0, :, hs[hh]], (((1,), (1,)), ((), ())),
                              preferred_element_type=F32) for hh in H]
        if masked:
            sc = [jnp.where(keep, t, NEG) for t in sc]
        m_prev = [m_ref[hh] for hh in H]
        m_new = [jnp.maximum(m_prev[hh], jnp.max(sc[hh], axis=-1, keepdims=True)) for hh in H]
        p = [jnp.exp2(sc[hh] - widen(m_new[hh])) for hh in H]
        alpha = [jnp.exp2(m_prev[hh] - m_new[hh]) for hh in H]
        pv = [jnp.dot(_bf(p[hh]), kv_ref[0, :, hs[hh]], preferred_element_type=F32) for hh in H]
        for hh in H:
            acc_ref[hh] = alpha[hh] * acc_ref[hh] + pv[hh]
            m_ref[hh] = m_new[hh]

    if causal:
        pl.when(ki < first_diag)(lambda: step(False))
        pl.when((ki >= first_diag) & (ki <= last))(lambda: step(True))
    else:
        step(False)

    @pl.when(ki == last)
    def _():
        lane = _iota((tq, LANES), 1)
        a_e, a_o = acc_ref[0], acc_ref[1]
        o_e = pltpu.roll(a_e, MLA_NOPE, 1) / a_e
        o_o = a_o / pltpu.roll(a_o, MLA_NOPE, 1)
        o_ref[0] = jnp.where(lane < MLA_NOPE, o_e, o_o)


def _last_kv_block(qi, tq, tk):
    return ((qi + 1) * tq - 1) // tk


def _attn_call(q, k, kv, causal):
    B, Lq, _ = q.shape
    Lk = k.shape[1]
    if causal:
        tq = min(Lq, 512)
        tk = min(Lk, 1024)
    else:
        tq, tk = Lq, Lk
    nq, nk = Lq // tq, Lk // tk
    w2 = 2 * _MLA_HW
    if causal:
        kmap = lambda b, hp, qi, ki: (b, jnp.minimum(ki, _last_kv_block(qi, tq, tk)), hp)
    else:
        kmap = lambda b, hp, qi, ki: (b, ki, hp)
    return pl.pallas_call(
        functools.partial(_attn_kernel, tq=tq, tk=tk, nk=nk, causal=causal),
        out_shape=jax.ShapeDtypeStruct((B, Lq, MLA_HEADS * MLA_NOPE), F32),
        grid=(B, MLA_HEADS // 2, nq, nk),
        in_specs=[pl.BlockSpec((1, tq, w2), lambda b, hp, qi, ki: (b, qi, hp)),
                  pl.BlockSpec((1, tk, w2), kmap),
                  pl.BlockSpec((1, tk, w2), kmap)],
        out_specs=pl.BlockSpec((1, tq, LANES), lambda b, hp, qi, ki: (b, qi, hp)),
        scratch_shapes=[pltpu.VMEM((2, tq, LANES), F32), pltpu.VMEM((2, tq, LANES), F32)],
        compiler_params=_cparams(("arbitrary",) * 4, VMEM_LIMIT),
        name="mla_attention",
    )(q, k, kv)


def _outproj_kernel(a_ref, b_ref, c_ref, x_ref, g1_ref, sh_ref, sc_ref, ng_ref, wa_ref, wb_ref, wc_ref,
                    x1_ref, h2_ref):
    y = (jnp.dot(_bf(a_ref[0]), wa_ref[...], preferred_element_type=F32)
         + jnp.dot(_bf(b_ref[0]), wb_ref[...], preferred_element_type=F32)
         + jnp.dot(_bf(c_ref[0]), wc_ref[...], preferred_element_type=F32))
    x1 = x_ref[0] + g1_ref[0] * y
    x1_ref[0] = x1
    n = x1 * lax.rsqrt(jnp.mean(x1 * x1, axis=-1, keepdims=True) + EPS) * ng_ref[...]
    h2_ref[0] = n * (1.0 + sc_ref[0]) + sh_ref[0]


def _outproj_call(oa, ob, oc, x, g1, sh2, sc2, ng, wa, wb, wc):
    B, L, _ = x.shape
    tb = min(L, 512)
    full = lambda shape: pl.BlockSpec(shape, lambda b, i: (0,) * len(shape))
    per_b = pl.BlockSpec((1, 1, D_MODEL), lambda b, i: (b, 0, 0))
    blk = lambda wdt: pl.BlockSpec((1, tb, wdt), lambda b, i: (b, i, 0))
    return pl.pallas_call(
        _outproj_kernel,
        out_shape=(jax.ShapeDtypeStruct((B, L, D_MODEL), F32), jax.ShapeDtypeStruct((B, L, D_MODEL), F32)),
        grid=(B, L // tb),
        in_specs=[blk(GDN_WIDTH), blk(MLA_HEADS * MLA_NOPE), blk(MLSTM_WIDTH), blk(D_MODEL),
                  per_b, per_b, per_b, full((1, D_MODEL)), full(wa.shape), full(wb.shape), full(wc.shape)],
        out_specs=(blk(D_MODEL), blk(D_MODEL)),
        compiler_params=_cparams(("arbitrary", "arbitrary"), VMEM_LIMIT),
        name="out_proj",
    )(oa, ob, oc, x, g1, sh2, sc2, ng, wa, wb, wc)


def _topk_rows(svs, io, k, payloads=None):
    svs = list(svs)
    vals = [[] for _ in svs]
    outs = [[] for _ in svs]
    big = jnp.float32(1e9)
    for _ in range(k):
        for a in range(len(svs)):
            m = jnp.max(svs[a], axis=0, keepdims=True)
            ix = jnp.min(jnp.where(svs[a] == m, io, big), axis=0, keepdims=True)
            hit = io == ix
            svs[a] = jnp.where(hit, -jnp.inf, svs[a])
            vals[a].append(m)
            outs[a].append(ix if payloads is None
                           else jnp.max(jnp.where(hit, payloads[a], -1.0), axis=0, keepdims=True))
    return [(jnp.concatenate(v, axis=0), jnp.concatenate(o, axis=0)) for v, o in zip(vals, outs)]


_CAND_ROWS = PEER_TOPK + 7 * 8 + 8


def _cand_blocks(t1, t2, combine):
    blocks = [combine(t1[0:1], t2)]
    blocks += [combine(t1[a:a + 1], t2[0:8]) for a in range(1, 8)]
    blocks.append(combine(t1[8:PEER_TOPK], t2[0:1]))
    return jnp.concatenate(blocks, axis=0)


def _route_kernel(h_ref, wq_ref, keys_ref, e_ref, g_ref, qs_ref, es_ref, gs_ref, *, tb):
    q = jnp.dot(_bf(h_ref[...]), wq_ref[...], preferred_element_type=F32)
    for g in range(2 * PEER_HEADS):
        qs_ref[g] = _bf(q[:, LANES * g:LANES * (g + 1)])
    io_k = _iota((N_KEYS, tb), 0).astype(F32)
    r = _iota((_CAND_ROWS, tb), 0)
    mid = r - PEER_TOPK
    io_c = jnp.where(r < PEER_TOPK, r,
                     jnp.where(r < _CAND_ROWS - 8, ((mid >> 3) + 1) * PEER_TOPK + (mid & 7),
                               (r - (_CAND_ROWS - 16)) * PEER_TOPK)).astype(F32)

    def head(h, carry):
        scores = [lax.dot_general(keys_ref[2 * h + p], qs_ref[2 * h + p], (((1,), (1,)), ((), ())),
                                  preferred_element_type=F32) for p in range(2)]
        (v1, i1), (v2, i2) = _topk_rows(scores, io_k, PEER_TOPK)
        cand = _cand_blocks(v1, v2, lambda x, y: x + y)
        expert = _cand_blocks(i1, i2, lambda x, y: x * N_KEYS + y)
        (sc, e), = _topk_rows([cand], io_c, PEER_TOPK, payloads=[expert])
        ex = jnp.exp(sc - jnp.max(sc, axis=0, keepdims=True))
        es_ref[h] = e * _ROW_SUB
        gs_ref[h] = ex / jnp.sum(ex, axis=0, keepdims=True)
        return carry

    lax.fori_loop(0, PEER_HEADS, head, 0)
    e_ref[...] = es_ref[...].reshape(PEER_SLOTS, tb).T.astype(jnp.int32)
    g_ref[...] = gs_ref[...].reshape(PEER_SLOTS, tb).T


def _route_call(h2, wq, keys):
    T = h2.shape[0]
    tb = min(T, 256)
    full = lambda shape: pl.BlockSpec(shape, lambda i: (0,) * len(shape))
    return pl.pallas_call(
        functools.partial(_route_kernel, tb=tb),
        out_shape=(jax.ShapeDtypeStruct((T, PEER_SLOTS), jnp.int32), jax.ShapeDtypeStruct((T, PEER_SLOTS), F32)),
        grid=(T // tb,),
        in_specs=[pl.BlockSpec((tb, D_MODEL), lambda i: (i, 0)), full(wq.shape), full(keys.shape)],
        out_specs=(pl.BlockSpec((tb, PEER_SLOTS), lambda i: (i, 0)), pl.BlockSpec((tb, PEER_SLOTS), lambda i: (i, 0))),
        scratch_shapes=[pltpu.VMEM((2 * PEER_HEADS, tb, LANES), BF16),
                        pltpu.VMEM((PEER_HEADS, PEER_TOPK, tb), F32),
                        pltpu.VMEM((PEER_HEADS, PEER_TOPK, tb), F32)],
        compiler_params=_cparams(("arbitrary",), VMEM_LIMIT),
        name="peer_route",
    )(h2, wq, keys)


_ROW_SUB = 4
_GROUP = 8
_BITREV3 = (0, 4, 2, 6, 1, 5, 3, 7)


def _expert_row(tab_ref, row0):
    return pltpu.bitcast(tab_ref[pl.ds(row0, _ROW_SUB), :], BF16).astype(F32)


def _gelu(x):
    return 0.5 * x * (1.0 + lax.erf(x * (2.0 ** -0.5)))


def _fold(a, b, h, sub):
    m = (sub & h) == 0
    if h == 4:
        return jnp.where(m, a, b) + pltpu.roll(jnp.where(m, b, a), 4, 0)
    return jnp.where(m, a + pltpu.roll(a, 8 - h, 0), b + pltpu.roll(b, h, 0))


def _transpose8(v, sub):
    v = list(v)
    for h in (4, 2, 1):
        m = (sub & h) == 0
        for i in range(8):
            if i & h == 0:
                a, b = v[i], v[i | h]
                v[i] = jnp.where(m, a, pltpu.roll(b, h, 0))
                v[i | h] = jnp.where(m, pltpu.roll(a, 8 - h, 0), b)
    return v


def _rows_to_dense(x_ref, dense_ref, tb, sub):
    for g in range(tb // 8):
        cols = [x_ref[8 * g:8 * (g + 1), LANES * c:LANES * (c + 1)] for c in range(8)]
        for k, d in enumerate(_transpose8(cols, sub)):
            dense_ref[8 * g + k] = d


def _dense_to_rows(dense_ref, o_ref, tb, sub):
    for g in range(tb // 8):
        cols = _transpose8([dense_ref[8 * g + k] for k in range(8)], sub)
        for c in range(8):
            o_ref[8 * g:8 * (g + 1), LANES * c:LANES * (c + 1)] = cols[c]


def _peer_u_kernel(idx_ref, xrow_ref, g_ref, tab_ref, c_ref, part_ref, x_ref, *, tb):
    ones8 = jnp.ones((8, LANES), BF16)
    sub = _iota((8, LANES), 0)
    nt = lambda a, b: lax.dot_general(a, b, (((1,), (1,)), ((), ())), preferred_element_type=F32)
    _rows_to_dense(xrow_ref, x_ref, tb, sub)

    def gather(t, slot, k):
        xv = x_ref[t]
        buf = part_ref.at[slot]
        for g in range(PEER_SLOTS // _GROUP):
            p = [_expert_row(tab_ref, idx_ref[t, _GROUP * g + _BITREV3[i]]) * xv for i in range(_GROUP)]
            z = [_fold(p[2 * i], p[2 * i + 1], 4, sub) for i in range(4)]
            w = [_fold(z[0], z[1], 2, sub), _fold(z[2], z[3], 2, sub)]
            r0 = PEER_SLOTS * k + _GROUP * g
            buf[r0:r0 + _GROUP, :] = _fold(w[0], w[1], 1, sub)

    def finish(grp, slot):
        res = nt(ones8, _bf(part_ref[slot]))
        act = res[:, 0:LANES]
        for k in range(1, 8):
            act = jnp.where(sub == k, res[:, LANES * k:LANES * (k + 1)], act)
        rows = pl.ds(pl.multiple_of(grp * 8, 8), 8)
        c_ref[rows, :] = g_ref[rows, :] * _gelu(act)

    for k in range(8):
        gather(k, 0, k)

    def trip(grp, carry):
        slot = grp & 1
        finish(grp - 1, 1 - slot)
        for k in range(8):
            gather(grp * 8 + k, slot, k)
        return carry

    n_grp = tb // 8
    lax.fori_loop(1, n_grp, trip, 0)
    finish(n_grp - 1, (n_grp - 1) & 1)


def _peer_v_kernel(idx_ref, c_ref, xrow_ref, g2_ref, tab_ref, o_ref, m_ref, x_ref, *, tb):
    sub = _iota((8, LANES), 0)
    _rows_to_dense(xrow_ref, x_ref, tb, sub)
    eye = _iota((PEER_SLOTS, LANES), 0) == _iota((PEER_SLOTS, LANES), 1)
    diag = jnp.where(eye[None], c_ref[...][:, None, :], 0.0).reshape(tb * PEER_SLOTS, LANES)
    rep = jnp.dot(_bf(diag), jnp.ones((LANES, LANES), BF16), preferred_element_type=F32)
    m_ref[...] = rep.reshape(tb, PEER_SLOTS, LANES)

    def tok(t):
        y = jnp.zeros((8, LANES), F32)
        base = t * PEER_SLOTS
        for g in range(PEER_SLOTS // _GROUP):
            rows = [idx_ref[base + k] for k in range(_GROUP)]
            r = [m_ref[t, pl.ds(_GROUP * g + k, 1), :] * _expert_row(tab_ref, rows[k]) for k in range(_GROUP)]
            y = y + (((r[0] + r[1]) + (r[2] + r[3])) + ((r[4] + r[5]) + (r[6] + r[7])))
            base = base + _GROUP + lax.shift_right_arithmetic(rows[0], jnp.int32(31))
        x_ref[t] = x_ref[t] + g2_ref[0] * y

    per_trip = 2

    def trip(i, carry):
        for k in range(per_trip):
            tok(i * per_trip + k)
        return carry

    lax.fori_loop(0, tb // per_trip, trip, 0)
    _dense_to_rows(x_ref, o_ref, tb, sub)


def _table_spec():
    return pl.BlockSpec((N_EXPERTS * _ROW_SUB, LANES), lambda i: (0, 0), pipeline_mode=pl.Buffered(1))


def _peer_u_call(idx, x, gate, tab, tb):
    T = idx.shape[0]
    smem = pl.BlockSpec((tb, PEER_SLOTS), lambda i: (i, 0), memory_space=pltpu.SMEM)
    return pl.pallas_call(
        functools.partial(_peer_u_kernel, tb=tb),
        out_shape=jax.ShapeDtypeStruct((T, PEER_SLOTS), F32),
        grid=(T // tb,),
        in_specs=[smem, pl.BlockSpec((tb, D_MODEL), lambda i: (i, 0)),
                  pl.BlockSpec((tb, PEER_SLOTS), lambda i: (i, 0)), _table_spec()],
        out_specs=pl.BlockSpec((tb, PEER_SLOTS), lambda i: (i, 0)),
        scratch_shapes=[pltpu.VMEM((2, 8 * PEER_SLOTS, LANES), F32), pltpu.VMEM((tb, 8, LANES), F32)],
        compiler_params=_cparams(("arbitrary",), VMEM_LIMIT),
        name="peer_u",
    )(idx, x, gate, tab)


def _peer_v_call(idx, coef, x, g2, tab, tb, L):
    T = idx.shape[0]
    smem = pl.BlockSpec((tb * PEER_SLOTS,), lambda i: (i,), memory_space=pltpu.SMEM)
    per_step = L // tb
    idx = idx.reshape(T * PEER_SLOTS)
    return pl.pallas_call(
        functools.partial(_peer_v_kernel, tb=tb),
        out_shape=jax.ShapeDtypeStruct((T, D_MODEL), F32),
        grid=(T // tb,),
        in_specs=[smem, pl.BlockSpec((tb, PEER_SLOTS), lambda i: (i, 0)),
                  pl.BlockSpec((tb, D_MODEL), lambda i: (i, 0)),
                  pl.BlockSpec((1, 8, LANES), lambda i: (i // per_step, 0, 0)), _table_spec()],
        out_specs=pl.BlockSpec((tb, D_MODEL), lambda i: (i, 0)),
        scratch_shapes=[pltpu.VMEM((tb, PEER_SLOTS, LANES), F32), pltpu.VMEM((tb, 8, LANES), F32)],
        compiler_params=_cparams(("arbitrary",), VMEM_LIMIT),
        name="peer_v",
    )(idx, coef, x, g2, tab)


def _pack_table(tab):
    bits = lax.bitcast_convert_type(tab.astype(BF16), jnp.uint16).astype(jnp.uint32)
    bits = bits.reshape(tab.shape[0], _ROW_SUB, 2, LANES)
    return (bits[:, :, 0, :] | (bits[:, :, 1, :] << 16)).reshape(tab.shape[0] * _ROW_SUB, LANES)


def _rep_heads(w, width=HEAD_W):
    return jnp.repeat(w, width, axis=-1)


def _to_bd(s):
    B, H = s.shape[:2]
    s = s.reshape(B, H // 2, 2, HEAD_W, HEAD_W)
    z = jnp.zeros_like(s[:, :, 0])
    top = jnp.concatenate([s[:, :, 0], z], axis=-1)
    bot = jnp.concatenate([z, s[:, :, 1]], axis=-1)
    return jnp.concatenate([top, bot], axis=-2)


def _from_bd(s):
    B, P = s.shape[:2]
    return jnp.stack([s[:, :, :HEAD_W, :HEAD_W], s[:, :, HEAD_W:, HEAD_W:]], axis=2).reshape(B, 2 * P, HEAD_W, HEAD_W)


def _prep_layer(l, w):
    o = [0]
    for sz in (GDN_QKV, GDN_WIDTH, GDN_HEADS, GDN_HEADS, MLA_Q_LORA, MLA_KV_LORA, MLA_ROPE,
               MLSTM_QKV, MLSTM_WIDTH, MLSTM_HEADS, MLSTM_HEADS):
        o.append(o[-1] + sz)
    wi = w['w_in'][l]
    col = lambda i: wi[:, o[i]:o[i + 1]]
    p = {}
    p['w_gdn'] = _bf(jnp.concatenate([col(0), col(1), _rep_heads(col(2)), _rep_heads(col(3))], axis=1))
    p['w_mla'] = _bf(jnp.concatenate([col(4), col(5), col(6), jnp.zeros((D_MODEL, LANES - MLA_ROPE), F32)], axis=1))
    p['w_mls'] = _bf(jnp.concatenate([col(7), col(8), _rep_heads(col(9)), _rep_heads(col(10))], axis=1))
    p['alog'] = _rep_heads(w['gdn_a_log'][l]).reshape(1, GDN_WIDTH)
    p['dtb'] = _rep_heads(w['gdn_dt_bias'][l]).reshape(1, GDN_WIDTH)
    p['gdn_og'] = jnp.tile(w['gdn_out_g'][l], 2).reshape(1, LANES)
    p['ib'] = _rep_heads(w['mlstm_i_bias'][l]).reshape(1, MLSTM_WIDTH)
    p['fb'] = _rep_heads(w['mlstm_f_bias'][l]).reshape(1, MLSTM_WIDTH)
    p['mls_og'] = w['mlstm_out_g'][l].reshape(1, MLSTM_WIDTH)
    wuq = w['mla_w_uq'][l].reshape(MLA_Q_LORA, MLA_HEADS, MLA_QK)
    p['wuq'] = _bf(jnp.pad(wuq, ((0, 0), (0, 0), (0, _MLA_HW - MLA_QK))).reshape(MLA_Q_LORA, MLA_HEADS * _MLA_HW))
    p['wukv'] = _bf(w['mla_w_ukv'][l])
    p['qag'] = w['mla_q_a_g'][l].reshape(1, MLA_Q_LORA)
    p['kvag'] = w['mla_kv_a_g'][l].reshape(1, MLA_KV_LORA)
    p['qgain'] = jnp.pad(w['mla_q_gain'][l], (0, LANES - MLA_QK)).reshape(1, LANES)
    kg = w['mla_k_gain'][l]
    p['kgain'] = jnp.pad(kg[:MLA_NOPE], (0, LANES - MLA_NOPE)).reshape(1, LANES)
    p['krgain'] = jnp.pad(kg[MLA_NOPE:], (0, LANES - MLA_ROPE)).reshape(1, LANES)
    wo = w['w_out'][l]
    p['wo_a'] = _bf(wo[0:GDN_WIDTH])
    p['wo_b'] = _bf(wo[GDN_WIDTH:2 * GDN_WIDTH])
    p['wo_c'] = _bf(wo[2 * GDN_WIDTH:])
    p['wq'] = _bf(w['peer_w_q'][l])
    p['keys'] = _bf(w['peer_sub_keys'][l].reshape(2 * PEER_HEADS, N_KEYS, LANES))
    p['u_tab'] = _pack_table(w['peer_u'][l])
    p['v_tab'] = _pack_table(w['peer_v'][l])
    p['conv_w'] = w['gdn_conv_w'][l]
    p['norm_attn_g'] = w['norm_attn_g'][l].reshape(1, D_MODEL)
    p['norm_ffn_g'] = w['norm_ffn_g'][l].reshape(1, D_MODEL)
    return p


def _rope_tables(pos):
    half = MLA_ROPE // 2
    inv = ROPE_THETA ** (-jnp.arange(half, dtype=F32) / half)
    ang = pos.astype(F32)[:, None] * inv[None, :]
    cos, sin = jnp.cos(ang), jnp.sin(ang)
    n = pos.shape[0]
    c2 = jnp.concatenate([cos, cos], axis=1)
    s2 = jnp.concatenate([-sin, sin], axis=1)
    padq = lambda t: jnp.pad(t, ((0, 0), (MLA_NOPE, LANES - MLA_QK)))
    padk = lambda t: jnp.pad(t, ((0, 0), (0, LANES - MLA_ROPE)))
    return padq(c2), padq(s2), padk(c2), padk(s2)


def _pad_rows(a, n):
    return jnp.pad(a, ((0, 0), (0, n - a.shape[1]), (0, 0)))


def _layer(x, mod, p, st, rope, prompt):
    B, L, _ = x.shape
    sh1, sc1, g1, sh2, sc2, g2 = [m.reshape(B, 1, D_MODEL) for m in jnp.split(mod, 6, axis=-1)]
    pg, pm, pl_ = _inproj_call(x, sh1, sc1, p['norm_attn_g'], p['w_gdn'], p['w_mla'], p['w_mls'])
    Lp = -(-L // CHUNK) * CHUNK
    if Lp != L:
        pg, pl_ = _pad_rows(pg, Lp), _pad_rows(pl_, Lp)
    o_a, gdn_bd, conv_new = _gdn_call(pg, st['gdn_conv'], _to_bd(st['gdn']), p['conv_w'],
                                      p['alog'], p['dtb'], p['gdn_og'], L)
    o_c, mc_bd, mn_p, mm_p = _mlstm_call(
        pl_, _to_bd(st['mlstm_c']), st['mlstm_n'].reshape(B, MLSTM_HEADS // 2, 1, LANES),
        _rep_heads(st['mlstm_m']).reshape(B, MLSTM_HEADS // 2, 1, LANES), p['ib'], p['fb'], p['mls_og'], L)
    o_a, o_c = o_a[:, :L], o_c[:, :L]
    q, ckv, kr = _mla_q_call(pm, p['qag'], p['wuq'], p['qgain'], p['kvag'], p['krgain'], *rope)
    krp = lambda t: jnp.pad(t, ((0, 0), (0, 0), (MLA_NOPE, LANES - MLA_QK)))
    k, kv = _mla_kv_call(ckv, krp(kr), p['wukv'], p['kgain'])
    if not prompt:
        k_c, kv_c = _mla_kv_call(st['mla_latent'], krp(st['mla_krope']), p['wukv'], p['kgain'])
        k = jnp.concatenate([k_c, k], axis=1)
        kv = jnp.concatenate([kv_c, kv], axis=1)
    o_b = _attn_call(q, k, kv, prompt)
    x1, h2 = _outproj_call(o_a, o_b, o_c, x, g1, sh2, sc2, p['norm_ffn_g'], p['wo_a'], p['wo_b'], p['wo_c'])
    T = B * L
    h2f = h2.reshape(T, D_MODEL)
    idx, gate = _route_call(h2f, p['wq'], p['keys'])
    tb = min(L, 128)
    coef = _peer_u_call(idx, h2f, gate, p['u_tab'], tb)
    x2 = _peer_v_call(idx, coef, x1.reshape(T, D_MODEL), g2.reshape(B, 8, LANES), p['v_tab'], tb, L)
    new_state = (ckv, kr, _from_bd(gdn_bd), conv_new, _from_bd(mc_bd),
                 mn_p.reshape(B, MLSTM_HEADS, HEAD_W), mm_p.reshape(B, MLSTM_HEADS, HEAD_W)[:, :, 0])
    return x2.reshape(B, L, D_MODEL), new_state


def kernel(x_prompt, x_sample, c_prompt, c_sample, cache_mla_latent, cache_mla_krope, state_gdn, state_gdn_conv, state_mlstm_c, state_mlstm_n, state_mlstm_m, ada_w, ada_b, norm_attn_g, norm_ffn_g, w_in, gdn_conv_w, gdn_a_log, gdn_dt_bias, gdn_out_g, mla_q_a_g, mla_w_uq, mla_kv_a_g, mla_w_ukv, mla_q_gain, mla_k_gain, mlstm_i_bias, mlstm_f_bias, mlstm_out_g, w_out, peer_w_q, peer_sub_keys, peer_u, peer_v):
    w = dict(ada_w=ada_w, ada_b=ada_b, norm_attn_g=norm_attn_g, norm_ffn_g=norm_ffn_g, w_in=w_in,
             gdn_conv_w=gdn_conv_w, gdn_a_log=gdn_a_log, gdn_dt_bias=gdn_dt_bias, gdn_out_g=gdn_out_g,
             mla_q_a_g=mla_q_a_g, mla_w_uq=mla_w_uq, mla_kv_a_g=mla_kv_a_g, mla_w_ukv=mla_w_ukv,
             mla_q_gain=mla_q_gain, mla_k_gain=mla_k_gain, mlstm_i_bias=mlstm_i_bias,
             mlstm_f_bias=mlstm_f_bias, mlstm_out_g=mlstm_out_g, w_out=w_out, peer_w_q=peer_w_q,
             peer_sub_keys=peer_sub_keys, peer_u=peer_u, peer_v=peer_v)
    B, Lp, _ = x_prompt.shape
    Bs, Ls, _ = x_sample.shape
    past = cache_mla_latent.shape[2]
    rope_p = _rope_tables(jnp.arange(Lp, dtype=jnp.int32))
    rope_s = _rope_tables(past + jnp.arange(Ls, dtype=jnp.int32))
    mods = _mod_call(jnp.concatenate([c_prompt, c_sample], axis=0), ada_w, ada_b)
    xp, xs = x_prompt, x_sample
    new_p, new_s = [], []
    for l in range(DEPTH):
        p = _prep_layer(l, w)
        st_p = {
            'gdn': jnp.zeros((B, GDN_HEADS, HEAD_W, HEAD_W), F32),
            'gdn_conv': jnp.zeros((B, CONV_W - 1, GDN_QKV), F32),
            'mlstm_c': jnp.zeros((B, MLSTM_HEADS, HEAD_W, HEAD_W), F32),
            'mlstm_n': jnp.zeros((B, MLSTM_HEADS, HEAD_W), F32),
            'mlstm_m': jnp.zeros((B, MLSTM_HEADS), F32),
        }
        st_s = {
            'mla_latent': cache_mla_latent[l], 'mla_krope': cache_mla_krope[l],
            'gdn': state_gdn[l], 'gdn_conv': state_gdn_conv[l],
            'mlstm_c': state_mlstm_c[l], 'mlstm_n': state_mlstm_n[l], 'mlstm_m': state_mlstm_m[l],
        }
        xp, sp = _layer(xp, mods[l, :B], p, st_p, rope_p, True)
        xs, ss = _layer(xs, mods[l, B:], p, st_s, rope_s, False)
        new_p.append(sp)
        new_s.append(ss)
    outs_p = [jnp.stack([s[i] for s in new_p]) for i in range(7)]
    outs_s = [jnp.stack([s[i] for s in new_s]) for i in range(7)]
    return (xp, xs, *outs_p, *outs_s)
```

```python
import functools
import math

import jax
import jax.numpy as jnp
from jax import lax
from jax.experimental import pallas as pl
from jax.experimental.pallas import tpu as pltpu

F32 = jnp.float32
BF16 = jnp.bfloat16

D_MODEL = 1024
DEPTH = 2
CHUNK = 64
EPS = 1e-6
GDN_HEADS = 6
GDN_DK = 64
GDN_WIDTH = 384
GDN_QKV = 1152
CONV_W = 4
MLA_HEADS = 6
MLA_NOPE = 64
MLA_ROPE = 32
MLA_QK = 96
MLA_Q_LORA = 384
MLA_KV_LORA = 256
ROPE_THETA = 10000.0
MLSTM_HEADS = 4
MLSTM_WIDTH = 256
MLSTM_QKV = 768
PEER_HEADS = 8
N_KEYS = 128
N_EXPERTS = N_KEYS * N_KEYS
PEER_TOPK = 16
PEER_SLOTS = PEER_HEADS * PEER_TOPK

HEAD_W = 64
LANES = 128
NEG = -1e30
VMEM_LIMIT = 56 * 1024 * 1024


def _bf(x):
    return x.astype(BF16)


def _mm(a, b):
    return jnp.dot(_bf(a), _bf(b), preferred_element_type=F32)


def _mm_nt(a, b):
    return lax.dot_general(_bf(a), _bf(b), (((1,), (1,)), ((), ())), preferred_element_type=F32)


def _split3(x):
    hi = _bf(x)
    r = x - hi.astype(F32)
    mid = _bf(r)
    lo = _bf(r - mid.astype(F32))
    return hi, mid, lo


def _mm_sel_exact(sel_bf, x):
    hi, mid, lo = _split3(x)
    d = lambda t: jnp.dot(sel_bf, t, preferred_element_type=F32)
    return d(hi) + d(mid) + d(lo)


def _sigmoid(x):
    return 1.0 / (1.0 + jnp.exp(-x))


def _silu(x):
    return x * _sigmoid(x)


def _softplus(x):
    return jnp.maximum(x, 0.0) + jnp.log1p(jnp.exp(-jnp.abs(x)))


def _iota(shape, axis):
    return lax.broadcasted_iota(jnp.int32, shape, axis)


def _pair_consts(C):
    row = _iota((C, LANES), 0)
    lane = _iota((C, LANES), 1)
    s = lane & (HEAD_W - 1)
    lo_half = lane < HEAD_W
    r2 = _iota((LANES, LANES), 0)
    l2 = _iota((LANES, LANES), 1)
    same_half = (r2 >> 6) == (l2 >> 6)
    return row, lane, s, lo_half, same_half


def _bd(y, lo_half):
    z = jnp.zeros_like(y)
    return jnp.concatenate([jnp.where(lo_half, y, z), jnp.where(lo_half, z, y)], axis=0)


def _diag_row(x, row, s):
    return jnp.sum(jnp.where(row == s, x, 0.0), axis=0, keepdims=True)


def _cparams(sem, vmem=None):
    kw = dict(dimension_semantics=sem)
    if vmem is not None:
        kw["vmem_limit_bytes"] = vmem
    return pltpu.CompilerParams(**kw)


def _mod_kernel(c_ref, w_ref, b_ref, o_ref):
    c = c_ref[...]
    o_ref[0] = _mm(_silu(c), w_ref[0]) + b_ref[0]


def _mod_call(c_all, ada_w, ada_b):
    nb = c_all.shape[0]
    nj = 6
    return pl.pallas_call(
        _mod_kernel,
        out_shape=jax.ShapeDtypeStruct((DEPTH, nb, 6 * D_MODEL), F32),
        grid=(DEPTH, nj),
        in_specs=[
            pl.BlockSpec((nb, D_MODEL), lambda l, j: (0, 0)),
            pl.BlockSpec((1, D_MODEL, D_MODEL), lambda l, j: (l, 0, j)),
            pl.BlockSpec((1, 1, D_MODEL), lambda l, j: (l, 0, j)),
        ],
        out_specs=pl.BlockSpec((1, nb, D_MODEL), lambda l, j: (l, 0, j)),
        compiler_params=_cparams(("arbitrary", "arbitrary")),
        name="adaln_mod",
    )(c_all, ada_w, ada_b.reshape(DEPTH, 1, 6 * D_MODEL))


def _inproj_kernel(x_ref, sh_ref, sc_ref, g_ref, wg_ref, wm_ref, wl_ref, og_ref, om_ref, ol_ref):
    x = x_ref[0]
    y = x * lax.rsqrt(jnp.mean(x * x, axis=-1, keepdims=True) + EPS) * g_ref[...]
    h = _bf(y * (1.0 + sc_ref[0]) + sh_ref[0])
    og_ref[0] = jnp.dot(h, wg_ref[...], preferred_element_type=F32)
    om_ref[0] = jnp.dot(h, wm_ref[...], preferred_element_type=F32)
    ol_ref[0] = jnp.dot(h, wl_ref[...], preferred_element_type=F32)


def _inproj_call(x, sh, sc, g, wg, wm, wl):
    B, L, _ = x.shape
    tb = min(L, 512)
    ng, nm, nl = wg.shape[1], wm.shape[1], wl.shape[1]
    full = lambda shape: pl.BlockSpec(shape, lambda b, i: (0,) * len(shape))
    per_b = pl.BlockSpec((1, 1, D_MODEL), lambda b, i: (b, 0, 0))
    return pl.pallas_call(
        _inproj_kernel,
        out_shape=(jax.ShapeDtypeStruct((B, L, ng), F32),
                   jax.ShapeDtypeStruct((B, L, nm), F32),
                   jax.ShapeDtypeStruct((B, L, nl), F32)),
        grid=(B, L // tb),
        in_specs=[pl.BlockSpec((1, tb, D_MODEL), lambda b, i: (b, i, 0)), per_b, per_b,
                  full((1, D_MODEL)), full(wg.shape), full(wm.shape), full(wl.shape)],
        out_specs=(pl.BlockSpec((1, tb, ng), lambda b, i: (b, i, 0)),
                   pl.BlockSpec((1, tb, nm), lambda b, i: (b, i, 0)),
                   pl.BlockSpec((1, tb, nl), lambda b, i: (b, i, 0))),
        compiler_params=_cparams(("arbitrary", "arbitrary"), VMEM_LIMIT),
        name="in_proj",
    )(x, sh, sc, g, wg, wm, wl)


_GDN_Z0 = GDN_QKV
_GDN_B0 = GDN_QKV + GDN_WIDTH
_GDN_A0 = _GDN_B0 + GDN_WIDTH
_GDN_COLS = _GDN_A0 + GDN_WIDTH
_CONV_PAD = 8


def _gdn_kernel(pg_ref, conv0_ref, s0_ref, convw_ref, alog_ref, dtb_ref, og_ref,
                o_ref, sfin_ref, convfin_ref, xp_ref, s_ref, *, C, cps, n_valid, n_steps):
    n = pl.program_id(1)
    hist = CONV_W - 1
    R = C * cps

    @pl.when(n == 0)
    def _():
        xp_ref[0:_CONV_PAD, :] = jnp.zeros((_CONV_PAD, GDN_QKV), F32)
        xp_ref[_CONV_PAD - hist:_CONV_PAD, :] = conv0_ref[0]
        s_ref[...] = s0_ref[0]

    a_pre = pg_ref[0, :, 0:GDN_QKV]
    xp_ref[_CONV_PAD:_CONV_PAD + R, :] = a_pre
    w = convw_ref[...]
    y = w[CONV_W - 1:CONV_W] * a_pre
    for j in range(CONV_W - 1):
        y = y + w[j:j + 1] * xp_ref[_CONV_PAD - hist + j:_CONV_PAD - hist + j + R, :]
    y = _silu(y)

    row, lane, s, lo_half, same_half = _pair_consts(C)
    ones_blk = jnp.where(same_half, 1.0, 0.0).astype(BF16)
    tr, tc = _iota((R, R), 0), _iota((R, R), 1)
    tri = jnp.where((tc <= tr) & ((tc >> 6) == (tr >> 6)), 1.0, 0.0).astype(BF16)
    causal = s <= row
    strict = s < row
    eye = s == row

    def hsum(x):
        return jnp.dot(_bf(x), ones_blk, preferred_element_type=F32)

    beta_all = _sigmoid(pg_ref[0, :, _GDN_B0:_GDN_A0])
    gl_all = -jnp.exp(alog_ref[...]) * _softplus(pg_ref[0, :, _GDN_A0:_GDN_COLS] + dtb_ref[...])
    if n_valid < R * n_steps:
        valid = (_iota((R, GDN_WIDTH), 0) + n * R) < n_valid
        beta_all = jnp.where(valid, beta_all, 0.0)
        gl_all = jnp.where(valid, gl_all, 0.0)
    g_all = _mm_sel_exact(tri, gl_all)

    items = [(c, j) for c in range(cps) for j in range(GDN_HEADS // 2)]
    I = range(len(items))
    blk = lambda a, base, i: a[C * items[i][0]:C * (items[i][0] + 1),
                               base + LANES * items[i][1]:base + LANES * (items[i][1] + 1)]
    bd = lambda a: _bd(a, lo_half)
    qk_raw = [jnp.concatenate([blk(y, 0, i), blk(y, GDN_WIDTH, i)], axis=0) for i in I]
    ssq = [hsum(t * t) for t in qk_raw]
    q2 = [qk_raw[i][0:C] * lax.rsqrt(ssq[i][0:C] + EPS) * (GDN_DK ** -0.5) for i in I]
    k2 = [qk_raw[i][C:2 * C] * lax.rsqrt(ssq[i][C:2 * C] + EPS) for i in I]
    v2 = [blk(y, 2 * GDN_WIDTH, i) for i in I]
    beta = [blk(beta_all, 0, i) for i in I]
    G = [blk(g_all, 0, i) for i in I]
    eG = [jnp.exp(G[i]) for i in I]
    decay = []
    for i in I:
        diff = G[i] - _diag_row(G[i], row, s)
        decay.append(jnp.where(causal, jnp.exp(jnp.where(causal, diff, 0.0)), 0.0))
    kbd = [bd(k2[i]) for i in I]
    kq = [_mm_nt(jnp.concatenate([k2[i], q2[i]], axis=0), kbd[i]) for i in I]
    A = [jnp.where(strict, beta[i] * kq[i][0:C] * decay[i], 0.0) for i in I]
    qk = [kq[i][C:2 * C] * decay[i] for i in I]
    first = ((row & 1) == 1) & (s == row - 1)
    T = [jnp.where(eye, 1.0, 0.0) - jnp.where(first, A[i], 0.0) for i in I]
    for lg in range(1, 6):
        sub = (((row >> lg) & 1) == 1) & ((s >> lg) == (row >> lg) - 1)
        t1 = [_mm(T[i], bd(jnp.where(sub, A[i], 0.0))) for i in I]
        t2 = [_mm(t1[i], bd(T[i])) for i in I]
        T = [T[i] - t2[i] for i in I]
    uw = [_mm(T[i], jnp.concatenate([bd(beta[i] * v2[i]), bd(beta[i] * eG[i] * k2[i])], axis=1)) for i in I]
    g_last = [G[i][C - 1:C, :] for i in I]
    kend_t = [(k2[i] * jnp.exp(g_last[i] - G[i])).T for i in I]
    P = range(GDN_HEADS // 2)
    S = [s_ref[j] for j in P]
    o = [None] * len(items)
    for c in range(cps):
        ids = [c * len(P) + j for j in P]
        ws = [_mm(jnp.concatenate([uw[i][:, LANES:2 * LANES], q2[i] * eG[i]], axis=0), S[j])
              for j, i in enumerate(ids)]
        wn = [uw[i][:, 0:LANES] - ws[j][0:C] for j, i in enumerate(ids)]
        upd = [_mm(kend_t[i], wn[j]) for j, i in enumerate(ids)]
        for j, i in enumerate(ids):
            o[i] = ws[j][C:2 * C] + _mm(qk[i], bd(wn[j]))
        S = [jnp.exp(g_last[i]) * S[j] + jnp.where(same_half, upd[j], 0.0) for j, i in enumerate(ids)]
    for j in P:
        s_ref[j] = S[j]
    oss = [hsum(o[i] * o[i]) for i in I]
    for i in I:
        c, j = items[i]
        on = o[i] * lax.rsqrt(oss[i] * (1.0 / HEAD_W) + EPS) * og_ref[...]
        z = pg_ref[0, C * c:C * (c + 1), _GDN_Z0 + LANES * j:_GDN_Z0 + LANES * (j + 1)]
        o_ref[0, C * c:C * (c + 1), LANES * j:LANES * (j + 1)] = on * _silu(z)

    @pl.when(n < n_steps - 1)
    def _():
        xp_ref[_CONV_PAD - hist:_CONV_PAD, :] = xp_ref[_CONV_PAD + R - hist:_CONV_PAD + R, :]

    @pl.when(n == n_steps - 1)
    def _():
        lv = n_valid - (n_steps - 1) * R
        convfin_ref[0] = xp_ref[_CONV_PAD + lv - hist:_CONV_PAD + lv, :]
        sfin_ref[0] = s_ref[...]


def _gdn_call(pg, conv0, s0_bd, conv_w, alog_rep, dtb_rep, og_rep, n_valid):
    B, Lp, _ = pg.shape
    C = CHUNK
    n_chunks = Lp // C
    cps = 4 if n_chunks % 4 == 0 else (2 if n_chunks % 2 == 0 else 1)
    n_steps = n_chunks // cps
    R = C * cps
    npair = GDN_HEADS // 2
    kern = functools.partial(_gdn_kernel, C=C, cps=cps, n_valid=n_valid, n_steps=n_steps)
    full = lambda shape: pl.BlockSpec(shape, lambda b, n: (0,) * len(shape))
    return pl.pallas_call(
        kern,
        out_shape=(jax.ShapeDtypeStruct((B, Lp, GDN_WIDTH), F32),
                   jax.ShapeDtypeStruct((B, npair, LANES, LANES), F32),
                   jax.ShapeDtypeStruct((B, CONV_W - 1, GDN_QKV), F32)),
        grid=(B, n_steps),
        in_specs=[pl.BlockSpec((1, R, _GDN_COLS), lambda b, n: (b, n, 0)),
                  pl.BlockSpec((1, CONV_W - 1, GDN_QKV), lambda b, n: (b, 0, 0)),
                  pl.BlockSpec((1, npair, LANES, LANES), lambda b, n: (b, 0, 0, 0)),
                  full((CONV_W, GDN_QKV)), full((1, GDN_WIDTH)), full((1, GDN_WIDTH)), full((1, LANES))],
        out_specs=(pl.BlockSpec((1, R, GDN_WIDTH), lambda b, n: (b, n, 0)),
                   pl.BlockSpec((1, npair, LANES, LANES), lambda b, n: (b, 0, 0, 0)),
                   pl.BlockSpec((1, CONV_W - 1, GDN_QKV), lambda b, n: (b, 0, 0))),
        scratch_shapes=[pltpu.VMEM((_CONV_PAD + R, GDN_QKV), F32),
                        pltpu.VMEM((npair, LANES, LANES), F32)],
        compiler_params=_cparams(("arbitrary", "arbitrary")),
        name="gdn_chunks",
    )(pg, conv0, s0_bd, conv_w, alog_rep, dtb_rep, og_rep)


_ML_O0 = MLSTM_QKV
_ML_I0 = _ML_O0 + MLSTM_WIDTH
_ML_F0 = _ML_I0 + MLSTM_WIDTH
_ML_COLS = _ML_F0 + MLSTM_WIDTH


def _mlstm_kernel(pm_ref, c0_ref, n0_ref, m0_ref, ib_ref, fb_ref, og_ref,
                  o_ref, cfin_ref, nfin_ref, mfin_ref, c_ref, n_ref, m_ref, *, C, cps, n_valid, n_steps):
    n = pl.program_id(1)
    R = C * cps

    @pl.when(n == 0)
    def _():
        c_ref[...] = c0_ref[0]
        n_ref[...] = n0_ref[0]
        m_ref[...] = m0_ref[0]

    row, lane, s, lo_half, same_half = _pair_consts(C)
    ones_blk = jnp.where(same_half, 1.0, 0.0).astype(BF16)
    tr, tc = _iota((R, R), 0), _iota((R, R), 1)
    tri = jnp.where((tc <= tr) & ((tc >> 6) == (tr >> 6)), 1.0, 0.0).astype(BF16)
    causal = s <= row

    def hsum(x):
        return jnp.dot(_bf(x), ones_blk, preferred_element_type=F32)

    ig_all = pm_ref[0, :, _ML_I0:_ML_F0] + ib_ref[...]
    lf_all = -_softplus(-(pm_ref[0, :, _ML_F0:_ML_COLS] + fb_ref[...]))
    if n_valid < R * n_steps:
        valid = (_iota((R, MLSTM_WIDTH), 0) + n * R) < n_valid
        ig_all = jnp.where(valid, ig_all, NEG)
        lf_all = jnp.where(valid, lf_all, 0.0)
    f_all = _mm_sel_exact(tri, lf_all)

    P = range(MLSTM_HEADS // 2)
    items = [(c, j) for c in range(cps) for j in P]
    I = range(len(items))
    col = lambda base, i: pm_ref[0, C * items[i][0]:C * (items[i][0] + 1),
                                 base + LANES * items[i][1]:base + LANES * (items[i][1] + 1)]
    sub = lambda a, i: a[C * items[i][0]:C * (items[i][0] + 1), LANES * items[i][1]:LANES * (items[i][1] + 1)]
    bd = lambda a: _bd(a, lo_half)
    q2 = [col(0, i) for i in I]
    k2 = [col(MLSTM_WIDTH, i) * (HEAD_W ** -0.5) for i in I]
    v2 = [col(2 * MLSTM_WIDTH, i) for i in I]
    ig = [sub(ig_all, i) for i in I]
    F = [sub(f_all, i) for i in I]
    dm, dmax = [], []
    for i in I:
        d = jnp.where(causal, F[i] - _diag_row(F[i], row, s) + _diag_row(ig[i], row, s), NEG)
        d_e = jnp.max(jnp.where(lo_half, d, NEG), axis=1, keepdims=True)
        d_o = jnp.max(jnp.where(lo_half, NEG, d), axis=1, keepdims=True)
        dm.append(d)
        dmax.append(jnp.where(lo_half, d_e, d_o))
    qk = [_mm_nt(q2[i], bd(k2[i])) for i in I]
    fe = [F[i][C - 1:C, :] for i in I]
    se = [fe[i] - F[i] + ig[i] for i in I]
    se_max = [jnp.max(se[i], axis=0, keepdims=True) for i in I]
    ms, m_new = [None] * len(items), [None] * len(items)
    m_run = [m_ref[j] for j in P]
    for i in I:
        j = items[i][1]
        ms[i] = m_run[j]
        m_new[i] = jnp.maximum(fe[i] + ms[i], se_max[i])
        m_run[j] = m_new[i]
    mt = [jnp.maximum(F[i] + ms[i], dmax[i]) for i in I]
    inter = [jnp.exp(F[i] + ms[i] - mt[i]) for i in I]
    wgt = [jnp.exp(dm[i] - mt[i]) * qk[i] for i in I]
    wv = [_mm(wgt[i], bd(v2[i])) for i in I]
    wsum = [hsum(wgt[i]) for i in I]
    kw = [k2[i] * jnp.exp(se[i] - m_new[i]) for i in I]
    upd = [_mm(kw[i].T, v2[i]) for i in I]
    ksum = [jnp.sum(kw[i], axis=0, keepdims=True) for i in I]
    sc = [jnp.exp(fe[i] + ms[i] - m_new[i]) for i in I]
    cs = [c_ref[j] for j in P]
    ns = [n_ref[j] for j in P]
    o = [None] * len(items)
    for i in I:
        j = items[i][1]
        den = inter[i] * hsum(q2[i] * ns[j]) + wsum[i]
        h = (inter[i] * _mm(q2[i], cs[j]) + wv[i]) / jnp.maximum(jnp.abs(den), jnp.exp(-mt[i]))
        o[i] = _sigmoid(col(_ML_O0, i)) * h
        cs[j] = sc[i] * cs[j] + jnp.where(same_half, upd[i], 0.0)
        ns[j] = sc[i] * ns[j] + ksum[i]
    oss = [hsum(o[i] * o[i]) for i in I]
    for j in P:
        c_ref[j] = cs[j]
        n_ref[j] = ns[j]
        m_ref[j] = m_run[j]
    for i in I:
        c, j = items[i]
        o_ref[0, C * c:C * (c + 1), LANES * j:LANES * (j + 1)] = (
            o[i] * lax.rsqrt(oss[i] * (1.0 / HEAD_W) + EPS) * og_ref[:, LANES * j:LANES * (j + 1)])

    @pl.when(n == n_steps - 1)
    def _():
        cfin_ref[0] = c_ref[...]
        nfin_ref[0] = n_ref[...]
        mfin_ref[0] = m_ref[...]


def _mlstm_call(pm, c0_bd, n0, m0_rep, ib_rep, fb_rep, og, n_valid):
    B, Lp, _ = pm.shape
    C = CHUNK
    n_chunks = Lp // C
    cps = 4 if n_chunks % 4 == 0 else (2 if n_chunks % 2 == 0 else 1)
    n_steps = n_chunks // cps
    R = C * cps
    npair = MLSTM_HEADS // 2
    kern = functools.partial(_mlstm_kernel, C=C, cps=cps, n_valid=n_valid, n_steps=n_steps)
    full = lambda shape: pl.BlockSpec(shape, lambda b, n: (0,) * len(shape))
    st_c = pl.BlockSpec((1, npair, LANES, LANES), lambda b, n: (b, 0, 0, 0))
    st_v = pl.BlockSpec((1, npair, 1, LANES), lambda b, n: (b, 0, 0, 0))
    return pl.pallas_call(
        kern,
        out_shape=(jax.ShapeDtypeStruct((B, Lp, MLSTM_WIDTH), F32),
                   jax.ShapeDtypeStruct((B, npair, LANES, LANES), F32),
                   jax.ShapeDtypeStruct((B, npair, 1, LANES), F32),
                   jax.ShapeDtypeStruct((B, npair, 1, LANES), F32)),
        grid=(B, n_steps),
        in_specs=[pl.BlockSpec((1, R, _ML_COLS), lambda b, n: (b, n, 0)), st_c, st_v, st_v,
                  full((1, MLSTM_WIDTH)), full((1, MLSTM_WIDTH)), full((1, MLSTM_WIDTH))],
        out_specs=(pl.BlockSpec((1, R, MLSTM_WIDTH), lambda b, n: (b, n, 0)), st_c, st_v, st_v),
        scratch_shapes=[pltpu.VMEM((npair, LANES, LANES), F32),
                        pltpu.VMEM((npair, 1, LANES), F32),
                        pltpu.VMEM((npair, 1, LANES), F32)],
        compiler_params=_cparams(("arbitrary", "arbitrary")),
        name="mlstm_chunks",
    )(pm, c0_bd, n0, m0_rep, ib_rep, fb_rep, og)


_MLA_C0 = MLA_Q_LORA
_MLA_R0 = MLA_Q_LORA + MLA_KV_LORA
_MLA_COLS = _MLA_R0 + LANES
_MLA_HW = LANES
_Q_SCALE = MLA_QK ** -0.5 * math.log2(math.e)


def _swap_halves(x, lane, base, half):
    up = pltpu.roll(x, LANES - half, 1)
    dn = pltpu.roll(x, half, 1)
    first = (lane >= base) & (lane < base + half)
    second = (lane >= base + half) & (lane < base + 2 * half)
    return jnp.where(first, up, jnp.where(second, dn, 0.0))


def _mla_q_kernel(pm_ref, qag_ref, wuq_ref, qgain_ref, kvag_ref, krgain_ref,
                  cosq_ref, sinq_ref, cosk_ref, sink_ref, q_ref, lat_ref, kr_ref, *, tb):
    lane = _iota((tb, LANES), 1)
    ql = pm_ref[0, :, 0:_MLA_C0]
    ql = ql * lax.rsqrt(jnp.mean(ql * ql, axis=-1, keepdims=True) + EPS) * qag_ref[...]
    q = jnp.dot(_bf(ql), wuq_ref[...], preferred_element_type=F32)
    nope = lane < MLA_NOPE
    rope = (lane >= MLA_NOPE) & (lane < MLA_QK)
    cosq = cosq_ref[...]
    sinq = sinq_ref[...]
    for h in range(MLA_HEADS):
        qh = q[:, _MLA_HW * h:_MLA_HW * (h + 1)]
        sq = qh * qh
        ss_n = jnp.sum(jnp.where(nope, sq, 0.0), axis=-1, keepdims=True) * (1.0 / MLA_NOPE)
        ss_r = jnp.sum(jnp.where(rope, sq, 0.0), axis=-1, keepdims=True) * (1.0 / MLA_ROPE)
        rinv = jnp.where(nope, lax.rsqrt(ss_n + EPS), lax.rsqrt(ss_r + EPS))
        qn = qh * rinv * qgain_ref[...]
        sw = _swap_halves(qn, lane, MLA_NOPE, MLA_ROPE // 2)
        qr = jnp.where(nope, qn, qn * cosq + sw * sinq)
        q_ref[0, :, _MLA_HW * h:_MLA_HW * (h + 1)] = _bf(qr * _Q_SCALE)
    ckv = pm_ref[0, :, _MLA_C0:_MLA_R0]
    lat_ref[0] = ckv * lax.rsqrt(jnp.mean(ckv * ckv, axis=-1, keepdims=True) + EPS) * kvag_ref[...]
    kx = pm_ref[0, :, _MLA_R0:_MLA_COLS]
    ssk = jnp.sum(kx * kx, axis=-1, keepdims=True) * (1.0 / MLA_ROPE)
    kn = kx * lax.rsqrt(ssk + EPS) * krgain_ref[...]
    swk = _swap_halves(kn, lane, 0, MLA_ROPE // 2)
    kr = kn * cosk_ref[...] + swk * sink_ref[...]
    kr_ref[0] = kr[:, 0:MLA_ROPE]


def _mla_q_call(pm, qag, wuq, qgain, kvag, krgain, cosq, sinq, cosk, sink):
    B, L, _ = pm.shape
    tb = min(L, 512)
    full = lambda shape: pl.BlockSpec(shape, lambda b, i: (0,) * len(shape))
    tab = pl.BlockSpec((tb, LANES), lambda b, i: (i, 0))
    nq = MLA_HEADS * _MLA_HW
    return pl.pallas_call(
        functools.partial(_mla_q_kernel, tb=tb),
        out_shape=(jax.ShapeDtypeStruct((B, L, nq), BF16),
                   jax.ShapeDtypeStruct((B, L, MLA_KV_LORA), F32),
                   jax.ShapeDtypeStruct((B, L, MLA_ROPE), F32)),
        grid=(B, L // tb),
        in_specs=[pl.BlockSpec((1, tb, _MLA_COLS), lambda b, i: (b, i, 0)),
                  full((1, MLA_Q_LORA)), full((MLA_Q_LORA, nq)), full((1, LANES)),
                  full((1, MLA_KV_LORA)), full((1, LANES)), tab, tab, tab, tab],
        out_specs=(pl.BlockSpec((1, tb, nq), lambda b, i: (b, i, 0)),
                   pl.BlockSpec((1, tb, MLA_KV_LORA), lambda b, i: (b, i, 0)),
                   pl.BlockSpec((1, tb, MLA_ROPE), lambda b, i: (b, i, 0))),
        compiler_params=_cparams(("arbitrary", "arbitrary")),
        name="mla_q_proj",
    )(pm, qag, wuq, qgain, kvag, krgain, cosq, sinq, cosk, sink)


def _mla_kv_kernel(ckv_ref, krp_ref, wukv_ref, kgain_ref, k_ref, kv_ref, *, tb):
    lane = _iota((tb, LANES), 1)
    nope = lane < MLA_NOPE
    kv = jnp.dot(_bf(ckv_ref[0]), wukv_ref[...], preferred_element_type=F32)
    krp = krp_ref[0]
    for h in range(MLA_HEADS):
        g = kv[:, _MLA_HW * h:_MLA_HW * (h + 1)]
        ss = jnp.sum(jnp.where(nope, g * g, 0.0), axis=-1, keepdims=True) * (1.0 / MLA_NOPE)
        k_ref[0, :, _MLA_HW * h:_MLA_HW * (h + 1)] = _bf(g * lax.rsqrt(ss + EPS) * kgain_ref[...] + krp)
        kv_ref[0, :, _MLA_HW * h:_MLA_HW * (h + 1)] = _bf(jnp.where(nope, 1.0, g))


def _mla_kv_call(ckv_all, krp_all, wukv, kgain):
    B, Lk, _ = ckv_all.shape
    tb = 512 if Lk % 512 == 0 else Lk
    nk = MLA_HEADS * _MLA_HW
    full = lambda shape: pl.BlockSpec(shape, lambda b, i: (0,) * len(shape))
    return pl.pallas_call(
        functools.partial(_mla_kv_kernel, tb=tb),
        out_shape=(jax.ShapeDtypeStruct((B, Lk, nk), BF16), jax.ShapeDtypeStruct((B, Lk, nk), BF16)),
        grid=(B, Lk // tb),
        in_specs=[pl.BlockSpec((1, tb, MLA_KV_LORA), lambda b, i: (b, i, 0)),
                  pl.BlockSpec((1, tb, LANES), lambda b, i: (b, i, 0)),
                  full((MLA_KV_LORA, nk)), full((1, LANES))],
        out_specs=(pl.BlockSpec((1, tb, nk), lambda b, i: (b, i, 0)),
                   pl.BlockSpec((1, tb, nk), lambda b, i: (b, i, 0))),
        compiler_params=_cparams(("arbitrary", "arbitrary"), VMEM_LIMIT),
        name="mla_kv_proj",
    )(ckv_all, krp_all, wukv, kgain)


def _attn_kernel(q_ref, k_ref, kv_ref, o_ref, m_ref, acc_ref, *, tq, tk, nk, causal):
    qi = pl.program_id(2)
    ki = pl.program_id(3)
    last = _last_kv_block(qi, tq, tk) if causal else nk - 1
    first_diag = (qi * tq) // tk

    @pl.when(ki == 0)
    def _():
        m_ref[...] = jnp.full(m_ref.shape, NEG, F32)
        acc_ref[...] = jnp.zeros(acc_ref.shape, F32)

    if tk % LANES == 0:
        widen = lambda v: jnp.concatenate([v] * (tk // LANES), axis=1)
    else:
        widen = lambda v: v[:, 0:1]

    def step(masked):
        if masked:
            qc = (qi * tq + _iota((tq, tk), 0)) >> 6
            kc = (ki * tk + _iota((tq, tk), 1)) >> 6
            keep = kc <= qc
        H = range(2)
        hs = [slice(_MLA_HW * hh, _MLA_HW * (hh + 1)) for hh in H]
        sc = [lax.dot_general(q_ref[0, :, hs[hh]], k_ref[0, :, hs[hh]], (((1,), (1,)), ((), ())),
                              preferred_element_type=F32) for hh in H]
        if masked:
            sc = [jnp.where(keep, t, NEG) for t in sc]
        m_prev = [m_ref[hh] for hh in H]
        m_new = [jnp.maximum(m_prev[hh], jnp.max(sc[hh], axis=-1, keepdims=True)) for hh in H]
        p = [jnp.exp2(sc[hh] - widen(m_new[hh])) for hh in H]
        alpha = [jnp.exp2(m_prev[hh] - m_new[hh]) for hh in H]
        pv = [jnp.dot(_bf(p[hh]), kv_ref[0, :, hs[hh]], preferred_element_type=F32) for hh in H]
        for hh in H:
            acc_ref[hh] = alpha[hh] * acc_ref[hh] + pv[hh]
            m_ref[hh] = m_new[hh]

    if causal:
        pl.when(ki < first_diag)(lambda: step(False))
        pl.when((ki >= first_diag) & (ki <= last))(lambda: step(True))
    else:
        step(False)

    @pl.when(ki == last)
    def _():
        lane = _iota((tq, LANES), 1)
        a_e, a_o = acc_ref[0], acc_ref[1]
        o_e = pltpu.roll(a_e, MLA_NOPE, 1) / a_e
        o_o = a_o / pltpu.roll(a_o, MLA_NOPE, 1)
        o_ref[0] = jnp.where(lane < MLA_NOPE, o_e, o_o)


def _last_kv_block(qi, tq, tk):
    return ((qi + 1) * tq - 1) // tk


def _attn_call(q, k, kv, causal):
    B, Lq, _ = q.shape
    Lk = k.shape[1]
    if causal:
        tq = min(Lq, 1024)
        tk = min(Lk, 1024)
    else:
        tq, tk = Lq, Lk
    nq, nk = Lq // tq, Lk // tk
    w2 = 2 * _MLA_HW
    if causal:
        kmap = lambda b, hp, qi, ki: (b, jnp.minimum(ki, _last_kv_block(qi, tq, tk)), hp)
    else:
        kmap = lambda b, hp, qi, ki: (b, ki, hp)
    return pl.pallas_call(
        functools.partial(_attn_kernel, tq=tq, tk=tk, nk=nk, causal=causal),
        out_shape=jax.ShapeDtypeStruct((B, Lq, MLA_HEADS * MLA_NOPE), F32),
        grid=(B, MLA_HEADS // 2, nq, nk),
        in_specs=[pl.BlockSpec((1, tq, w2), lambda b, hp, qi, ki: (b, qi, hp)),
                  pl.BlockSpec((1, tk, w2), kmap),
                  pl.BlockSpec((1, tk, w2), kmap)],
        out_specs=pl.BlockSpec((1, tq, LANES), lambda b, hp, qi, ki: (b, qi, hp)),
        scratch_shapes=[pltpu.VMEM((2, tq, LANES), F32), pltpu.VMEM((2, tq, LANES), F32)],
        compiler_params=_cparams(("arbitrary",) * 4, VMEM_LIMIT),
        name="mla_attention",
    )(q, k, kv)


def _outproj_kernel(a_ref, b_ref, c_ref, x_ref, g1_ref, sh_ref, sc_ref, ng_ref, wa_ref, wb_ref, wc_ref,
                    x1_ref, h2_ref):
    y = (jnp.dot(_bf(a_ref[0]), wa_ref[...], preferred_element_type=F32)
         + jnp.dot(_bf(b_ref[0]), wb_ref[...], preferred_element_type=F32)
         + jnp.dot(_bf(c_ref[0]), wc_ref[...], preferred_element_type=F32))
    x1 = x_ref[0] + g1_ref[0] * y
    x1_ref[0] = x1
    n = x1 * lax.rsqrt(jnp.mean(x1 * x1, axis=-1, keepdims=True) + EPS) * ng_ref[...]
    h2_ref[0] = n * (1.0 + sc_ref[0]) + sh_ref[0]


def _outproj_call(oa, ob, oc, x, g1, sh2, sc2, ng, wa, wb, wc):
    B, L, _ = x.shape
    tb = min(L, 512)
    full = lambda shape: pl.BlockSpec(shape, lambda b, i: (0,) * len(shape))
    per_b = pl.BlockSpec((1, 1, D_MODEL), lambda b, i: (b, 0, 0))
    blk = lambda wdt: pl.BlockSpec((1, tb, wdt), lambda b, i: (b, i, 0))
    return pl.pallas_call(
        _outproj_kernel,
        out_shape=(jax.ShapeDtypeStruct((B, L, D_MODEL), F32), jax.ShapeDtypeStruct((B, L, D_MODEL), F32)),
        grid=(B, L // tb),
        in_specs=[blk(GDN_WIDTH), blk(MLA_HEADS * MLA_NOPE), blk(MLSTM_WIDTH), blk(D_MODEL),
                  per_b, per_b, per_b, full((1, D_MODEL)), full(wa.shape), full(wb.shape), full(wc.shape)],
        out_specs=(blk(D_MODEL), blk(D_MODEL)),
        compiler_params=_cparams(("arbitrary", "arbitrary"), VMEM_LIMIT),
        name="out_proj",
    )(oa, ob, oc, x, g1, sh2, sc2, ng, wa, wb, wc)


def _topk_rows(svs, io, k, payloads=None):
    svs = list(svs)
    vals = [[] for _ in svs]
    outs = [[] for _ in svs]
    big = jnp.float32(1e9)
    for _ in range(k):
        for a in range(len(svs)):
            m = jnp.max(svs[a], axis=0, keepdims=True)
            ix = jnp.min(jnp.where(svs[a] == m, io, big), axis=0, keepdims=True)
            hit = io == ix
            svs[a] = jnp.where(hit, -jnp.inf, svs[a])
            vals[a].append(m)
            outs[a].append(ix if payloads is None
                           else jnp.max(jnp.where(hit, payloads[a], -1.0), axis=0, keepdims=True))
    return [(jnp.concatenate(v, axis=0), jnp.concatenate(o, axis=0)) for v, o in zip(vals, outs)]


_CAND_ROWS = PEER_TOPK + 7 * 8 + 8


def _cand_blocks(t1, t2, combine):
    blocks = [combine(t1[0:1], t2)]
    blocks += [combine(t1[a:a + 1], t2[0:8]) for a in range(1, 8)]
    blocks.append(combine(t1[8:PEER_TOPK], t2[0:1]))
    return jnp.concatenate(blocks, axis=0)


def _route_kernel(h_ref, wq_ref, keys_ref, e_ref, g_ref, qs_ref, es_ref, gs_ref, *, tb):
    q = jnp.dot(_bf(h_ref[...]), wq_ref[...], preferred_element_type=F32)
    for g in range(2 * PEER_HEADS):
        qs_ref[g] = _bf(q[:, LANES * g:LANES * (g + 1)])
    io_k = _iota((N_KEYS, tb), 0).astype(F32)
    r = _iota((_CAND_ROWS, tb), 0)
    mid = r - PEER_TOPK
    io_c = jnp.where(r < PEER_TOPK, r,
                     jnp.where(r < _CAND_ROWS - 8, ((mid >> 3) + 1) * PEER_TOPK + (mid & 7),
                               (r - (_CAND_ROWS - 16)) * PEER_TOPK)).astype(F32)

    def head(h, carry):
        scores = [lax.dot_general(keys_ref[2 * h + p], qs_ref[2 * h + p], (((1,), (1,)), ((), ())),
                                  preferred_element_type=F32) for p in range(2)]
        (v1, i1), (v2, i2) = _topk_rows(scores, io_k, PEER_TOPK)
        cand = _cand_blocks(v1, v2, lambda x, y: x + y)
        expert = _cand_blocks(i1, i2, lambda x, y: x * N_KEYS + y)
        (sc, e), = _topk_rows([cand], io_c, PEER_TOPK, payloads=[expert])
        ex = jnp.exp(sc - jnp.max(sc, axis=0, keepdims=True))
        es_ref[h] = e * _ROW_SUB
        gs_ref[h] = ex / jnp.sum(ex, axis=0, keepdims=True)
        return carry

    lax.fori_loop(0, PEER_HEADS, head, 0)
    e_ref[...] = es_ref[...].reshape(PEER_SLOTS, tb).T.astype(jnp.int32)
    g_ref[...] = gs_ref[...].reshape(PEER_SLOTS, tb).T


def _route_call(h2, wq, keys):
    T = h2.shape[0]
    tb = min(T, 256)
    full = lambda shape: pl.BlockSpec(shape, lambda i: (0,) * len(shape))
    return pl.pallas_call(
        functools.partial(_route_kernel, tb=tb),
        out_shape=(jax.ShapeDtypeStruct((T, PEER_SLOTS), jnp.int32), jax.ShapeDtypeStruct((T, PEER_SLOTS), F32)),
        grid=(T // tb,),
        in_specs=[pl.BlockSpec((tb, D_MODEL), lambda i: (i, 0)), full(wq.shape), full(keys.shape)],
        out_specs=(pl.BlockSpec((tb, PEER_SLOTS), lambda i: (i, 0)), pl.BlockSpec((tb, PEER_SLOTS), lambda i: (i, 0))),
        scratch_shapes=[pltpu.VMEM((2 * PEER_HEADS, tb, LANES), BF16),
                        pltpu.VMEM((PEER_HEADS, PEER_TOPK, tb), F32),
                        pltpu.VMEM((PEER_HEADS, PEER_TOPK, tb), F32)],
        compiler_params=_cparams(("arbitrary",), VMEM_LIMIT),
        name="peer_route",
    )(h2, wq, keys)


_ROW_SUB = 4
_GROUP = 8
_BITREV3 = (0, 4, 2, 6, 1, 5, 3, 7)


def _expert_row(tab_ref, row0):
    return pltpu.bitcast(tab_ref[pl.ds(row0, _ROW_SUB), :], BF16).astype(F32)


def _gelu(x):
    return 0.5 * x * (1.0 + lax.erf(x * (2.0 ** -0.5)))


def _fold(a, b, h, sub):
    m = (sub & h) == 0
    if h == 4:
        return jnp.where(m, a, b) + pltpu.roll(jnp.where(m, b, a), 4, 0)
    return jnp.where(m, a + pltpu.roll(a, 8 - h, 0), b + pltpu.roll(b, h, 0))


def _transpose8(v, sub):
    v = list(v)
    for h in (4, 2, 1):
        m = (sub & h) == 0
        for i in range(8):
            if i & h == 0:
                a, b = v[i], v[i | h]
                v[i] = jnp.where(m, a, pltpu.roll(b, h, 0))
                v[i | h] = jnp.where(m, pltpu.roll(a, 8 - h, 0), b)
    return v


def _rows_to_dense(x_ref, dense_ref, tb, sub):
    for g in range(tb // 8):
        cols = [x_ref[8 * g:8 * (g + 1), LANES * c:LANES * (c + 1)] for c in range(8)]
        for k, d in enumerate(_transpose8(cols, sub)):
            dense_ref[8 * g + k] = d


def _dense_to_rows(dense_ref, o_ref, tb, sub):
    for g in range(tb // 8):
        cols = _transpose8([dense_ref[8 * g + k] for k in range(8)], sub)
        for c in range(8):
            o_ref[8 * g:8 * (g + 1), LANES * c:LANES * (c + 1)] = cols[c]


def _peer_u_kernel(idx_ref, xrow_ref, g_ref, tab_ref, c_ref, part_ref, x_ref, *, tb):
    ones8 = jnp.ones((8, LANES), BF16)
    sub = _iota((8, LANES), 0)
    nt = lambda a, b: lax.dot_general(a, b, (((1,), (1,)), ((), ())), preferred_element_type=F32)
    _rows_to_dense(xrow_ref, x_ref, tb, sub)

    def gather(t, slot, k):
        xv = x_ref[t]
        buf = part_ref.at[slot]
        for g in range(PEER_SLOTS // _GROUP):
            p = [_expert_row(tab_ref, idx_ref[t, _GROUP * g + _BITREV3[i]]) * xv for i in range(_GROUP)]
            z = [_fold(p[2 * i], p[2 * i + 1], 4, sub) for i in range(4)]
            w = [_fold(z[0], z[1], 2, sub), _fold(z[2], z[3], 2, sub)]
            r0 = PEER_SLOTS * k + _GROUP * g
            buf[r0:r0 + _GROUP, :] = _fold(w[0], w[1], 1, sub)

    def finish(grp, slot):
        res = nt(ones8, _bf(part_ref[slot]))
        act = res[:, 0:LANES]
        for k in range(1, 8):
            act = jnp.where(sub == k, res[:, LANES * k:LANES * (k + 1)], act)
        rows = pl.ds(pl.multiple_of(grp * 8, 8), 8)
        c_ref[rows, :] = g_ref[rows, :] * _gelu(act)

    for k in range(8):
        gather(k, 0, k)

    def trip(grp, carry):
        slot = grp & 1
        finish(grp - 1, 1 - slot)
        for k in range(8):
            gather(grp * 8 + k, slot, k)
        return carry

    n_grp = tb // 8
    lax.fori_loop(1, n_grp, trip, 0)
    finish(n_grp - 1, (n_grp - 1) & 1)


def _peer_v_kernel(idx_ref, c_ref, xrow_ref, g2_ref, tab_ref, o_ref, m_ref, x_ref, *, tb):
    sub = _iota((8, LANES), 0)
    _rows_to_dense(xrow_ref, x_ref, tb, sub)
    eye = _iota((PEER_SLOTS, LANES), 0) == _iota((PEER_SLOTS, LANES), 1)
    diag = jnp.where(eye[None], c_ref[...][:, None, :], 0.0).reshape(tb * PEER_SLOTS, LANES)
    rep = jnp.dot(_bf(diag), jnp.ones((LANES, LANES), BF16), preferred_element_type=F32)
    m_ref[...] = rep.reshape(tb, PEER_SLOTS, LANES)

    def tok(t):
        y = jnp.zeros((8, LANES), F32)
        base = t * PEER_SLOTS
        for g in range(PEER_SLOTS // _GROUP):
            rows = [idx_ref[base + k] for k in range(_GROUP)]
            r = [m_ref[t, pl.ds(_GROUP * g + k, 1), :] * _expert_row(tab_ref, rows[k]) for k in range(_GROUP)]
            y = y + (((r[0] + r[1]) + (r[2] + r[3])) + ((r[4] + r[5]) + (r[6] + r[7])))
            base = base + _GROUP + lax.shift_right_arithmetic(rows[0], jnp.int32(31))
        x_ref[t] = x_ref[t] + g2_ref[0] * y

    per_trip = 2

    def trip(i, carry):
        for k in range(per_trip):
            tok(i * per_trip + k)
        return carry

    lax.fori_loop(0, tb // per_trip, trip, 0)
    _dense_to_rows(x_ref, o_ref, tb, sub)


def _table_spec():
    return pl.BlockSpec((N_EXPERTS * _ROW_SUB, LANES), lambda i: (0, 0), pipeline_mode=pl.Buffered(1))


def _peer_u_call(idx, x, gate, tab, tb):
    T = idx.shape[0]
    smem = pl.BlockSpec((tb, PEER_SLOTS), lambda i: (i, 0), memory_space=pltpu.SMEM)
    return pl.pallas_call(
        functools.partial(_peer_u_kernel, tb=tb),
        out_shape=jax.ShapeDtypeStruct((T, PEER_SLOTS), F32),
        grid=(T // tb,),
        in_specs=[smem, pl.BlockSpec((tb, D_MODEL), lambda i: (i, 0)),
                  pl.BlockSpec((tb, PEER_SLOTS), lambda i: (i, 0)), _table_spec()],
        out_specs=pl.BlockSpec((tb, PEER_SLOTS), lambda i: (i, 0)),
        scratch_shapes=[pltpu.VMEM((2, 8 * PEER_SLOTS, LANES), F32), pltpu.VMEM((tb, 8, LANES), F32)],
        compiler_params=_cparams(("arbitrary",), VMEM_LIMIT),
        name="peer_u",
    )(idx, x, gate, tab)


def _peer_v_call(idx, coef, x, g2, tab, tb, L):
    T = idx.shape[0]
    smem = pl.BlockSpec((tb * PEER_SLOTS,), lambda i: (i,), memory_space=pltpu.SMEM)
    per_step = L // tb
    idx = idx.reshape(T * PEER_SLOTS)
    return pl.pallas_call(
        functools.partial(_peer_v_kernel, tb=tb),
        out_shape=jax.ShapeDtypeStruct((T, D_MODEL), F32),
        grid=(T // tb,),
        in_specs=[smem, pl.BlockSpec((tb, PEER_SLOTS), lambda i: (i, 0)),
                  pl.BlockSpec((tb, D_MODEL), lambda i: (i, 0)),
                  pl.BlockSpec((1, 8, LANES), lambda i: (i // per_step, 0, 0)), _table_spec()],
        out_specs=pl.BlockSpec((tb, D_MODEL), lambda i: (i, 0)),
        scratch_shapes=[pltpu.VMEM((tb, PEER_SLOTS, LANES), F32), pltpu.VMEM((tb, 8, LANES), F32)],
        compiler_params=_cparams(("arbitrary",), VMEM_LIMIT),
        name="peer_v",
    )(idx, coef, x, g2, tab)


def _pack_table(tab):
    bits = lax.bitcast_convert_type(tab.astype(BF16), jnp.uint16).astype(jnp.uint32)
    bits = bits.reshape(tab.shape[0], _ROW_SUB, 2, LANES)
    return (bits[:, :, 0, :] | (bits[:, :, 1, :] << 16)).reshape(tab.shape[0] * _ROW_SUB, LANES)


def _rep_heads(w, width=HEAD_W):
    return jnp.repeat(w, width, axis=-1)


def _to_bd(s):
    B, H = s.shape[:2]
    s = s.reshape(B, H // 2, 2, HEAD_W, HEAD_W)
    z = jnp.zeros_like(s[:, :, 0])
    top = jnp.concatenate([s[:, :, 0], z], axis=-1)
    bot = jnp.concatenate([z, s[:, :, 1]], axis=-1)
    return jnp.concatenate([top, bot], axis=-2)


def _from_bd(s):
    B, P = s.shape[:2]
    return jnp.stack([s[:, :, :HEAD_W, :HEAD_W], s[:, :, HEAD_W:, HEAD_W:]], axis=2).reshape(B, 2 * P, HEAD_W, HEAD_W)


def _prep_layer(l, w):
    o = [0]
    for sz in (GDN_QKV, GDN_WIDTH, GDN_HEADS, GDN_HEADS, MLA_Q_LORA, MLA_KV_LORA, MLA_ROPE,
               MLSTM_QKV, MLSTM_WIDTH, MLSTM_HEADS, MLSTM_HEADS):
        o.append(o[-1] + sz)
    wi = w['w_in'][l]
    col = lambda i: wi[:, o[i]:o[i + 1]]
    p = {}
    p['w_gdn'] = _bf(jnp.concatenate([col(0), col(1), _rep_heads(col(2)), _rep_heads(col(3))], axis=1))
    p['w_mla'] = _bf(jnp.concatenate([col(4), col(5), col(6), jnp.zeros((D_MODEL, LANES - MLA_ROPE), F32)], axis=1))
    p['w_mls'] = _bf(jnp.concatenate([col(7), col(8), _rep_heads(col(9)), _rep_heads(col(10))], axis=1))
    p['alog'] = _rep_heads(w['gdn_a_log'][l]).reshape(1, GDN_WIDTH)
    p['dtb'] = _rep_heads(w['gdn_dt_bias'][l]).reshape(1, GDN_WIDTH)
    p['gdn_og'] = jnp.tile(w['gdn_out_g'][l], 2).reshape(1, LANES)
    p['ib'] = _rep_heads(w['mlstm_i_bias'][l]).reshape(1, MLSTM_WIDTH)
    p['fb'] = _rep_heads(w['mlstm_f_bias'][l]).reshape(1, MLSTM_WIDTH)
    p['mls_og'] = w['mlstm_out_g'][l].reshape(1, MLSTM_WIDTH)
    wuq = w['mla_w_uq'][l].reshape(MLA_Q_LORA, MLA_HEADS, MLA_QK)
    p['wuq'] = _bf(jnp.pad(wuq, ((0, 0), (0, 0), (0, _MLA_HW - MLA_QK))).reshape(MLA_Q_LORA, MLA_HEADS * _MLA_HW))
    p['wukv'] = _bf(w['mla_w_ukv'][l])
    p['qag'] = w['mla_q_a_g'][l].reshape(1, MLA_Q_LORA)
    p['kvag'] = w['mla_kv_a_g'][l].reshape(1, MLA_KV_LORA)
    p['qgain'] = jnp.pad(w['mla_q_gain'][l], (0, LANES - MLA_QK)).reshape(1, LANES)
    kg = w['mla_k_gain'][l]
    p['kgain'] = jnp.pad(kg[:MLA_NOPE], (0, LANES - MLA_NOPE)).reshape(1, LANES)
    p['krgain'] = jnp.pad(kg[MLA_NOPE:], (0, LANES - MLA_ROPE)).reshape(1, LANES)
    wo = w['w_out'][l]
    p['wo_a'] = _bf(wo[0:GDN_WIDTH])
    p['wo_b'] = _bf(wo[GDN_WIDTH:2 * GDN_WIDTH])
    p['wo_c'] = _bf(wo[2 * GDN_WIDTH:])
    p['wq'] = _bf(w['peer_w_q'][l])
    p['keys'] = _bf(w['peer_sub_keys'][l].reshape(2 * PEER_HEADS, N_KEYS, LANES))
    p['u_tab'] = _pack_table(w['peer_u'][l])
    p['v_tab'] = _pack_table(w['peer_v'][l])
    p['conv_w'] = w['gdn_conv_w'][l]
    p['norm_attn_g'] = w['norm_attn_g'][l].reshape(1, D_MODEL)
    p['norm_ffn_g'] = w['norm_ffn_g'][l].reshape(1, D_MODEL)
    return p


def _rope_tables(pos):
    half = MLA_ROPE // 2
    inv = ROPE_THETA ** (-jnp.arange(half, dtype=F32) / half)
    ang = pos.astype(F32)[:, None] * inv[None, :]
    cos, sin = jnp.cos(ang), jnp.sin(ang)
    n = pos.shape[0]
    c2 = jnp.concatenate([cos, cos], axis=1)
    s2 = jnp.concatenate([-sin, sin], axis=1)
    padq = lambda t: jnp.pad(t, ((0, 0), (MLA_NOPE, LANES - MLA_QK)))
    padk = lambda t: jnp.pad(t, ((0, 0), (0, LANES - MLA_ROPE)))
    return padq(c2), padq(s2), padk(c2), padk(s2)


def _pad_rows(a, n):
    return jnp.pad(a, ((0, 0), (0, n - a.shape[1]), (0, 0)))


def _layer(x, mod, p, st, rope, prompt):
    B, L, _ = x.shape
    sh1, sc1, g1, sh2, sc2, g2 = [m.reshape(B, 1, D_MODEL) for m in jnp.split(mod, 6, axis=-1)]
    pg, pm, pl_ = _inproj_call(x, sh1, sc1, p['norm_attn_g'], p['w_gdn'], p['w_mla'], p['w_mls'])
    Lp = -(-L // CHUNK) * CHUNK
    if Lp != L:
        pg, pl_ = _pad_rows(pg, Lp), _pad_rows(pl_, Lp)
    o_a, gdn_bd, conv_new = _gdn_call(pg, st['gdn_conv'], _to_bd(st['gdn']), p['conv_w'],
                                      p['alog'], p['dtb'], p['gdn_og'], L)
    o_c, mc_bd, mn_p, mm_p = _mlstm_call(
        pl_, _to_bd(st['mlstm_c']), st['mlstm_n'].reshape(B, MLSTM_HEADS // 2, 1, LANES),
        _rep_heads(st['mlstm_m']).reshape(B, MLSTM_HEADS // 2, 1, LANES), p['ib'], p['fb'], p['mls_og'], L)
    o_a, o_c = o_a[:, :L], o_c[:, :L]
    q, ckv, kr = _mla_q_call(pm, p['qag'], p['wuq'], p['qgain'], p['kvag'], p['krgain'], *rope)
    krp = lambda t: jnp.pad(t, ((0, 0), (0, 0), (MLA_NOPE, LANES - MLA_QK)))
    k, kv = _mla_kv_call(ckv, krp(kr), p['wukv'], p['kgain'])
    if not prompt:
        k_c, kv_c = _mla_kv_call(st['mla_latent'], krp(st['mla_krope']), p['wukv'], p['kgain'])
        k = jnp.concatenate([k_c, k], axis=1)
        kv = jnp.concatenate([kv_c, kv], axis=1)
    o_b = _attn_call(q, k, kv, prompt)
    x1, h2 = _outproj_call(o_a, o_b, o_c, x, g1, sh2, sc2, p['norm_ffn_g'], p['wo_a'], p['wo_b'], p['wo_c'])
    T = B * L
    h2f = h2.reshape(T, D_MODEL)
    idx, gate = _route_call(h2f, p['wq'], p['keys'])
    tb = min(L, 128)
    coef = _peer_u_call(idx, h2f, gate, p['u_tab'], tb)
    x2 = _peer_v_call(idx, coef, x1.reshape(T, D_MODEL), g2.reshape(B, 8, LANES), p['v_tab'], tb, L)
    new_state = (ckv, kr, _from_bd(gdn_bd), conv_new, _from_bd(mc_bd),
                 mn_p.reshape(B, MLSTM_HEADS, HEAD_W), mm_p.reshape(B, MLSTM_HEADS, HEAD_W)[:, :, 0])
    return x2.reshape(B, L, D_MODEL), new_state


def kernel(x_prompt, x_sample, c_prompt, c_sample, cache_mla_latent, cache_mla_krope, state_gdn, state_gdn_conv, state_mlstm_c, state_mlstm_n, state_mlstm_m, ada_w, ada_b, norm_attn_g, norm_ffn_g, w_in, gdn_conv_w, gdn_a_log, gdn_dt_bias, gdn_out_g, mla_q_a_g, mla_w_uq, mla_kv_a_g, mla_w_ukv, mla_q_gain, mla_k_gain, mlstm_i_bias, mlstm_f_bias, mlstm_out_g, w_out, peer_w_q, peer_sub_keys, peer_u, peer_v):
    w = dict(ada_w=ada_w, ada_b=ada_b, norm_attn_g=norm_attn_g, norm_ffn_g=norm_ffn_g, w_in=w_in,
             gdn_conv_w=gdn_conv_w, gdn_a_log=gdn_a_log, gdn_dt_bias=gdn_dt_bias, gdn_out_g=gdn_out_g,
             mla_q_a_g=mla_q_a_g, mla_w_uq=mla_w_uq, mla_kv_a_g=mla_kv_a_g, mla_w_ukv=mla_w_ukv,
             mla_q_gain=mla_q_gain, mla_k_gain=mla_k_gain, mlstm_i_bias=mlstm_i_bias,
             mlstm_f_bias=mlstm_f_bias, mlstm_out_g=mlstm_out_g, w_out=w_out, peer_w_q=peer_w_q,
             peer_sub_keys=peer_sub_keys, peer_u=peer_u, peer_v=peer_v)
    B, Lp, _ = x_prompt.shape
    Bs, Ls, _ = x_sample.shape
    past = cache_mla_latent.shape[2]
    rope_p = _rope_tables(jnp.arange(Lp, dtype=jnp.int32))
    rope_s = _rope_tables(past + jnp.arange(Ls, dtype=jnp.int32))
    mods = _mod_call(jnp.concatenate([c_prompt, c_sample], axis=0), ada_w, ada_b)
    xp, xs = x_prompt, x_sample
    new_p, new_s = [], []
    for l in range(DEPTH):
        p = _prep_layer(l, w)
        st_p = {
            'gdn': jnp.zeros((B, GDN_HEADS, HEAD_W, HEAD_W), F32),
            'gdn_conv': jnp.zeros((B, CONV_W - 1, GDN_QKV), F32),
            'mlstm_c': jnp.zeros((B, MLSTM_HEADS, HEAD_W, HEAD_W), F32),
            'mlstm_n': jnp.zeros((B, MLSTM_HEADS, HEAD_W), F32),
            'mlstm_m': jnp.zeros((B, MLSTM_HEADS), F32),
        }
        st_s = {
            'mla_latent': cache_mla_latent[l], 'mla_krope': cache_mla_krope[l],
            'gdn': state_gdn[l], 'gdn_conv': state_gdn_conv[l],
            'mlstm_c': state_mlstm_c[l], 'mlstm_n': state_mlstm_n[l], 'mlstm_m': state_mlstm_m[l],
        }
        xp, sp = _layer(xp, mods[l, :B], p, st_p, rope_p, True)
        xs, ss = _layer(xs, mods[l, B:], p, st_s, rope_s, False)
        new_p.append(sp)
        new_s.append(ss)
    outs_p = [jnp.stack([s[i] for s in new_p]) for i in range(7)]
    outs_s = [jnp.stack([s[i] for s in new_s]) for i in range(7)]
    return (xp, xs, *outs_p, *outs_s)
```

```python
import functools
import math

import jax
import jax.numpy as jnp
from jax import lax
from jax.experimental import pallas as pl
from jax.experimental.pallas import tpu as pltpu

F32 = jnp.float32
BF16 = jnp.bfloat16

D_MODEL = 1024
DEPTH = 2
CHUNK = 64
EPS = 1e-6
GDN_HEADS = 6
GDN_DK = 64
GDN_WIDTH = 384
GDN_QKV = 1152
CONV_W = 4
MLA_HEADS = 6
MLA_NOPE = 64
MLA_ROPE = 32
MLA_QK = 96
MLA_Q_LORA = 384
MLA_KV_LORA = 256
ROPE_THETA = 10000.0
MLSTM_HEADS = 4
MLSTM_WIDTH = 256
MLSTM_QKV = 768
PEER_HEADS = 8
N_KEYS = 128
N_EXPERTS = N_KEYS * N_KEYS
PEER_TOPK = 16
PEER_SLOTS = PEER_HEADS * PEER_TOPK

HEAD_W = 64
_LOG2_HEAD = HEAD_W.bit_length() - 1
_LOG2_CHUNK = CHUNK.bit_length() - 1
assert HEAD_W == 1 << _LOG2_HEAD and CHUNK == 1 << _LOG2_CHUNK and CHUNK == HEAD_W
LANES = 128
NEG = -1e30
VMEM_LIMIT = 56 * 1024 * 1024


def _bf(x):
    return x.astype(BF16)


def _mm(a, b):
    return jnp.dot(_bf(a), _bf(b), preferred_element_type=F32)


def _mm_nt(a, b):
    return lax.dot_general(_bf(a), _bf(b), (((1,), (1,)), ((), ())), preferred_element_type=F32)


def _split3(x):
    hi = _bf(x)
    r = x - hi.astype(F32)
    mid = _bf(r)
    lo = _bf(r - mid.astype(F32))
    return hi, mid, lo


def _mm_sel_exact(sel_bf, x):
    hi, mid, lo = _split3(x)
    d = lambda t: jnp.dot(sel_bf, t, preferred_element_type=F32)
    return d(hi) + d(mid) + d(lo)


def _sigmoid(x):
    return 1.0 / (1.0 + jnp.exp(-x))


def _silu(x):
    return x * _sigmoid(x)


def _softplus(x):
    return jnp.maximum(x, 0.0) + jnp.log1p(jnp.exp(-jnp.abs(x)))


def _iota(shape, axis):
    return lax.broadcasted_iota(jnp.int32, shape, axis)


def _pair_consts(C):
    row = _iota((C, LANES), 0)
    lane = _iota((C, LANES), 1)
    s = lane & (HEAD_W - 1)
    lo_half = lane < HEAD_W
    r2 = _iota((LANES, LANES), 0)
    l2 = _iota((LANES, LANES), 1)
    same_half = (r2 >> _LOG2_HEAD) == (l2 >> _LOG2_HEAD)
    return row, lane, s, lo_half, same_half


def _bd(y, lo_half):
    z = jnp.zeros_like(y)
    return jnp.concatenate([jnp.where(lo_half, y, z), jnp.where(lo_half, z, y)], axis=0)


def _diag_row(x, row, s):
    return jnp.sum(jnp.where(row == s, x, 0.0), axis=0, keepdims=True)


def _cparams(sem, vmem=None):
    kw = dict(dimension_semantics=sem)
    if vmem is not None:
        kw["vmem_limit_bytes"] = vmem
    return pltpu.CompilerParams(**kw)


def _mod_kernel(c_ref, w_ref, b_ref, o_ref):
    c = c_ref[...]
    o_ref[0] = _mm(_silu(c), w_ref[0]) + b_ref[0]


def _mod_call(c_all, ada_w, ada_b):
    nb = c_all.shape[0]
    nj = 6
    return pl.pallas_call(
        _mod_kernel,
        out_shape=jax.ShapeDtypeStruct((DEPTH, nb, 6 * D_MODEL), F32),
        grid=(DEPTH, nj),
        in_specs=[
            pl.BlockSpec((nb, D_MODEL), lambda l, j: (0, 0)),
            pl.BlockSpec((1, D_MODEL, D_MODEL), lambda l, j: (l, 0, j)),
            pl.BlockSpec((1, 1, D_MODEL), lambda l, j: (l, 0, j)),
        ],
        out_specs=pl.BlockSpec((1, nb, D_MODEL), lambda l, j: (l, 0, j)),
        compiler_params=_cparams(("arbitrary", "arbitrary")),
        name="adaln_mod",
    )(c_all, ada_w, ada_b.reshape(DEPTH, 1, 6 * D_MODEL))


def _inproj_kernel(x_ref, sh_ref, sc_ref, g_ref, wg_ref, wm_ref, wl_ref, og_ref, om_ref, ol_ref):
    x = x_ref[0]
    y = x * lax.rsqrt(jnp.mean(x * x, axis=-1, keepdims=True) + EPS) * g_ref[...]
    h = _bf(y * (1.0 + sc_ref[0]) + sh_ref[0])
    og_ref[0] = jnp.dot(h, wg_ref[...], preferred_element_type=F32)
    om_ref[0] = jnp.dot(h, wm_ref[...], preferred_element_type=F32)
    ol_ref[0] = jnp.dot(h, wl_ref[...], preferred_element_type=F32)


def _inproj_call(x, sh, sc, g, wg, wm, wl):
    B, L, _ = x.shape
    tb = min(L, 512)
    ng, nm, nl = wg.shape[1], wm.shape[1], wl.shape[1]
    full = lambda shape: pl.BlockSpec(shape, lambda b, i: (0,) * len(shape))
    per_b = pl.BlockSpec((1, 1, D_MODEL), lambda b, i: (b, 0, 0))
    return pl.pallas_call(
        _inproj_kernel,
        out_shape=(jax.ShapeDtypeStruct((B, L, ng), F32),
                   jax.ShapeDtypeStruct((B, L, nm), F32),
                   jax.ShapeDtypeStruct((B, L, nl), F32)),
        grid=(B, L // tb),
        in_specs=[pl.BlockSpec((1, tb, D_MODEL), lambda b, i: (b, i, 0)), per_b, per_b,
                  full((1, D_MODEL)), full(wg.shape), full(wm.shape), full(wl.shape)],
        out_specs=(pl.BlockSpec((1, tb, ng), lambda b, i: (b, i, 0)),
                   pl.BlockSpec((1, tb, nm), lambda b, i: (b, i, 0)),
                   pl.BlockSpec((1, tb, nl), lambda b, i: (b, i, 0))),
        compiler_params=_cparams(("arbitrary", "arbitrary"), VMEM_LIMIT),
        name="in_proj",
    )(x, sh, sc, g, wg, wm, wl)


_GDN_Z0 = GDN_QKV
_GDN_B0 = GDN_QKV + GDN_WIDTH
_GDN_A0 = _GDN_B0 + GDN_WIDTH
_GDN_COLS = _GDN_A0 + GDN_WIDTH
_CONV_PAD = 8


def _gdn_kernel(pg_ref, conv0_ref, s0_ref, convw_ref, alog_ref, dtb_ref, og_ref,
                o_ref, sfin_ref, convfin_ref, xp_ref, s_ref, *, C, cps, n_valid, n_steps):
    n = pl.program_id(1)
    hist = CONV_W - 1
    R = C * cps

    @pl.when(n == 0)
    def _():
        xp_ref[0:_CONV_PAD, :] = jnp.zeros((_CONV_PAD, GDN_QKV), F32)
        xp_ref[_CONV_PAD - hist:_CONV_PAD, :] = conv0_ref[0]
        s_ref[...] = s0_ref[0]

    a_pre = pg_ref[0, :, 0:GDN_QKV]
    xp_ref[_CONV_PAD:_CONV_PAD + R, :] = a_pre
    w = convw_ref[...]
    y = w[CONV_W - 1:CONV_W] * a_pre
    for j in range(CONV_W - 1):
        y = y + w[j:j + 1] * xp_ref[_CONV_PAD - hist + j:_CONV_PAD - hist + j + R, :]
    y = _silu(y)

    row, lane, s, lo_half, same_half = _pair_consts(C)
    ones_blk = jnp.where(same_half, 1.0, 0.0).astype(BF16)
    tr, tc = _iota((R, R), 0), _iota((R, R), 1)
    tri = jnp.where((tc <= tr) & ((tc >> _LOG2_CHUNK) == (tr >> _LOG2_CHUNK)), 1.0, 0.0).astype(BF16)
    causal = s <= row
    strict = s < row
    eye = s == row

    def hsum(x):
        return jnp.dot(_bf(x), ones_blk, preferred_element_type=F32)

    beta_all = _sigmoid(pg_ref[0, :, _GDN_B0:_GDN_A0])
    gl_all = -jnp.exp(alog_ref[...]) * _softplus(pg_ref[0, :, _GDN_A0:_GDN_COLS] + dtb_ref[...])
    if n_valid < R * n_steps:
        valid = (_iota((R, GDN_WIDTH), 0) + n * R) < n_valid
        beta_all = jnp.where(valid, beta_all, 0.0)
        gl_all = jnp.where(valid, gl_all, 0.0)
    g_all = _mm_sel_exact(tri, gl_all)

    items = [(c, j) for c in range(cps) for j in range(GDN_HEADS // 2)]
    I = range(len(items))
    blk = lambda a, base, i: a[C * items[i][0]:C * (items[i][0] + 1),
                               base + LANES * items[i][1]:base + LANES * (items[i][1] + 1)]
    bd = lambda a: _bd(a, lo_half)
    qk_raw = [jnp.concatenate([blk(y, 0, i), blk(y, GDN_WIDTH, i)], axis=0) for i in I]
    ssq = [hsum(t * t) for t in qk_raw]
    q2 = [qk_raw[i][0:C] * lax.rsqrt(ssq[i][0:C] + EPS) * (GDN_DK ** -0.5) for i in I]
    k2 = [qk_raw[i][C:2 * C] * lax.rsqrt(ssq[i][C:2 * C] + EPS) for i in I]
    v2 = [blk(y, 2 * GDN_WIDTH, i) for i in I]
    beta = [blk(beta_all, 0, i) for i in I]
    G = [blk(g_all, 0, i) for i in I]
    eG = [jnp.exp(G[i]) for i in I]
    decay = []
    for i in I:
        diff = G[i] - _diag_row(G[i], row, s)
        decay.append(jnp.where(causal, jnp.exp(jnp.where(causal, diff, 0.0)), 0.0))
    kbd = [bd(k2[i]) for i in I]
    kq = [_mm_nt(jnp.concatenate([k2[i], q2[i]], axis=0), kbd[i]) for i in I]
    A = [jnp.where(strict, beta[i] * kq[i][0:C] * decay[i], 0.0) for i in I]
    qk = [kq[i][C:2 * C] * decay[i] for i in I]
    first = ((row & 1) == 1) & (s == row - 1)
    T = [jnp.where(eye, 1.0, 0.0) - jnp.where(first, A[i], 0.0) for i in I]
    for lg in range(1, 6):
        sub = (((row >> lg) & 1) == 1) & ((s >> lg) == (row >> lg) - 1)
        t1 = [_mm(T[i], bd(jnp.where(sub, A[i], 0.0))) for i in I]
        t2 = [_mm(t1[i], bd(T[i])) for i in I]
        T = [T[i] - t2[i] for i in I]
    uw = [_mm(T[i], jnp.concatenate([bd(beta[i] * v2[i]), bd(beta[i] * eG[i] * k2[i])], axis=1)) for i in I]
    g_last = [G[i][C - 1:C, :] for i in I]
    kend_t = [(k2[i] * jnp.exp(g_last[i] - G[i])).T for i in I]
    P = range(GDN_HEADS // 2)
    S = [s_ref[j] for j in P]
    o = [None] * len(items)
    for c in range(cps):
        ids = [c * len(P) + j for j in P]
        ws = [_mm(jnp.concatenate([uw[i][:, LANES:2 * LANES], q2[i] * eG[i]], axis=0), S[j])
              for j, i in enumerate(ids)]
        wn = [uw[i][:, 0:LANES] - ws[j][0:C] for j, i in enumerate(ids)]
        upd = [_mm(kend_t[i], wn[j]) for j, i in enumerate(ids)]
        for j, i in enumerate(ids):
            o[i] = ws[j][C:2 * C] + _mm(qk[i], bd(wn[j]))
        S = [jnp.exp(g_last[i]) * S[j] + jnp.where(same_half, upd[j], 0.0) for j, i in enumerate(ids)]
    for j in P:
        s_ref[j] = S[j]
    oss = [hsum(o[i] * o[i]) for i in I]
    for i in I:
        c, j = items[i]
        on = o[i] * lax.rsqrt(oss[i] * (1.0 / HEAD_W) + EPS) * og_ref[...]
        z = pg_ref[0, C * c:C * (c + 1), _GDN_Z0 + LANES * j:_GDN_Z0 + LANES * (j + 1)]
        o_ref[0, C * c:C * (c + 1), LANES * j:LANES * (j + 1)] = on * _silu(z)

    @pl.when(n < n_steps - 1)
    def _():
        xp_ref[_CONV_PAD - hist:_CONV_PAD, :] = xp_ref[_CONV_PAD + R - hist:_CONV_PAD + R, :]

    @pl.when(n == n_steps - 1)
    def _():
        lv = n_valid - (n_steps - 1) * R
        convfin_ref[0] = xp_ref[_CONV_PAD + lv - hist:_CONV_PAD + lv, :]
        sfin_ref[0] = s_ref[...]


def _gdn_call(pg, conv0, s0_bd, conv_w, alog_rep, dtb_rep, og_rep, n_valid):
    B, Lp, _ = pg.shape
    C = CHUNK
    n_chunks = Lp // C
    cps = 4 if n_chunks % 4 == 0 else (2 if n_chunks % 2 == 0 else 1)
    n_steps = n_chunks // cps
    R = C * cps
    npair = GDN_HEADS // 2
    kern = functools.partial(_gdn_kernel, C=C, cps=cps, n_valid=n_valid, n_steps=n_steps)
    full = lambda shape: pl.BlockSpec(shape, lambda b, n: (0,) * len(shape))
    return pl.pallas_call(
        kern,
        out_shape=(jax.ShapeDtypeStruct((B, Lp, GDN_WIDTH), F32),
                   jax.ShapeDtypeStruct((B, npair, LANES, LANES), F32),
                   jax.ShapeDtypeStruct((B, CONV_W - 1, GDN_QKV), F32)),
        grid=(B, n_steps),
        in_specs=[pl.BlockSpec((1, R, _GDN_COLS), lambda b, n: (b, n, 0)),
                  pl.BlockSpec((1, CONV_W - 1, GDN_QKV), lambda b, n: (b, 0, 0)),
                  pl.BlockSpec((1, npair, LANES, LANES), lambda b, n: (b, 0, 0, 0)),
                  full((CONV_W, GDN_QKV)), full((1, GDN_WIDTH)), full((1, GDN_WIDTH)), full((1, LANES))],
        out_specs=(pl.BlockSpec((1, R, GDN_WIDTH), lambda b, n: (b, n, 0)),
                   pl.BlockSpec((1, npair, LANES, LANES), lambda b, n: (b, 0, 0, 0)),
                   pl.BlockSpec((1, CONV_W - 1, GDN_QKV), lambda b, n: (b, 0, 0))),
        scratch_shapes=[pltpu.VMEM((_CONV_PAD + R, GDN_QKV), F32),
                        pltpu.VMEM((npair, LANES, LANES), F32)],
        compiler_params=_cparams(("arbitrary", "arbitrary")),
        name="gdn_chunks",
    )(pg, conv0, s0_bd, conv_w, alog_rep, dtb_rep, og_rep)


_ML_O0 = MLSTM_QKV
_ML_I0 = _ML_O0 + MLSTM_WIDTH
_ML_F0 = _ML_I0 + MLSTM_WIDTH
_ML_COLS = _ML_F0 + MLSTM_WIDTH


def _mlstm_kernel(pm_ref, c0_ref, n0_ref, m0_ref, ib_ref, fb_ref, og_ref,
                  o_ref, cfin_ref, nfin_ref, mfin_ref, c_ref, n_ref, m_ref, *, C, cps, n_valid, n_steps):
    n = pl.program_id(1)
    R = C * cps

    @pl.when(n == 0)
    def _():
        c_ref[...] = c0_ref[0]
        n_ref[...] = n0_ref[0]
        m_ref[...] = m0_ref[0]

    row, lane, s, lo_half, same_half = _pair_consts(C)
    ones_blk = jnp.where(same_half, 1.0, 0.0).astype(BF16)
    tr, tc = _iota((R, R), 0), _iota((R, R), 1)
    tri = jnp.where((tc <= tr) & ((tc >> _LOG2_CHUNK) == (tr >> _LOG2_CHUNK)), 1.0, 0.0).astype(BF16)
    causal = s <= row

    def hsum(x):
        return jnp.dot(_bf(x), ones_blk, preferred_element_type=F32)

    ig_all = pm_ref[0, :, _ML_I0:_ML_F0] + ib_ref[...]
    lf_all = -_softplus(-(pm_ref[0, :, _ML_F0:_ML_COLS] + fb_ref[...]))
    if n_valid < R * n_steps:
        valid = (_iota((R, MLSTM_WIDTH), 0) + n * R) < n_valid
        ig_all = jnp.where(valid, ig_all, NEG)
        lf_all = jnp.where(valid, lf_all, 0.0)
    f_all = _mm_sel_exact(tri, lf_all)

    P = range(MLSTM_HEADS // 2)
    items = [(c, j) for c in range(cps) for j in P]
    I = range(len(items))
    col = lambda base, i: pm_ref[0, C * items[i][0]:C * (items[i][0] + 1),
                                 base + LANES * items[i][1]:base + LANES * (items[i][1] + 1)]
    sub = lambda a, i: a[C * items[i][0]:C * (items[i][0] + 1), LANES * items[i][1]:LANES * (items[i][1] + 1)]
    bd = lambda a: _bd(a, lo_half)
    q2 = [col(0, i) for i in I]
    k2 = [col(MLSTM_WIDTH, i) * (HEAD_W ** -0.5) for i in I]
    v2 = [col(2 * MLSTM_WIDTH, i) for i in I]
    ig = [sub(ig_all, i) for i in I]
    F = [sub(f_all, i) for i in I]
    dm, dmax = [], []
    for i in I:
        d = jnp.where(causal, F[i] - _diag_row(F[i], row, s) + _diag_row(ig[i], row, s), NEG)
        d_e = jnp.max(jnp.where(lo_half, d, NEG), axis=1, keepdims=True)
        d_o = jnp.max(jnp.where(lo_half, NEG, d), axis=1, keepdims=True)
        dm.append(d)
        dmax.append(jnp.where(lo_half, d_e, d_o))
    qk = [_mm_nt(q2[i], bd(k2[i])) for i in I]
    fe = [F[i][C - 1:C, :] for i in I]
    se = [fe[i] - F[i] + ig[i] for i in I]
    se_max = [jnp.max(se[i], axis=0, keepdims=True) for i in I]
    ms, m_new = [None] * len(items), [None] * len(items)
    m_run = [m_ref[j] for j in P]
    for i in I:
        j = items[i][1]
        ms[i] = m_run[j]
        m_new[i] = jnp.maximum(fe[i] + ms[i], se_max[i])
        m_run[j] = m_new[i]
    mt = [jnp.maximum(F[i] + ms[i], dmax[i]) for i in I]
    inter = [jnp.exp(F[i] + ms[i] - mt[i]) for i in I]
    wgt = [jnp.exp(dm[i] - mt[i]) * qk[i] for i in I]
    wv = [_mm(wgt[i], bd(v2[i])) for i in I]
    wsum = [hsum(wgt[i]) for i in I]
    kw = [k2[i] * jnp.exp(se[i] - m_new[i]) for i in I]
    upd = [_mm(kw[i].T, v2[i]) for i in I]
    ksum = [jnp.sum(kw[i], axis=0, keepdims=True) for i in I]
    sc = [jnp.exp(fe[i] + ms[i] - m_new[i]) for i in I]
    cs = [c_ref[j] for j in P]
    ns = [n_ref[j] for j in P]
    o = [None] * len(items)
    for i in I:
        j = items[i][1]
        den = inter[i] * hsum(q2[i] * ns[j]) + wsum[i]
        h = (inter[i] * _mm(q2[i], cs[j]) + wv[i]) / jnp.maximum(jnp.abs(den), jnp.exp(-mt[i]))
        o[i] = _sigmoid(col(_ML_O0, i)) * h
        cs[j] = sc[i] * cs[j] + jnp.where(same_half, upd[i], 0.0)
        ns[j] = sc[i] * ns[j] + ksum[i]
    oss = [hsum(o[i] * o[i]) for i in I]
    for j in P:
        c_ref[j] = cs[j]
        n_ref[j] = ns[j]
        m_ref[j] = m_run[j]
    for i in I:
        c, j = items[i]
        o_ref[0, C * c:C * (c + 1), LANES * j:LANES * (j + 1)] = (
            o[i] * lax.rsqrt(oss[i] * (1.0 / HEAD_W) + EPS) * og_ref[:, LANES * j:LANES * (j + 1)])

    @pl.when(n == n_steps - 1)
    def _():
        cfin_ref[0] = c_ref[...]
        nfin_ref[0] = n_ref[...]
        mfin_ref[0] = m_ref[...]


def _mlstm_call(pm, c0_bd, n0, m0_rep, ib_rep, fb_rep, og, n_valid):
    B, Lp, _ = pm.shape
    C = CHUNK
    n_chunks = Lp // C
    cps = 4 if n_chunks % 4 == 0 else (2 if n_chunks % 2 == 0 else 1)
    n_steps = n_chunks // cps
    R = C * cps
    npair = MLSTM_HEADS // 2
    kern = functools.partial(_mlstm_kernel, C=C, cps=cps, n_valid=n_valid, n_steps=n_steps)
    full = lambda shape: pl.BlockSpec(shape, lambda b, n: (0,) * len(shape))
    st_c = pl.BlockSpec((1, npair, LANES, LANES), lambda b, n: (b, 0, 0, 0))
    st_v = pl.BlockSpec((1, npair, 1, LANES), lambda b, n: (b, 0, 0, 0))
    return pl.pallas_call(
        kern,
        out_shape=(jax.ShapeDtypeStruct((B, Lp, MLSTM_WIDTH), F32),
                   jax.ShapeDtypeStruct((B, npair, LANES, LANES), F32),
                   jax.ShapeDtypeStruct((B, npair, 1, LANES), F32),
                   jax.ShapeDtypeStruct((B, npair, 1, LANES), F32)),
        grid=(B, n_steps),
        in_specs=[pl.BlockSpec((1, R, _ML_COLS), lambda b, n: (b, n, 0)), st_c, st_v, st_v,
                  full((1, MLSTM_WIDTH)), full((1, MLSTM_WIDTH)), full((1, MLSTM_WIDTH))],
        out_specs=(pl.BlockSpec((1, R, MLSTM_WIDTH), lambda b, n: (b, n, 0)), st_c, st_v, st_v),
        scratch_shapes=[pltpu.VMEM((npair, LANES, LANES), F32),
                        pltpu.VMEM((npair, 1, LANES), F32),
                        pltpu.VMEM((npair, 1, LANES), F32)],
        compiler_params=_cparams(("arbitrary", "arbitrary")),
        name="mlstm_chunks",
    )(pm, c0_bd, n0, m0_rep, ib_rep, fb_rep, og)


_MLA_C0 = MLA_Q_LORA
_MLA_R0 = MLA_Q_LORA + MLA_KV_LORA
_MLA_COLS = _MLA_R0 + LANES
_MLA_HW = LANES
_Q_SCALE = MLA_QK ** -0.5 * math.log2(math.e)


def _swap_halves(x, lane, base, half):
    up = pltpu.roll(x, LANES - half, 1)
    dn = pltpu.roll(x, half, 1)
    first = (lane >= base) & (lane < base + half)
    second = (lane >= base + half) & (lane < base + 2 * half)
    return jnp.where(first, up, jnp.where(second, dn, 0.0))


def _mla_q_kernel(pm_ref, qag_ref, wuq_ref, qgain_ref, kvag_ref, krgain_ref,
                  cosq_ref, sinq_ref, cosk_ref, sink_ref, q_ref, lat_ref, kr_ref, *, tb):
    lane = _iota((tb, LANES), 1)
    ql = pm_ref[0, :, 0:_MLA_C0]
    ql = ql * lax.rsqrt(jnp.mean(ql * ql, axis=-1, keepdims=True) + EPS) * qag_ref[...]
    q = jnp.dot(_bf(ql), wuq_ref[...], preferred_element_type=F32)
    nope = lane < MLA_NOPE
    rope = (lane >= MLA_NOPE) & (lane < MLA_QK)
    cosq = cosq_ref[...]
    sinq = sinq_ref[...]
    for h in range(MLA_HEADS):
        qh = q[:, _MLA_HW * h:_MLA_HW * (h + 1)]
        sq = qh * qh
        ss_n = jnp.sum(jnp.where(nope, sq, 0.0), axis=-1, keepdims=True) * (1.0 / MLA_NOPE)
        ss_r = jnp.sum(jnp.where(rope, sq, 0.0), axis=-1, keepdims=True) * (1.0 / MLA_ROPE)
        rinv = jnp.where(nope, lax.rsqrt(ss_n + EPS), lax.rsqrt(ss_r + EPS))
        qn = qh * rinv * qgain_ref[...]
        sw = _swap_halves(qn, lane, MLA_NOPE, MLA_ROPE // 2)
        qr = jnp.where(nope, qn, qn * cosq + sw * sinq)
        q_ref[0, :, _MLA_HW * h:_MLA_HW * (h + 1)] = _bf(qr * _Q_SCALE)
    ckv = pm_ref[0, :, _MLA_C0:_MLA_R0]
    lat_ref[0] = ckv * lax.rsqrt(jnp.mean(ckv * ckv, axis=-1, keepdims=True) + EPS) * kvag_ref[...]
    kx = pm_ref[0, :, _MLA_R0:_MLA_COLS]
    ssk = jnp.sum(kx * kx, axis=-1, keepdims=True) * (1.0 / MLA_ROPE)
    kn = kx * lax.rsqrt(ssk + EPS) * krgain_ref[...]
    swk = _swap_halves(kn, lane, 0, MLA_ROPE // 2)
    kr = kn * cosk_ref[...] + swk * sink_ref[...]
    kr_ref[0] = kr[:, 0:MLA_ROPE]


def _mla_q_call(pm, qag, wuq, qgain, kvag, krgain, cosq, sinq, cosk, sink):
    B, L, _ = pm.shape
    tb = min(L, 512)
    full = lambda shape: pl.BlockSpec(shape, lambda b, i: (0,) * len(shape))
    tab = pl.BlockSpec((tb, LANES), lambda b, i: (i, 0))
    nq = MLA_HEADS * _MLA_HW
    return pl.pallas_call(
        functools.partial(_mla_q_kernel, tb=tb),
        out_shape=(jax.ShapeDtypeStruct((B, L, nq), BF16),
                   jax.ShapeDtypeStruct((B, L, MLA_KV_LORA), F32),
                   jax.ShapeDtypeStruct((B, L, MLA_ROPE), F32)),
        grid=(B, L // tb),
        in_specs=[pl.BlockSpec((1, tb, _MLA_COLS), lambda b, i: (b, i, 0)),
                  full((1, MLA_Q_LORA)), full((MLA_Q_LORA, nq)), full((1, LANES)),
                  full((1, MLA_KV_LORA)), full((1, LANES)), tab, tab, tab, tab],
        out_specs=(pl.BlockSpec((1, tb, nq), lambda b, i: (b, i, 0)),
                   pl.BlockSpec((1, tb, MLA_KV_LORA), lambda b, i: (b, i, 0)),
                   pl.BlockSpec((1, tb, MLA_ROPE), lambda b, i: (b, i, 0))),
        compiler_params=_cparams(("arbitrary", "arbitrary")),
        name="mla_q_proj",
    )(pm, qag, wuq, qgain, kvag, krgain, cosq, sinq, cosk, sink)


def _mla_kv_kernel(ckv_ref, krp_ref, wukv_ref, kgain_ref, k_ref, kv_ref, *, tb):
    lane = _iota((tb, LANES), 1)
    nope = lane < MLA_NOPE
    kv = jnp.dot(_bf(ckv_ref[0]), wukv_ref[...], preferred_element_type=F32)
    krp = krp_ref[0]
    for h in range(MLA_HEADS):
        g = kv[:, _MLA_HW * h:_MLA_HW * (h + 1)]
        ss = jnp.sum(jnp.where(nope, g * g, 0.0), axis=-1, keepdims=True) * (1.0 / MLA_NOPE)
        k_ref[0, :, _MLA_HW * h:_MLA_HW * (h + 1)] = _bf(g * lax.rsqrt(ss + EPS) * kgain_ref[...] + krp)
        kv_ref[0, :, _MLA_HW * h:_MLA_HW * (h + 1)] = _bf(jnp.where(nope, 1.0, g))


def _mla_kv_call(ckv_all, krp_all, wukv, kgain):
    B, Lk, _ = ckv_all.shape
    tb = 512 if Lk % 512 == 0 else Lk
    nk = MLA_HEADS * _MLA_HW
    full = lambda shape: pl.BlockSpec(shape, lambda b, i: (0,) * len(shape))
    return pl.pallas_call(
        functools.partial(_mla_kv_kernel, tb=tb),
        out_shape=(jax.ShapeDtypeStruct((B, Lk, nk), BF16), jax.ShapeDtypeStruct((B, Lk, nk), BF16)),
        grid=(B, Lk // tb),
        in_specs=[pl.BlockSpec((1, tb, MLA_KV_LORA), lambda b, i: (b, i, 0)),
                  pl.BlockSpec((1, tb, LANES), lambda b, i: (b, i, 0)),
                  full((MLA_KV_LORA, nk)), full((1, LANES))],
        out_specs=(pl.BlockSpec((1, tb, nk), lambda b, i: (b, i, 0)),
                   pl.BlockSpec((1, tb, nk), lambda b, i: (b, i, 0))),
        compiler_params=_cparams(("arbitrary", "arbitrary"), VMEM_LIMIT),
        name="mla_kv_proj",
    )(ckv_all, krp_all, wukv, kgain)


def _attn_kernel(q_ref, k_ref, kv_ref, o_ref, m_ref, acc_ref, *, tq, tk, nk, causal):
    qi = pl.program_id(2)
    ki = pl.program_id(3)
    last = _last_kv_block(qi, tq, tk) if causal else nk - 1
    first_diag = (qi * tq) // tk

    @pl.when(ki == 0)
    def _():
        m_ref[...] = jnp.full(m_ref.shape, NEG, F32)
        acc_ref[...] = jnp.zeros(acc_ref.shape, F32)

    if tk % LANES == 0:
        widen = lambda v: jnp.concatenate([v] * (tk // LANES), axis=1)
    else:
        widen = lambda v: v[:, 0:1]

    def step(masked):
        if masked:
            qc = (qi * tq + _iota((tq, tk), 0)) >> _LOG2_CHUNK
            kc = (ki * tk + _iota((tq, tk), 1)) >> _LOG2_CHUNK
            keep = kc <= qc
        H = range(2)
        hs = [slice(_MLA_HW * hh, _MLA_HW * (hh + 1)) for hh in H]
        sc = [lax.dot_general(q_ref[0, :, hs[hh]], k_ref[0, :, hs[hh]], (((1,), (1,)), ((), ())),
                              preferred_element_type=F32) for hh in H]
        if masked:
            sc = [jnp.where(keep, t, NEG) for t in sc]
        m_prev = [m_ref[hh] for hh in H]
        m_new = [jnp.maximum(m_prev[hh], jnp.max(sc[hh], axis=-1, keepdims=True)) for hh in H]
        p = [jnp.exp2(sc[hh] - widen(m_new[hh])) for hh in H]
        alpha = [jnp.exp2(m_prev[hh] - m_new[hh]) for hh in H]
        pv = [jnp.dot(_bf(p[hh]), kv_ref[0, :, hs[hh]], preferred_element_type=F32) for hh in H]
        for hh in H:
            acc_ref[hh] = alpha[hh] * acc_ref[hh] + pv[hh]
            m_ref[hh] = m_new[hh]

    if causal:
        pl.when(ki < first_diag)(lambda: step(False))
        pl.when((ki >= first_diag) & (ki <= last))(lambda: step(True))
    else:
        step(False)

    @pl.when(ki == last)
    def _():
        lane = _iota((tq, LANES), 1)
        a_e, a_o = acc_ref[0], acc_ref[1]
        o_e = pltpu.roll(a_e, MLA_NOPE, 1) / a_e
        o_o = a_o / pltpu.roll(a_o, MLA_NOPE, 1)
        o_ref[0] = jnp.where(lane < MLA_NOPE, o_e, o_o)


def _last_kv_block(qi, tq, tk):
    return ((qi + 1) * tq - 1) // tk


def _attn_call(q, k, kv, causal):
    B, Lq, _ = q.shape
    Lk = k.shape[1]
    if causal:
        tq = min(Lq, 1024)
        tk = min(Lk, 1024)
    else:
        tq, tk = Lq, Lk
    nq, nk = Lq // tq, Lk // tk
    w2 = 2 * _MLA_HW
    if causal:
        kmap = lambda b, hp, qi, ki: (b, jnp.minimum(ki, _last_kv_block(qi, tq, tk)), hp)
    else:
        kmap = lambda b, hp, qi, ki: (b, ki, hp)
    return pl.pallas_call(
        functools.partial(_attn_kernel, tq=tq, tk=tk, nk=nk, causal=causal),
        out_shape=jax.ShapeDtypeStruct((B, Lq, MLA_HEADS * MLA_NOPE), F32),
        grid=(B, MLA_HEADS // 2, nq, nk),
        in_specs=[pl.BlockSpec((1, tq, w2), lambda b, hp, qi, ki: (b, qi, hp)),
                  pl.BlockSpec((1, tk, w2), kmap),
                  pl.BlockSpec((1, tk, w2), kmap)],
        out_specs=pl.BlockSpec((1, tq, LANES), lambda b, hp, qi, ki: (b, qi, hp)),
        scratch_shapes=[pltpu.VMEM((2, tq, LANES), F32), pltpu.VMEM((2, tq, LANES), F32)],
        compiler_params=_cparams(("arbitrary",) * 4, VMEM_LIMIT),
        name="mla_attention",
    )(q, k, kv)


def _outproj_kernel(a_ref, b_ref, c_ref, x_ref, g1_ref, sh_ref, sc_ref, ng_ref, wa_ref, wb_ref, wc_ref,
                    x1_ref, h2_ref):
    y = (jnp.dot(_bf(a_ref[0]), wa_ref[...], preferred_element_type=F32)
         + jnp.dot(_bf(b_ref[0]), wb_ref[...], preferred_element_type=F32)
         + jnp.dot(_bf(c_ref[0]), wc_ref[...], preferred_element_type=F32))
    x1 = x_ref[0] + g1_ref[0] * y
    x1_ref[0] = x1
    n = x1 * lax.rsqrt(jnp.mean(x1 * x1, axis=-1, keepdims=True) + EPS) * ng_ref[...]
    h2_ref[0] = n * (1.0 + sc_ref[0]) + sh_ref[0]


def _outproj_call(oa, ob, oc, x, g1, sh2, sc2, ng, wa, wb, wc):
    B, L, _ = x.shape
    tb = min(L, 512)
    full = lambda shape: pl.BlockSpec(shape, lambda b, i: (0,) * len(shape))
    per_b = pl.BlockSpec((1, 1, D_MODEL), lambda b, i: (b, 0, 0))
    blk = lambda wdt: pl.BlockSpec((1, tb, wdt), lambda b, i: (b, i, 0))
    return pl.pallas_call(
        _outproj_kernel,
        out_shape=(jax.ShapeDtypeStruct((B, L, D_MODEL), F32), jax.ShapeDtypeStruct((B, L, D_MODEL), F32)),
        grid=(B, L // tb),
        in_specs=[blk(GDN_WIDTH), blk(MLA_HEADS * MLA_NOPE), blk(MLSTM_WIDTH), blk(D_MODEL),
                  per_b, per_b, per_b, full((1, D_MODEL)), full(wa.shape), full(wb.shape), full(wc.shape)],
        out_specs=(blk(D_MODEL), blk(D_MODEL)),
        compiler_params=_cparams(("arbitrary", "arbitrary"), VMEM_LIMIT),
        name="out_proj",
    )(oa, ob, oc, x, g1, sh2, sc2, ng, wa, wb, wc)


def _topk_rows(svs, io, k, payloads=None):
    svs = list(svs)
    vals = [[] for _ in svs]
    outs = [[] for _ in svs]
    big = jnp.float32(1e9)
    for _ in range(k):
        for a in range(len(svs)):
            m = jnp.max(svs[a], axis=0, keepdims=True)
            ix = jnp.min(jnp.where(svs[a] == m, io, big), axis=0, keepdims=True)
            hit = io == ix
            svs[a] = jnp.where(hit, -jnp.inf, svs[a])
            vals[a].append(m)
            outs[a].append(ix if payloads is None
                           else jnp.max(jnp.where(hit, payloads[a], -1.0), axis=0, keepdims=True))
    return [(jnp.concatenate(v, axis=0), jnp.concatenate(o, axis=0)) for v, o in zip(vals, outs)]


_CAND_ROWS = PEER_TOPK + 7 * 8 + 8


def _cand_blocks(t1, t2, combine):
    blocks = [combine(t1[0:1], t2)]
    blocks += [combine(t1[a:a + 1], t2[0:8]) for a in range(1, 8)]
    blocks.append(combine(t1[8:PEER_TOPK], t2[0:1]))
    return jnp.concatenate(blocks, axis=0)


def _route_kernel(h_ref, wq_ref, keys_ref, e_ref, g_ref, qs_ref, es_ref, gs_ref, *, tb):
    q = jnp.dot(_bf(h_ref[...]), wq_ref[...], preferred_element_type=F32)
    for g in range(2 * PEER_HEADS):
        qs_ref[g] = _bf(q[:, LANES * g:LANES * (g + 1)])
    io_k = _iota((N_KEYS, tb), 0).astype(F32)
    r = _iota((_CAND_ROWS, tb), 0)
    mid = r - PEER_TOPK
    io_c = jnp.where(r < PEER_TOPK, r,
                     jnp.where(r < _CAND_ROWS - 8, ((mid >> 3) + 1) * PEER_TOPK + (mid & 7),
                               (r - (_CAND_ROWS - 16)) * PEER_TOPK)).astype(F32)

    def head(h, carry):
        scores = [lax.dot_general(keys_ref[2 * h + p], qs_ref[2 * h + p], (((1,), (1,)), ((), ())),
                                  preferred_element_type=F32) for p in range(2)]
        (v1, i1), (v2, i2) = _topk_rows(scores, io_k, PEER_TOPK)
        cand = _cand_blocks(v1, v2, lambda x, y: x + y)
        expert = _cand_blocks(i1, i2, lambda x, y: x * N_KEYS + y)
        (sc, e), = _topk_rows([cand], io_c, PEER_TOPK, payloads=[expert])
        ex = jnp.exp(sc - jnp.max(sc, axis=0, keepdims=True))
        es_ref[h] = e * _ROW_SUB
        gs_ref[h] = ex / jnp.sum(ex, axis=0, keepdims=True)
        return carry

    lax.fori_loop(0, PEER_HEADS, head, 0)
    e_ref[...] = es_ref[...].reshape(PEER_SLOTS, tb).T.astype(jnp.int32)
    g_ref[...] = gs_ref[...].reshape(PEER_SLOTS, tb).T


def _route_call(h2, wq, keys):
    T = h2.shape[0]
    tb = min(T, 256)
    full = lambda shape: pl.BlockSpec(shape, lambda i: (0,) * len(shape))
    return pl.pallas_call(
        functools.partial(_route_kernel, tb=tb),
        out_shape=(jax.ShapeDtypeStruct((T, PEER_SLOTS), jnp.int32), jax.ShapeDtypeStruct((T, PEER_SLOTS), F32)),
        grid=(T // tb,),
        in_specs=[pl.BlockSpec((tb, D_MODEL), lambda i: (i, 0)), full(wq.shape), full(keys.shape)],
        out_specs=(pl.BlockSpec((tb, PEER_SLOTS), lambda i: (i, 0)), pl.BlockSpec((tb, PEER_SLOTS), lambda i: (i, 0))),
        scratch_shapes=[pltpu.VMEM((2 * PEER_HEADS, tb, LANES), BF16),
                        pltpu.VMEM((PEER_HEADS, PEER_TOPK, tb), F32),
                        pltpu.VMEM((PEER_HEADS, PEER_TOPK, tb), F32)],
        compiler_params=_cparams(("arbitrary",), VMEM_LIMIT),
        name="peer_route",
    )(h2, wq, keys)


_ROW_SUB = 4
_GROUP = 8
_BITREV3 = (0, 4, 2, 6, 1, 5, 3, 7)


def _expert_row(tab_ref, row0):
    return pltpu.bitcast(tab_ref[pl.ds(row0, _ROW_SUB), :], BF16).astype(F32)


def _gelu(x):
    return 0.5 * x * (1.0 + lax.erf(x * (2.0 ** -0.5)))


def _fold(a, b, h, sub):
    m = (sub & h) == 0
    if h == 4:
        return jnp.where(m, a, b) + pltpu.roll(jnp.where(m, b, a), 4, 0)
    return jnp.where(m, a + pltpu.roll(a, 8 - h, 0), b + pltpu.roll(b, h, 0))


def _transpose8(v, sub):
    v = list(v)
    for h in (4, 2, 1):
        m = (sub & h) == 0
        for i in range(8):
            if i & h == 0:
                a, b = v[i], v[i | h]
                v[i] = jnp.where(m, a, pltpu.roll(b, h, 0))
                v[i | h] = jnp.where(m, pltpu.roll(a, 8 - h, 0), b)
    return v


def _rows_to_dense(x_ref, dense_ref, tb, sub):
    for g in range(tb // 8):
        cols = [x_ref[8 * g:8 * (g + 1), LANES * c:LANES * (c + 1)] for c in range(8)]
        for k, d in enumerate(_transpose8(cols, sub)):
            dense_ref[8 * g + k] = d


def _dense_to_rows(dense_ref, o_ref, tb, sub):
    for g in range(tb // 8):
        cols = _transpose8([dense_ref[8 * g + k] for k in range(8)], sub)
        for c in range(8):
            o_ref[8 * g:8 * (g + 1), LANES * c:LANES * (c + 1)] = cols[c]


def _peer_u_kernel(idx_ref, xrow_ref, g_ref, tab_ref, c_ref, part_ref, x_ref, *, tb):
    ones8 = jnp.ones((8, LANES), BF16)
    sub = _iota((8, LANES), 0)
    nt = lambda a, b: lax.dot_general(a, b, (((1,), (1,)), ((), ())), preferred_element_type=F32)
    _rows_to_dense(xrow_ref, x_ref, tb, sub)

    def gather(t, slot, k):
        xv = x_ref[t]
        buf = part_ref.at[slot]
        for g in range(PEER_SLOTS // _GROUP):
            p = [_expert_row(tab_ref, idx_ref[t, _GROUP * g + _BITREV3[i]]) * xv for i in range(_GROUP)]
            z = [_fold(p[2 * i], p[2 * i + 1], 4, sub) for i in range(4)]
            w = [_fold(z[0], z[1], 2, sub), _fold(z[2], z[3], 2, sub)]
            r0 = PEER_SLOTS * k + _GROUP * g
            buf[r0:r0 + _GROUP, :] = _fold(w[0], w[1], 1, sub)

    def finish(grp, slot):
        res = nt(ones8, _bf(part_ref[slot]))
        act = res[:, 0:LANES]
        for k in range(1, 8):
            act = jnp.where(sub == k, res[:, LANES * k:LANES * (k + 1)], act)
        rows = pl.ds(pl.multiple_of(grp * 8, 8), 8)
        c_ref[rows, :] = g_ref[rows, :] * _gelu(act)

    for k in range(8):
        gather(k, 0, k)

    def trip(grp, carry):
        slot = grp & 1
        finish(grp - 1, 1 - slot)
        for k in range(8):
            gather(grp * 8 + k, slot, k)
        return carry

    n_grp = tb // 8
    lax.fori_loop(1, n_grp, trip, 0)
    finish(n_grp - 1, (n_grp - 1) & 1)


def _peer_v_kernel(idx_ref, c_ref, xrow_ref, g2_ref, tab_ref, o_ref, m_ref, x_ref, *, tb):
    sub = _iota((8, LANES), 0)
    _rows_to_dense(xrow_ref, x_ref, tb, sub)
    eye = _iota((PEER_SLOTS, LANES), 0) == _iota((PEER_SLOTS, LANES), 1)
    diag = jnp.where(eye[None], c_ref[...][:, None, :], 0.0).reshape(tb * PEER_SLOTS, LANES)
    rep = jnp.dot(_bf(diag), jnp.ones((LANES, LANES), BF16), preferred_element_type=F32)
    m_ref[...] = rep.reshape(tb, PEER_SLOTS, LANES)

    def tok(t):
        y = jnp.zeros((8, LANES), F32)
        base = t * PEER_SLOTS
        for g in range(PEER_SLOTS // _GROUP):
            rows = [idx_ref[base + k] for k in range(_GROUP)]
            r = [m_ref[t, pl.ds(_GROUP * g + k, 1), :] * _expert_row(tab_ref, rows[k]) for k in range(_GROUP)]
            y = y + (((r[0] + r[1]) + (r[2] + r[3])) + ((r[4] + r[5]) + (r[6] + r[7])))
            base = base + _GROUP + lax.shift_right_arithmetic(rows[0], jnp.int32(31))
        x_ref[t] = x_ref[t] + g2_ref[0] * y

    per_trip = 2

    def trip(i, carry):
        for k in range(per_trip):
            tok(i * per_trip + k)
        return carry

    lax.fori_loop(0, tb // per_trip, trip, 0)
    _dense_to_rows(x_ref, o_ref, tb, sub)


def _table_spec():
    return pl.BlockSpec((N_EXPERTS * _ROW_SUB, LANES), lambda i: (0, 0), pipeline_mode=pl.Buffered(1))


def _peer_u_call(idx, x, gate, tab, tb):
    T = idx.shape[0]
    smem = pl.BlockSpec((tb, PEER_SLOTS), lambda i: (i, 0), memory_space=pltpu.SMEM)
    return pl.pallas_call(
        functools.partial(_peer_u_kernel, tb=tb),
        out_shape=jax.ShapeDtypeStruct((T, PEER_SLOTS), F32),
        grid=(T // tb,),
        in_specs=[smem, pl.BlockSpec((tb, D_MODEL), lambda i: (i, 0)),
                  pl.BlockSpec((tb, PEER_SLOTS), lambda i: (i, 0)), _table_spec()],
        out_specs=pl.BlockSpec((tb, PEER_SLOTS), lambda i: (i, 0)),
        scratch_shapes=[pltpu.VMEM((2, 8 * PEER_SLOTS, LANES), F32), pltpu.VMEM((tb, 8, LANES), F32)],
        compiler_params=_cparams(("arbitrary",), VMEM_LIMIT),
        name="peer_u",
    )(idx, x, gate, tab)


def _peer_v_call(idx, coef, x, g2, tab, tb, L):
    T = idx.shape[0]
    smem = pl.BlockSpec((tb * PEER_SLOTS,), lambda i: (i,), memory_space=pltpu.SMEM)
    per_step = L // tb
    idx = idx.reshape(T * PEER_SLOTS)
    return pl.pallas_call(
        functools.partial(_peer_v_kernel, tb=tb),
        out_shape=jax.ShapeDtypeStruct((T, D_MODEL), F32),
        grid=(T // tb,),
        in_specs=[smem, pl.BlockSpec((tb, PEER_SLOTS), lambda i: (i, 0)),
                  pl.BlockSpec((tb, D_MODEL), lambda i: (i, 0)),
                  pl.BlockSpec((1, 8, LANES), lambda i: (i // per_step, 0, 0)), _table_spec()],
        out_specs=pl.BlockSpec((tb, D_MODEL), lambda i: (i, 0)),
        scratch_shapes=[pltpu.VMEM((tb, PEER_SLOTS, LANES), F32), pltpu.VMEM((tb, 8, LANES), F32)],
        compiler_params=_cparams(("arbitrary",), VMEM_LIMIT),
        name="peer_v",
    )(idx, coef, x, g2, tab)


def _pack_table(tab):
    bits = lax.bitcast_convert_type(tab.astype(BF16), jnp.uint16).astype(jnp.uint32)
    bits = bits.reshape(tab.shape[0], _ROW_SUB, 2, LANES)
    return (bits[:, :, 0, :] | (bits[:, :, 1, :] << 16)).reshape(tab.shape[0] * _ROW_SUB, LANES)


def _rep_heads(w, width=HEAD_W):
    return jnp.repeat(w, width, axis=-1)


def _to_bd(s):
    B, H = s.shape[:2]
    s = s.reshape(B, H // 2, 2, HEAD_W, HEAD_W)
    z = jnp.zeros_like(s[:, :, 0])
    top = jnp.concatenate([s[:, :, 0], z], axis=-1)
    bot = jnp.concatenate([z, s[:, :, 1]], axis=-1)
    return jnp.concatenate([top, bot], axis=-2)


def _from_bd(s):
    B, P = s.shape[:2]
    return jnp.stack([s[:, :, :HEAD_W, :HEAD_W], s[:, :, HEAD_W:, HEAD_W:]], axis=2).reshape(B, 2 * P, HEAD_W, HEAD_W)


def _prep_layer(l, w):
    o = [0]
    for sz in (GDN_QKV, GDN_WIDTH, GDN_HEADS, GDN_HEADS, MLA_Q_LORA, MLA_KV_LORA, MLA_ROPE,
               MLSTM_QKV, MLSTM_WIDTH, MLSTM_HEADS, MLSTM_HEADS):
        o.append(o[-1] + sz)
    wi = w['w_in'][l]
    col = lambda i: wi[:, o[i]:o[i + 1]]
    p = {}
    p['w_gdn'] = _bf(jnp.concatenate([col(0), col(1), _rep_heads(col(2)), _rep_heads(col(3))], axis=1))
    p['w_mla'] = _bf(jnp.concatenate([col(4), col(5), col(6), jnp.zeros((D_MODEL, LANES - MLA_ROPE), F32)], axis=1))
    p['w_mls'] = _bf(jnp.concatenate([col(7), col(8), _rep_heads(col(9)), _rep_heads(col(10))], axis=1))
    p['alog'] = _rep_heads(w['gdn_a_log'][l]).reshape(1, GDN_WIDTH)
    p['dtb'] = _rep_heads(w['gdn_dt_bias'][l]).reshape(1, GDN_WIDTH)
    p['gdn_og'] = jnp.tile(w['gdn_out_g'][l], 2).reshape(1, LANES)
    p['ib'] = _rep_heads(w['mlstm_i_bias'][l]).reshape(1, MLSTM_WIDTH)
    p['fb'] = _rep_heads(w['mlstm_f_bias'][l]).reshape(1, MLSTM_WIDTH)
    p['mls_og'] = w['mlstm_out_g'][l].reshape(1, MLSTM_WIDTH)
    wuq = w['mla_w_uq'][l].reshape(MLA_Q_LORA, MLA_HEADS, MLA_QK)
    p['wuq'] = _bf(jnp.pad(wuq, ((0, 0), (0, 0), (0, _MLA_HW - MLA_QK))).reshape(MLA_Q_LORA, MLA_HEADS * _MLA_HW))
    p['wukv'] = _bf(w['mla_w_ukv'][l])
    p['qag'] = w['mla_q_a_g'][l].reshape(1, MLA_Q_LORA)
    p['kvag'] = w['mla_kv_a_g'][l].reshape(1, MLA_KV_LORA)
    p['qgain'] = jnp.pad(w['mla_q_gain'][l], (0, LANES - MLA_QK)).reshape(1, LANES)
    kg = w['mla_k_gain'][l]
    p['kgain'] = jnp.pad(kg[:MLA_NOPE], (0, LANES - MLA_NOPE)).reshape(1, LANES)
    p['krgain'] = jnp.pad(kg[MLA_NOPE:], (0, LANES - MLA_ROPE)).reshape(1, LANES)
    wo = w['w_out'][l]
    p['wo_a'] = _bf(wo[0:GDN_WIDTH])
    p['wo_b'] = _bf(wo[GDN_WIDTH:2 * GDN_WIDTH])
    p['wo_c'] = _bf(wo[2 * GDN_WIDTH:])
    p['wq'] = _bf(w['peer_w_q'][l])
    p['keys'] = _bf(w['peer_sub_keys'][l].reshape(2 * PEER_HEADS, N_KEYS, LANES))
    p['u_tab'] = _pack_table(w['peer_u'][l])
    p['v_tab'] = _pack_table(w['peer_v'][l])
    p['conv_w'] = w['gdn_conv_w'][l]
    p['norm_attn_g'] = w['norm_attn_g'][l].reshape(1, D_MODEL)
    p['norm_ffn_g'] = w['norm_ffn_g'][l].reshape(1, D_MODEL)
    return p


def _rope_tables(pos):
    half = MLA_ROPE // 2
    inv = ROPE_THETA ** (-jnp.arange(half, dtype=F32) / half)
    ang = pos.astype(F32)[:, None] * inv[None, :]
    cos, sin = jnp.cos(ang), jnp.sin(ang)
    n = pos.shape[0]
    c2 = jnp.concatenate([cos, cos], axis=1)
    s2 = jnp.concatenate([-sin, sin], axis=1)
    padq = lambda t: jnp.pad(t, ((0, 0), (MLA_NOPE, LANES - MLA_QK)))
    padk = lambda t: jnp.pad(t, ((0, 0), (0, LANES - MLA_ROPE)))
    return padq(c2), padq(s2), padk(c2), padk(s2)


def _pad_rows(a, n):
    return jnp.pad(a, ((0, 0), (0, n - a.shape[1]), (0, 0)))


def _layer(x, mod, p, st, rope, prompt):
    B, L, _ = x.shape
    sh1, sc1, g1, sh2, sc2, g2 = [m.reshape(B, 1, D_MODEL) for m in jnp.split(mod, 6, axis=-1)]
    pg, pm, pl_ = _inproj_call(x, sh1, sc1, p['norm_attn_g'], p['w_gdn'], p['w_mla'], p['w_mls'])
    Lp = -(-L // CHUNK) * CHUNK
    if Lp != L:
        pg, pl_ = _pad_rows(pg, Lp), _pad_rows(pl_, Lp)
    o_a, gdn_bd, conv_new = _gdn_call(pg, st['gdn_conv'], _to_bd(st['gdn']), p['conv_w'],
                                      p['alog'], p['dtb'], p['gdn_og'], L)
    o_c, mc_bd, mn_p, mm_p = _mlstm_call(
        pl_, _to_bd(st['mlstm_c']), st['mlstm_n'].reshape(B, MLSTM_HEADS // 2, 1, LANES),
        _rep_heads(st['mlstm_m']).reshape(B, MLSTM_HEADS // 2, 1, LANES), p['ib'], p['fb'], p['mls_og'], L)
    o_a, o_c = o_a[:, :L], o_c[:, :L]
    q, ckv, kr = _mla_q_call(pm, p['qag'], p['wuq'], p['qgain'], p['kvag'], p['krgain'], *rope)
    krp = lambda t: jnp.pad(t, ((0, 0), (0, 0), (MLA_NOPE, LANES - MLA_QK)))
    k, kv = _mla_kv_call(ckv, krp(kr), p['wukv'], p['kgain'])
    if not prompt:
        k_c, kv_c = _mla_kv_call(st['mla_latent'], krp(st['mla_krope']), p['wukv'], p['kgain'])
        k = jnp.concatenate([k_c, k], axis=1)
        kv = jnp.concatenate([kv_c, kv], axis=1)
    o_b = _attn_call(q, k, kv, prompt)
    x1, h2 = _outproj_call(o_a, o_b, o_c, x, g1, sh2, sc2, p['norm_ffn_g'], p['wo_a'], p['wo_b'], p['wo_c'])
    T = B * L
    h2f = h2.reshape(T, D_MODEL)
    idx, gate = _route_call(h2f, p['wq'], p['keys'])
    tb = min(L, 128)
    coef = _peer_u_call(idx, h2f, gate, p['u_tab'], tb)
    x2 = _peer_v_call(idx, coef, x1.reshape(T, D_MODEL), g2.reshape(B, 8, LANES), p['v_tab'], tb, L)
    new_state = (ckv, kr, _from_bd(gdn_bd), conv_new, _from_bd(mc_bd),
                 mn_p.reshape(B, MLSTM_HEADS, HEAD_W), mm_p.reshape(B, MLSTM_HEADS, HEAD_W)[:, :, 0])
    return x2.reshape(B, L, D_MODEL), new_state


def kernel(x_prompt, x_sample, c_prompt, c_sample, cache_mla_latent, cache_mla_krope, state_gdn, state_gdn_conv, state_mlstm_c, state_mlstm_n, state_mlstm_m, ada_w, ada_b, norm_attn_g, norm_ffn_g, w_in, gdn_conv_w, gdn_a_log, gdn_dt_bias, gdn_out_g, mla_q_a_g, mla_w_uq, mla_kv_a_g, mla_w_ukv, mla_q_gain, mla_k_gain, mlstm_i_bias, mlstm_f_bias, mlstm_out_g, w_out, peer_w_q, peer_sub_keys, peer_u, peer_v):
    w = dict(ada_w=ada_w, ada_b=ada_b, norm_attn_g=norm_attn_g, norm_ffn_g=norm_ffn_g, w_in=w_in,
             gdn_conv_w=gdn_conv_w, gdn_a_log=gdn_a_log, gdn_dt_bias=gdn_dt_bias, gdn_out_g=gdn_out_g,
             mla_q_a_g=mla_q_a_g, mla_w_uq=mla_w_uq, mla_kv_a_g=mla_kv_a_g, mla_w_ukv=mla_w_ukv,
             mla_q_gain=mla_q_gain, mla_k_gain=mla_k_gain, mlstm_i_bias=mlstm_i_bias,
             mlstm_f_bias=mlstm_f_bias, mlstm_out_g=mlstm_out_g, w_out=w_out, peer_w_q=peer_w_q,
             peer_sub_keys=peer_sub_keys, peer_u=peer_u, peer_v=peer_v)
    B, Lp, _ = x_prompt.shape
    Bs, Ls, _ = x_sample.shape
    past = cache_mla_latent.shape[2]
    rope_p = _rope_tables(jnp.arange(Lp, dtype=jnp.int32))
    rope_s = _rope_tables(past + jnp.arange(Ls, dtype=jnp.int32))
    mods = _mod_call(jnp.concatenate([c_prompt, c_sample], axis=0), ada_w, ada_b)
    xp, xs = x_prompt, x_sample
    new_p, new_s = [], []
    for l in range(DEPTH):
        p = _prep_layer(l, w)
        st_p = {
            'gdn': jnp.zeros((B, GDN_HEADS, HEAD_W, HEAD_W), F32),
            'gdn_conv': jnp.zeros((B, CONV_W - 1, GDN_QKV), F32),
            'mlstm_c': jnp.zeros((B, MLSTM_HEADS, HEAD_W, HEAD_W), F32),
            'mlstm_n': jnp.zeros((B, MLSTM_HEADS, HEAD_W), F32),
            'mlstm_m': jnp.zeros((B, MLSTM_HEADS), F32),
        }
        st_s = {
            'mla_latent': cache_mla_latent[l], 'mla_krope': cache_mla_krope[l],
            'gdn': state_gdn[l], 'gdn_conv': state_gdn_conv[l],
            'mlstm_c': state_mlstm_c[l], 'mlstm_n': state_mlstm_n[l], 'mlstm_m': state_mlstm_m[l],
        }
        xp, sp = _layer(xp, mods[l, :B], p, st_p, rope_p, True)
        xs, ss = _layer(xs, mods[l, B:], p, st_s, rope_s, False)
        new_p.append(sp)
        new_s.append(ss)
    outs_p = [jnp.stack([s[i] for s in new_p]) for i in range(7)]
    outs_s = [jnp.stack([s[i] for s in new_s]) for i in range(7)]
    return (xp, xs, *outs_p, *outs_s)
```
